```python
import math
import jax, jax.numpy as jnp
from jax import lax
import numpy as np

D_MODEL = 2048
BATCH = 16
SEQ = 256
DEPTH = 2
DEC_BATCH = 2
DEC_SEQ = 2048
PAST_LEN = 256

GRID_W = 64
MIX_W = D_MODEL // 4
ATT_HEADS = 4
ATT_KV_HEADS = 2
HEAD_DIM = MIX_W // ATT_HEADS
ROPE_THETA = 10000.0
Q_BLOCK = 128
S5_CH = 16
S5_GROUPS = MIX_W // S5_CH
S5_STATE = 64
MLSTM_HEADS = 4
MLSTM_DIM = MIX_W // MLSTM_HEADS
MLSTM_CHUNK = 128
GMLP_GROUPS = 4
GMLP_CHUNK = 128
D_FF = 7 * D_MODEL // 2
N_EXPERTS = 8
TOP_K = 2
N_DENSE = (DEPTH + 1) // 2
N_MOE = DEPTH // 2
MOD_CHUNKS = 6
EPS = 1e-6

MLSTM_GATE_W = 2 * 2 * MLSTM_HEADS
SPLIT_SIZES = (ATT_HEADS * HEAD_DIM, ATT_KV_HEADS * HEAD_DIM, ATT_KV_HEADS * HEAD_DIM,
               MIX_W, MIX_W, MIX_W, MIX_W, MIX_W, MLSTM_GATE_W, MIX_W, MIX_W)
IN_COLS = 3 * ATT_HEADS * HEAD_DIM // 2 + 7 * MIX_W + MLSTM_GATE_W + 0 * ATT_KV_HEADS
IN_COLS = ATT_HEADS * HEAD_DIM + 2 * ATT_KV_HEADS * HEAD_DIM + 7 * MIX_W + MLSTM_GATE_W

kernel_name = 'hybrid_diffusion_trunk_step'


def rms_norm(x, g):
    xf = x.astype(jnp.float32)
    y = xf * lax.rsqrt(jnp.mean(xf * xf, axis=-1, keepdims=True) + EPS)
    return (y * g.astype(jnp.float32)).astype(x.dtype)


def adaln_modulation(cond, w, b):
    return jax.nn.silu(cond) @ w + b


def axial_rope_tables(L):
    rows = L // GRID_W
    r = jnp.repeat(jnp.arange(rows, dtype=jnp.float32), GRID_W)
    col = (jnp.arange(L) % GRID_W).astype(jnp.float32)
    half = HEAD_DIM // 2
    inv = ROPE_THETA ** (-jnp.arange(0, half, 2, dtype=jnp.float32) / half)
    ang = jnp.concatenate([r[:, None] * inv, col[:, None] * inv], axis=-1)
    return jnp.cos(ang), jnp.sin(ang)


def apply_axial_rope(x, cos, sin):
    xf = x.astype(jnp.float32)
    qd = HEAD_DIM // 4
    def rot(z, c, s):
        z1, z2 = z[..., :qd], z[..., qd:]
        return jnp.concatenate([z1 * c - z2 * s, z2 * c + z1 * s], axis=-1)
    out = jnp.concatenate([rot(xf[..., :2 * qd], cos[:, :qd], sin[:, :qd]),
                           rot(xf[..., 2 * qd:], cos[:, qd:], sin[:, qd:])], axis=-1)
    return out.astype(x.dtype)


def block_attention(q, k, v):
    B_, Hq, Sq, hd = q.shape
    Hkv = k.shape[1]
    G = Hq // Hkv
    nb = Sq // Q_BLOCK
    qb = jnp.moveaxis(q.reshape(B_, Hkv, G, nb, Q_BLOCK, hd), 3, 0)
    def one_block(qi):
        s = jnp.einsum('bhgqd,bhkd->bhgqk', qi, k).astype(jnp.float32) * (hd ** -0.5)
        pr = jax.nn.softmax(s, axis=-1).astype(v.dtype)
        return jnp.einsum('bhgqk,bhkd->bhgqd', pr, v)
    ob = lax.map(one_block, qb)
    return jnp.moveaxis(ob, 0, 3).reshape(B_, Hq, Sq, hd)


def s5_discretise(a_re, a_im, log_dt, b_re, b_im):
    A = lax.complex(a_re.astype(jnp.float32), a_im.astype(jnp.float32))
    dt = jnp.exp(log_dt.astype(jnp.float32))[:, None]
    a_bar = jnp.exp(dt * A)
    b = lax.complex(b_re.astype(jnp.float32), b_im.astype(jnp.float32))
    b_bar = ((a_bar - 1.0) / A)[..., None] * b
    return a_bar, b_bar


def s5_scan(u, a_bar, b_bar, c_mat, h0):
    bu = jnp.einsum('blgi,gpi->blgp', u.astype(jnp.complex64), b_bar)
    bu = bu.at[:, 0].add(a_bar * h0)
    a = jnp.broadcast_to(a_bar, bu.shape)
    def combine(e1, e2):
        a1, b1 = e1
        a2, b2 = e2
        return a2 * a1, a2 * b1 + b2
    _, h = lax.associative_scan(combine, (a, bu), axis=1)
    y = jnp.einsum('blgp,gip->blgi', h, c_mat).real
    return y, h[:, -1]


def s5_mixer(sx, h0_re, h0_im, a_re, a_im, log_dt, b_re, b_im, c_re, c_im, d_skip, w_glu, b_glu):
    B_, L, _ = sx.shape
    u = sx.astype(jnp.float32).reshape(B_, L, S5_GROUPS, S5_CH)
    ys, finals = [], []
    for d in range(2):
        a_bar, b_bar = s5_discretise(a_re[d], a_im[d], log_dt[d], b_re[d], b_im[d])
        c_mat = lax.complex(c_re[d].astype(jnp.float32), c_im[d].astype(jnp.float32))
        h0 = lax.complex(h0_re[:, d].astype(jnp.float32), h0_im[:, d].astype(jnp.float32))
        if d == 0:
            y, hl = s5_scan(u, a_bar, b_bar, c_mat, h0)
        else:
            y, hl = s5_scan(jnp.flip(u, 1), a_bar, b_bar, c_mat, h0)
            y = jnp.flip(y, 1)
        ys.append(y)
        finals.append(hl)
    y = (ys[0] + ys[1]).reshape(B_, L, MIX_W) + d_skip.astype(jnp.float32) * sx.astype(jnp.float32)
    y = jax.nn.gelu(y)
    y = y * jax.nn.sigmoid(y @ w_glu.astype(jnp.float32) + b_glu.astype(jnp.float32))
    h_t = jnp.stack(finals, axis=1)
    return y.astype(sx.dtype), h_t.real, h_t.imag


def mlstm_chunkwise(q, k, v, log_i, log_f, c0, n0, m0):
    B_, H, L, _ = q.shape
    nc = L // MLSTM_CHUNK
    def chunks(t):
        return jnp.moveaxis(t.reshape(B_, H, nc, MLSTM_CHUNK, *t.shape[3:]), 2, 0)
    tri = jnp.tril(jnp.ones((MLSTM_CHUNK, MLSTM_CHUNK), dtype=bool))
    def step(carry, inp):
        c_st, n_st, m_st = carry
        qc, kc, vc, li, lf = inp
        b = jnp.cumsum(lf, axis=-1)
        dmat = jnp.where(tri, b[..., :, None] - b[..., None, :] + li[..., None, :], -jnp.inf)
        inter = b + m_st[..., None]
        m_j = jnp.maximum(inter, jnp.max(dmat, axis=-1))
        s = jnp.einsum('bhjd,bhsd->bhjs', qc, kc) * jnp.exp(dmat - m_j[..., None])
        w_inter = jnp.exp(inter - m_j)
        num = jnp.einsum('bhjs,bhse->bhje', s, vc) + w_inter[..., None] * jnp.einsum('bhjd,bhde->bhje', qc, c_st)
        den = jnp.sum(s, axis=-1) + w_inter * jnp.einsum('bhjd,bhd->bhj', qc, n_st)
        h = num / jnp.maximum(jnp.abs(den), jnp.exp(-m_j))[..., None]
        m_end = m_j[..., -1]
        w = jnp.exp(b[..., -1:] - b + li - m_end[..., None])
        decay = jnp.exp(b[..., -1] + m_st - m_end)
        c_new = decay[..., None, None] * c_st + jnp.einsum('bhs,bhsd,bhse->bhde', w, kc, vc)
        n_new = decay[..., None] * n_st + jnp.einsum('bhs,bhsd->bhd', w, kc)
        return (c_new, n_new, m_end), h
    (c_f, n_f, m_f), hs = lax.scan(step, (c0, n0, m0),
                                   (chunks(q), chunks(k), chunks(v), chunks(log_i), chunks(log_f)))
    h = jnp.moveaxis(hs, 0, 2).reshape(B_, H, L, v.shape[-1])
    return h, c_f, n_f, m_f


def mlstm_mixer(mq, mk, mv, mo, mg, c0, n0, m0, i_bias, f_bias, norm_g):
    B_, L, _ = mq.shape
    def heads(t):
        return t.astype(jnp.float32).reshape(B_, L, MLSTM_HEADS, MLSTM_DIM).transpose(0, 2, 1, 3)
    q = heads(mq)
    k = heads(mk) * (MLSTM_DIM ** -0.5)
    v = heads(mv)
    g = mg.astype(jnp.float32).reshape(B_, L, 2, 2, MLSTM_HEADS).transpose(2, 3, 0, 4, 1)
    hs, cs, ns, ms = [], [], [], []
    for d in range(2):
        li = g[d, 0] + i_bias[d].astype(jnp.float32)[None, :, None]
        lf = jax.nn.log_sigmoid(g[d, 1] + f_bias[d].astype(jnp.float32)[None, :, None])
        init = (c0[:, d].astype(jnp.float32), n0[:, d].astype(jnp.float32), m0[:, d].astype(jnp.float32))
        if d == 0:
            h, c_f, n_f, m_f = mlstm_chunkwise(q, k, v, li, lf, *init)
        else:
            h, c_f, n_f, m_f = mlstm_chunkwise(jnp.flip(q, 2), jnp.flip(k, 2), jnp.flip(v, 2),
                                               jnp.flip(li, 2), jnp.flip(lf, 2), *init)
            h = jnp.flip(h, 2)
        hs.append(h)
        cs.append(c_f)
        ns.append(n_f)
        ms.append(m_f)
    h = rms_norm(hs[0] + hs[1], norm_g[:, None, :])
    h = h.transpose(0, 2, 1, 3).reshape(B_, L, MIX_W) * jax.nn.sigmoid(mo.astype(jnp.float32))
    return (h.astype(mq.dtype), jnp.stack(cs, axis=1), jnp.stack(ns, axis=1), jnp.stack(ms, axis=1))


def chunk_gmlp(u, v, norm_g, w_s, b_s):
    B_, L, _ = v.shape
    nc = L // GMLP_CHUNK
    vn = rms_norm(v, norm_g).reshape(B_, nc, GMLP_CHUNK, GMLP_GROUPS, MIX_W // GMLP_GROUPS)
    mixed = jnp.einsum('gts,bcsgd->bctgd', w_s, vn) + b_s.T[None, None, :, :, None]
    return u * mixed.reshape(B_, L, MIX_W)


def swiglu(h, w_gate, w_up, w_down):
    return (jax.nn.silu(h @ w_gate) * (h @ w_up)) @ w_down


def moe_swiglu(h, router, router_bias, w_gate, w_up, w_down):
    logits = (h @ router).astype(jnp.float32) + router_bias.astype(jnp.float32)
    top_val, top_idx = lax.top_k(logits, TOP_K)
    top_w = jax.nn.softmax(top_val, axis=-1)
    gates = jnp.einsum('blk,blke->ble', top_w,
                       jax.nn.one_hot(top_idx, N_EXPERTS, dtype=jnp.float32)).astype(h.dtype)
    out = jnp.zeros_like(h)
    for e in range(N_EXPERTS):
        out = out + gates[..., e:e + 1] * swiglu(h, w_gate[e], w_up[e], w_down[e])
    return out


def trunk_layer(x, mod, p, use_moe, rope, att_k0, att_v0, s5_re0, s5_im0, ml_c0, ml_n0, ml_m0):
    B_, L, _ = x.shape
    shift1, scale1, gate1, shift2, scale2, gate2 = jnp.split(mod, MOD_CHUNKS, axis=-1)
    h = rms_norm(x, p['norm1_g']) * (1 + scale1) + shift1
    z = h @ p['w_in']
    offs = np.cumsum(SPLIT_SIZES)[:-1].tolist()
    aq, ak, av, sx, mq, mk, mv, mo, mg, gu, gv = jnp.split(z, offs, axis=-1)
    q = rms_norm(aq.reshape(B_, L, ATT_HEADS, HEAD_DIM).transpose(0, 2, 1, 3), p['q_norm_g'])
    k = rms_norm(ak.reshape(B_, L, ATT_KV_HEADS, HEAD_DIM).transpose(0, 2, 1, 3), p['k_norm_g'])
    v = av.reshape(B_, L, ATT_KV_HEADS, HEAD_DIM).transpose(0, 2, 1, 3)
    if rope is None:
        k_all, v_all = k, v
    else:
        cos, sin = rope
        q = apply_axial_rope(q, cos, sin)
        k = apply_axial_rope(k, cos, sin)
        k_all = jnp.concatenate([k, att_k0.astype(k.dtype)], axis=2)
        v_all = jnp.concatenate([v, att_v0.astype(v.dtype)], axis=2)
    att = block_attention(q, k_all, v_all).transpose(0, 2, 1, 3).reshape(B_, L, MIX_W)
    s5_out, s5_re, s5_im = s5_mixer(sx, s5_re0, s5_im0, p['s5_a_re'], p['s5_a_im'], p['s5_log_dt'],
                                    p['s5_b_re'], p['s5_b_im'], p['s5_c_re'], p['s5_c_im'],
                                    p['s5_d'], p['s5_w_glu'], p['s5_b_glu'])
    ml_out, ml_c, ml_n, ml_m = mlstm_mixer(mq, mk, mv, mo, mg, ml_c0, ml_n0, ml_m0,
                                           p['mlstm_i_bias'], p['mlstm_f_bias'], p['mlstm_norm_g'])
    gm_out = chunk_gmlp(gu, gv, p['gmlp_norm_g'], p['gmlp_w_s'], p['gmlp_b_s'])
    mix = jnp.concatenate([att, s5_out, ml_out, gm_out], axis=-1) @ p['w_out']
    x = x + gate1 * mix
    h2 = rms_norm(x, p['norm2_g']) * (1 + scale2) + shift2
    ffn = moe_swiglu(h2, *p['ffn']) if use_moe else swiglu(h2, *p['ffn'])
    x = x + gate2 * ffn
    return x, (k, v, s5_re, s5_im, ml_c, ml_n, ml_m)


def setup_inputs(seed: int = 0) -> dict:
    key = jax.random.key(seed)
    ks = iter(jax.random.split(key, 64))
    f32 = jnp.float32
    def nrm(shape, s=1.0):
        return s * jax.random.normal(next(ks), shape, f32)
    D = D_MODEL
    return {
        'x_prompt': nrm((BATCH, SEQ, D)),
        'x_sample': nrm((DEC_BATCH, DEC_SEQ, D)),
        'c': nrm((DEC_BATCH, D)),
        'c_ctx': nrm((D,)),
        'cache_attn_k': nrm((DEC_BATCH, DEPTH, ATT_KV_HEADS, PAST_LEN, HEAD_DIM)),
        'cache_attn_v': nrm((DEC_BATCH, DEPTH, ATT_KV_HEADS, PAST_LEN, HEAD_DIM)),
        'state_s5_re': nrm((DEC_BATCH, DEPTH, 2, S5_GROUPS, S5_STATE), 0.5),
        'state_s5_im': nrm((DEC_BATCH, DEPTH, 2, S5_GROUPS, S5_STATE), 0.5),
        'state_mlstm_c': nrm((DEC_BATCH, DEPTH, 2, MLSTM_HEADS, MLSTM_DIM, MLSTM_DIM), 0.1),
        'state_mlstm_n': nrm((DEC_BATCH, DEPTH, 2, MLSTM_HEADS, MLSTM_DIM), 0.1),
        'state_mlstm_m': nrm((DEC_BATCH, DEPTH, 2, MLSTM_HEADS)),
        'norm1_g': 1.0 + nrm((DEPTH, D), 0.02),
        'norm2_g': 1.0 + nrm((DEPTH, D), 0.02),
        'w_mod': nrm((DEPTH, D, MOD_CHUNKS * D), 0.5 * D ** -0.5),
        'b_mod': nrm((DEPTH, MOD_CHUNKS * D), 0.01),
        'w_in': nrm((DEPTH, D, IN_COLS), D ** -0.5),
        'w_out': nrm((DEPTH, D, D), D ** -0.5),
        'q_norm_g': 1.0 + nrm((DEPTH, HEAD_DIM), 0.02),
        'k_norm_g': 1.0 + nrm((DEPTH, HEAD_DIM), 0.02),
        's5_a_re': -0.5 * jnp.exp(nrm((DEPTH, 2, S5_GROUPS, S5_STATE), 0.02)),
        's5_a_im': math.pi * jnp.arange(S5_STATE, dtype=f32) + nrm((DEPTH, 2, S5_GROUPS, S5_STATE), 0.01),
        's5_log_dt': jax.random.uniform(next(ks), (DEPTH, 2, S5_GROUPS), f32, math.log(1e-3), math.log(1e-1)),
        's5_b_re': nrm((DEPTH, 2, S5_GROUPS, S5_STATE, S5_CH), (2 * S5_CH) ** -0.5),
        's5_b_im': nrm((DEPTH, 2, S5_GROUPS, S5_STATE, S5_CH), (2 * S5_CH) ** -0.5),
        's5_c_re': nrm((DEPTH, 2, S5_GROUPS, S5_CH, S5_STATE), (2 * S5_STATE) ** -0.5),
        's5_c_im': nrm((DEPTH, 2, S5_GROUPS, S5_CH, S5_STATE), (2 * S5_STATE) ** -0.5),
        's5_d': nrm((DEPTH, MIX_W)),
        's5_w_glu': nrm((DEPTH, MIX_W, MIX_W), MIX_W ** -0.5),
        's5_b_glu': nrm((DEPTH, MIX_W), 0.01),
        'mlstm_i_bias': nrm((DEPTH, 2, MLSTM_HEADS), 0.1),
        'mlstm_f_bias': jnp.linspace(3.0, 6.0, MLSTM_HEADS, dtype=f32) + nrm((DEPTH, 2, MLSTM_HEADS), 0.1),
        'mlstm_norm_g': 1.0 + nrm((DEPTH, MLSTM_HEADS, MLSTM_DIM), 0.02),
        'gmlp_norm_g': 1.0 + nrm((DEPTH, MIX_W), 0.02),
        'gmlp_w_s': nrm((DEPTH, GMLP_GROUPS, GMLP_CHUNK, GMLP_CHUNK), GMLP_CHUNK ** -0.5),
        'gmlp_b_s': 1.0 + nrm((DEPTH, GMLP_GROUPS, GMLP_CHUNK), 0.01),
        'ffn_w_gate': nrm((N_DENSE, D, D_FF), D ** -0.5),
        'ffn_w_up': nrm((N_DENSE, D, D_FF), D ** -0.5),
        'ffn_w_down': nrm((N_DENSE, D_FF, D), D_FF ** -0.5),
        'moe_router': nrm((N_MOE, D, N_EXPERTS), D ** -0.5),
        'moe_router_bias': nrm((N_MOE, N_EXPERTS), 0.01),
        'moe_w_gate': nrm((N_MOE, N_EXPERTS, D, D_FF), D ** -0.5),
        'moe_w_up': nrm((N_MOE, N_EXPERTS, D, D_FF), D ** -0.5),
        'moe_w_down': nrm((N_MOE, N_EXPERTS, D_FF, D), D_FF ** -0.5),
        'final_norm_g': 1.0 + nrm((D,), 0.02),
    }


def reference(x_prompt, x_sample, c, c_ctx, cache_attn_k, cache_attn_v, state_s5_re, state_s5_im,
              state_mlstm_c, state_mlstm_n, state_mlstm_m, norm1_g, norm2_g, w_mod, b_mod, w_in, w_out,
              q_norm_g, k_norm_g, s5_a_re, s5_a_im, s5_log_dt, s5_b_re, s5_b_im, s5_c_re, s5_c_im,
              s5_d, s5_w_glu, s5_b_glu, mlstm_i_bias, mlstm_f_bias, mlstm_norm_g, gmlp_norm_g,
              gmlp_w_s, gmlp_b_s, ffn_w_gate, ffn_w_up, ffn_w_down, moe_router, moe_router_bias,
              moe_w_gate, moe_w_up, moe_w_down, final_norm_g):
    bp = x_prompt.shape[0]
    f32 = jnp.float32
    s5_zero = jnp.zeros((bp, 2, S5_GROUPS, S5_STATE), f32)
    ml_c_zero = jnp.zeros((bp, 2, MLSTM_HEADS, MLSTM_DIM, MLSTM_DIM), f32)
    ml_n_zero = jnp.zeros((bp, 2, MLSTM_HEADS, MLSTM_DIM), f32)
    ml_m_zero = jnp.zeros((bp, 2, MLSTM_HEADS), f32)
    rope = axial_rope_tables(x_sample.shape[1])
    xp, xs = x_prompt, x_sample
    ctx_states = []
    for l in range(DEPTH):
        use_moe = (l % 2 == 1)
        j = l // 2
        if use_moe:
            ffn = (moe_router[j], moe_router_bias[j], moe_w_gate[j], moe_w_up[j], moe_w_down[j])
        else:
            ffn = (ffn_w_gate[j], ffn_w_up[j], ffn_w_down[j])
        p = {'norm1_g': norm1_g[l], 'norm2_g': norm2_g[l], 'w_in': w_in[l], 'w_out': w_out[l],
             'q_norm_g': q_norm_g[l], 'k_norm_g': k_norm_g[l],
             's5_a_re': s5_a_re[l], 's5_a_im': s5_a_im[l], 's5_log_dt': s5_log_dt[l],
             's5_b_re': s5_b_re[l], 's5_b_im': s5_b_im[l], 's5_c_re': s5_c_re[l], 's5_c_im': s5_c_im[l],
             's5_d': s5_d[l], 's5_w_glu': s5_w_glu[l], 's5_b_glu': s5_b_glu[l],
             'mlstm_i_bias': mlstm_i_bias[l], 'mlstm_f_bias': mlstm_f_bias[l], 'mlstm_norm_g': mlstm_norm_g[l],
             'gmlp_norm_g': gmlp_norm_g[l], 'gmlp_w_s': gmlp_w_s[l], 'gmlp_b_s': gmlp_b_s[l],
             'ffn': ffn}
        mod_ctx = adaln_modulation(c_ctx[None, :], w_mod[l], b_mod[l])[:, None, :]
        xp, st = trunk_layer(xp, mod_ctx, p, use_moe, None, None, None,
                             s5_zero, s5_zero, ml_c_zero, ml_n_zero, ml_m_zero)
        ctx_states.append(st)
        mod_lat = adaln_modulation(c, w_mod[l], b_mod[l])[:, None, :]
        xs, _ = trunk_layer(xs, mod_lat, p, use_moe, rope, cache_attn_k[:, l], cache_attn_v[:, l],
                            state_s5_re[:, l], state_s5_im[:, l], state_mlstm_c[:, l],
                            state_mlstm_n[:, l], state_mlstm_m[:, l])
    y_prompt = rms_norm(xp, final_norm_g)
    y_sample = rms_norm(xs, final_norm_g)
    new_attn_k = jnp.stack([s[0] for s in ctx_states], axis=1)
    new_attn_v = jnp.stack([s[1] for s in ctx_states], axis=1)
    new_s5_re = jnp.stack([s[2] for s in ctx_states], axis=1)
    new_s5_im = jnp.stack([s[3] for s in ctx_states], axis=1)
    new_mlstm_c = jnp.stack([s[4] for s in ctx_states], axis=1)
    new_mlstm_n = jnp.stack([s[5] for s in ctx_states], axis=1)
    new_mlstm_m = jnp.stack([s[6] for s in ctx_states], axis=1)
    return (y_prompt, y_sample, new_attn_k, new_attn_v, new_s5_re, new_s5_im, new_mlstm_c, new_mlstm_n, new_mlstm_m)
```

```python
import functools
import math

import numpy as np
import jax
import jax.numpy as jnp
from jax import lax
from jax.experimental import pallas as pl
from jax.experimental.pallas import tpu as pltpu

F32 = jnp.float32
BF16 = jnp.bfloat16

D_MODEL = 2048
BATCH = 16
SEQ = 256
DEPTH = 2
DEC_BATCH = 2
DEC_SEQ = 2048
PAST_LEN = 256
GRID_W = 64
MIX_W = 512
ATT_HEADS = 4
ATT_KV_HEADS = 2
HEAD_DIM = 128
ROPE_THETA = 10000.0
S5_CH = 16
S5_GROUPS = 32
S5_STATE = 64
MLSTM_HEADS = 4
MLSTM_DIM = 128
CHUNK = 128
GMLP_GROUPS = 4
D_FF = 7168
N_EXPERTS = 8
MOD_CHUNKS = 6
EPS = 1e-6

T_CTX = BATCH * SEQ
T_LAT = DEC_BATCH * DEC_SEQ
T_ALL = T_CTX + T_LAT
N_GROUPS_MOD = 1 + DEC_BATCH

Z_AQ, Z_AK, Z_AV, Z_SX = 0, 512, 768, 1024
Z_MQ, Z_MK, Z_MV, Z_MO = 1536, 2048, 2560, 3072
Z_GU, Z_GV, Z_MG = 3584, 4096, 4608
Z_COLS = 4736
LANE = 128
SUBLANE = 8

VMEM_LIMIT = 56 * 1024 * 1024

S5_SG = 4
S5_SGW = 8 * S5_STATE
S5_LAGS = 8

FFN_TM = 1024
FFN_SUB = 256
FFN_TF = 512
MOE_TILES = 2 * T_ALL // FFN_TM + N_EXPERTS
GATHER_WINDOW = 16


def _cparams(sem=None):
    return pltpu.CompilerParams(dimension_semantics=sem, vmem_limit_bytes=VMEM_LIMIT)


def _mod_group(i, tm):
    return jnp.maximum(i * tm // DEC_SEQ - (T_CTX // DEC_SEQ - 1), 0)


def _rms(x, g):
    return x * lax.rsqrt(jnp.mean(x * x, axis=-1, keepdims=True) + EPS) * g


MOD_TN = 512


def _mod_kernel(cb_ref, w_ref, b_ref, o_ref):
    w = w_ref[0]
    rows = []
    for r in range(N_GROUPS_MOD):
        c = cb_ref[r]
        s = c * jax.nn.sigmoid(c)
        parts = [jnp.sum(w[:, j * LANE:(j + 1) * LANE] * s, axis=0, keepdims=True)
                 for j in range(MOD_TN // LANE)]
        rows.append(jnp.concatenate(parts, axis=1))
    rows.append(jnp.zeros((SUBLANE - N_GROUPS_MOD, MOD_TN), F32))
    o_ref[0] = jnp.concatenate(rows, axis=0) + b_ref[0]


def _modulation(cond, w_mod, b_mod):
    cb = jnp.broadcast_to(cond[:, :, None], (N_GROUPS_MOD, D_MODEL, LANE))
    n = MOD_CHUNKS * D_MODEL
    out = pl.pallas_call(
        _mod_kernel,
        grid=(DEPTH, n // MOD_TN),
        in_specs=[pl.BlockSpec((N_GROUPS_MOD, D_MODEL, LANE), lambda l, j: (0, 0, 0)),
                  pl.BlockSpec((1, D_MODEL, MOD_TN), lambda l, j: (l, 0, j)),
                  pl.BlockSpec((1, 1, MOD_TN), lambda l, j: (l, 0, j))],
        out_specs=pl.BlockSpec((1, SUBLANE, MOD_TN), lambda l, j: (l, 0, j)),
        out_shape=jax.ShapeDtypeStruct((DEPTH, SUBLANE, n), F32),
        compiler_params=_cparams(("arbitrary", "arbitrary")),
        name="adaln_mod",
    )(cb, w_mod, b_mod.reshape(DEPTH, 1, n))
    return out[:, :N_GROUPS_MOD].reshape(DEPTH, N_GROUPS_MOD, MOD_CHUNKS, D_MODEL)


IN_TM = 256
IN_CHUNK = 512


def _in_kernel(x_ref, mod_ref, g_ref, w_ref, z_ref):
    x = x_ref[...]
    shift = mod_ref[0, 0:1, :]
    scale = mod_ref[0, 1:2, :]
    h = (_rms(x, g_ref[...]) * (1.0 + scale) + shift).astype(BF16)
    for c0 in range(0, Z_COLS, IN_CHUNK):
        cw = min(IN_CHUNK, Z_COLS - c0)
        z_ref[:, c0:c0 + cw] = jnp.dot(h, w_ref[:, c0:c0 + cw], preferred_element_type=F32)


def _in_proj(x, mod_l, g, w_p):
    return pl.pallas_call(
        _in_kernel,
        grid=(T_ALL // IN_TM,),
        in_specs=[pl.BlockSpec((IN_TM, D_MODEL), lambda i: (i, 0)),
                  pl.BlockSpec((1, MOD_CHUNKS, D_MODEL), lambda i: (_mod_group(i, IN_TM), 0, 0)),
                  pl.BlockSpec((1, D_MODEL), lambda i: (0, 0)),
                  pl.BlockSpec((D_MODEL, Z_COLS), lambda i: (0, 0), pipeline_mode=pl.Buffered(1))],
        out_specs=pl.BlockSpec((IN_TM, Z_COLS), lambda i: (i, 0)),
        out_shape=jax.ShapeDtypeStruct((T_ALL, Z_COLS), F32),
        compiler_params=_cparams(("arbitrary",)),
        name="in_proj",
    )(x, mod_l, g.reshape(1, D_MODEL), w_p)


def _rope(t, c, s):
    lane = lax.broadcasted_iota(jnp.int32, t.shape, 1)
    first = (lane % (HEAD_DIM // 2)) < (HEAD_DIM // 4)
    swapped = jnp.where(first, pltpu.roll(t, HEAD_DIM - HEAD_DIM // 4, 1), pltpu.roll(t, HEAD_DIM // 4, 1))
    return t * c + swapped * s


def _attn_kernel(*refs, seq, past, rope):
    if rope:
        (aq_ref, ak_ref, av_ref, kp_ref, vp_ref, cos_ref, sin_ref, qg_ref, kg_ref,
         att_ref, kb_scr, vb_scr) = refs
    else:
        aq_ref, ak_ref, av_ref, qg_ref, kg_ref, att_ref, knew_ref, vnew_ref, kb_scr, vb_scr = refs
    kn = _rms(ak_ref[...], kg_ref[...])
    v = av_ref[...]
    if rope:
        kn = _rope(kn, cos_ref[...], sin_ref[...])
        kb_scr[seq:seq + past, :] = kp_ref[0, 0].astype(BF16)
        vb_scr[seq:seq + past, :] = vp_ref[0, 0].astype(BF16)
    else:
        knew_ref[0, 0] = kn
        vnew_ref[0, 0] = v
    kb_scr[0:seq, :] = kn.astype(BF16)
    vb_scr[0:seq, :] = v.astype(BF16)
    grp = ATT_HEADS // ATT_KV_HEADS

    def q_block(qb, carry):
        rows = pl.ds(pl.multiple_of(qb * CHUNK, CHUNK), CHUNK)
        qs = []
        for g in range(grp):
            q = _rms(aq_ref[rows, g * HEAD_DIM:(g + 1) * HEAD_DIM], qg_ref[...])
            if rope:
                q = _rope(q, cos_ref[rows, :], sin_ref[rows, :])
            qs.append(q)
        q2 = jnp.concatenate(qs, axis=0).astype(BF16)
        s = lax.dot_general(q2, kb_scr[...], (((1,), (1,)), ((), ())),
                            preferred_element_type=F32) * (HEAD_DIM ** -0.5)
        m = jnp.max(s, axis=-1, keepdims=True)
        p = jnp.exp(s - m)
        den = jnp.sum(p, axis=-1, keepdims=True)
        o = jnp.dot(p.astype(BF16), vb_scr[...], preferred_element_type=F32) / den
        for g in range(grp):
            att_ref[rows, g * HEAD_DIM:(g + 1) * HEAD_DIM] = o[g * CHUNK:(g + 1) * CHUNK].astype(BF16)
        return carry

    lax.fori_loop(0, seq // CHUNK, q_block, 0)


def _attention(z, att_out_shape, qg, kg, *, ctx, kpast=None, vpast=None, cos=None, sin=None):
    seq = SEQ if ctx else DEC_SEQ
    nb = BATCH if ctx else DEC_BATCH
    row0 = 0 if ctx else T_CTX // DEC_SEQ
    past = 0 if ctx else PAST_LEN
    qw = HEAD_DIM * (ATT_HEADS // ATT_KV_HEADS)
    in_specs = [pl.BlockSpec((seq, qw), lambda b, h: (row0 + b, Z_AQ // qw + h)),
                pl.BlockSpec((seq, HEAD_DIM), lambda b, h: (row0 + b, Z_AK // HEAD_DIM + h)),
                pl.BlockSpec((seq, HEAD_DIM), lambda b, h: (row0 + b, Z_AV // HEAD_DIM + h))]
    args = [z, z, z]
    if not ctx:
        in_specs += [pl.BlockSpec((1, 1, past, HEAD_DIM), lambda b, h: (b, h, 0, 0)),
                     pl.BlockSpec((1, 1, past, HEAD_DIM), lambda b, h: (b, h, 0, 0)),
                     pl.BlockSpec((seq, HEAD_DIM), lambda b, h: (0, 0)),
                     pl.BlockSpec((seq, HEAD_DIM), lambda b, h: (0, 0))]
        args += [kpast, vpast, cos, sin]
    in_specs += [pl.BlockSpec((1, HEAD_DIM), lambda b, h: (0, 0)),
                 pl.BlockSpec((1, HEAD_DIM), lambda b, h: (0, 0))]
    args += [qg.reshape(1, HEAD_DIM), kg.reshape(1, HEAD_DIM)]
    out_specs = [pl.BlockSpec((seq, qw), lambda b, h: (b, h))]
    out_shape = [jax.ShapeDtypeStruct((nb * seq, MIX_W), BF16)]
    if ctx:
        out_specs += [pl.BlockSpec((1, 1, seq, HEAD_DIM), lambda b, h: (b, h, 0, 0))] * 2
        out_shape += [jax.ShapeDtypeStruct((nb, ATT_KV_HEADS, seq, HEAD_DIM), F32)] * 2
    return pl.pallas_call(
        functools.partial(_attn_kernel, seq=seq, past=past, rope=not ctx),
        grid=(nb, ATT_KV_HEADS),
        in_specs=in_specs, out_specs=out_specs, out_shape=out_shape,
        scratch_shapes=[pltpu.VMEM((seq + past, HEAD_DIM), BF16),
                        pltpu.VMEM((seq + past, HEAD_DIM), BF16)],
        compiler_params=_cparams(("arbitrary", "arbitrary")),
        name="attn_ctx" if ctx else "attn_lat",
    )(*args)


def _s5_prep_kernel(are_ref, aim_ref, ldt_ref, bre_ref, bim_ref, pre_ref, pim_ref, wre_ref, wim_ref):
    a_re = are_ref[...]
    a_im = aim_ref[...]
    dt = jnp.exp(ldt_ref[...])
    pows = []
    for tau in range(S5_LAGS + 1):
        mag = jnp.exp((tau * dt) * a_re)
        ang = (tau * dt) * a_im
        pr, pi = mag * jnp.cos(ang), mag * jnp.sin(ang)
        pre_ref[tau] = pr
        pim_ref[tau] = pi
        pows.append((pr, pi))
    nr, ni = pows[1][0] - 1.0, pows[1][1]
    den = a_re * a_re + a_im * a_im
    cr = (nr * a_re + ni * a_im) / den
    ci = (ni * a_re - nr * a_im) / den
    for d in range(2):
        b_r, b_i = bre_ref[d], bim_ref[d]
        bb_r = cr[d:d + 1] * b_r - ci[d:d + 1] * b_i
        bb_i = cr[d:d + 1] * b_i + ci[d:d + 1] * b_r
        for tau in range(S5_LAGS):
            pr, pi = pows[tau][0][d:d + 1], pows[tau][1][d:d + 1]
            wre_ref[d, tau] = pr * bb_r - pi * bb_i
            wim_ref[d, tau] = pr * bb_i + pi * bb_r


def _s5_prep(a_re, a_im, log_dt, b_re, b_im, c_re, c_im):
    gp = S5_GROUPS * S5_STATE
    ldt = jnp.broadcast_to(log_dt[:, :, None], (2, S5_GROUPS, S5_STATE)).reshape(2, gp)
    bt = lambda b: b.transpose(0, 3, 1, 2).reshape(2, S5_CH, gp)
    pre, pim, wre, wim = pl.pallas_call(
        _s5_prep_kernel,
        out_shape=[jax.ShapeDtypeStruct((S5_LAGS + 1, 2, gp), F32)] * 2
        + [jax.ShapeDtypeStruct((2, S5_LAGS, S5_CH, gp), F32)] * 2,
        name="s5_prep",
    )(a_re.reshape(2, gp), a_im.reshape(2, gp), ldt, bt(b_re), bt(b_im))
    eye = jnp.eye(8, dtype=F32)

    def w_layout(w):
        w = w.reshape(2, S5_LAGS, S5_CH, S5_SG, 8, S5_STATE).transpose(0, 3, 1, 4, 2, 5)
        w = w[:, :, :, :, :, None, :] * eye[None, None, None, :, None, :, None]
        return w.reshape(2, S5_SG, S5_LAGS * LANE, S5_SGW)

    w_in = jnp.concatenate([w_layout(wre), w_layout(wim)], axis=-1).astype(BF16)

    def c_layout(c):
        c = c.reshape(2, S5_SG, 8, S5_CH, S5_STATE).transpose(0, 1, 2, 4, 3)
        c = c[:, :, :, :, None, :] * eye[None, None, :, None, :, None]
        return c.reshape(2, S5_SG, S5_SGW, LANE)

    w_out = jnp.concatenate([c_layout(c_re), -c_layout(c_im)], axis=2).astype(BF16)

    def p_layout(p):
        p = p[1:].reshape(S5_LAGS, 2, S5_SG, S5_SGW).transpose(1, 2, 0, 3)
        return jnp.stack([p[0], p[1, :, ::-1]], axis=0)

    def a8_layout(p):
        p = p[S5_LAGS].reshape(2, S5_SG, 1, S5_SGW)
        return jnp.broadcast_to(p, (2, S5_SG, SUBLANE, S5_SGW))

    return w_in, w_out, a8_layout(pre), a8_layout(pim), p_layout(pre), p_layout(pim)


def _s5_kernel(u_ref, w_ref, c_ref, a8r_ref, a8i_ref, pwr_ref, pwi_ref, h0r_ref, h0i_ref,
               y_ref, hfr_ref, hfi_ref, upad, wbr, wbi, *, seq, tc):
    d = pl.program_id(0)
    zeros = jnp.zeros((SUBLANE, LANE), F32)
    upad[0:SUBLANE, :] = zeros
    upad[SUBLANE:seq + SUBLANE, :] = u_ref[...]
    upad[seq + SUBLANE:seq + 2 * SUBLANE, :] = zeros
    nch = seq // tc
    nt = tc // SUBLANE
    a8r = a8r_ref[0, 0]
    a8i = a8i_ref[0, 0]

    def run(fwd):
        h0r = h0r_ref[0, 0, 0]
        h0i = h0i_ref[0, 0, 0]
        pwr = pwr_ref[0, 0]
        pwi = pwi_ref[0, 0]
        hr = jnp.zeros((SUBLANE, S5_SGW), F32)
        hi = jnp.zeros((SUBLANE, S5_SGW), F32)
        for ci in range(nch):
            c = ci if fwd else nch - 1 - ci
            if fwd:
                win = upad[c * tc:c * tc + tc + SUBLANE, :]
                lags = [pltpu.roll(win, tau, 0)[SUBLANE:SUBLANE + tc] if tau else win[SUBLANE:SUBLANE + tc]
                        for tau in range(S5_LAGS)]
            else:
                win = upad[c * tc + SUBLANE:c * tc + tc + 2 * SUBLANE, :]
                lags = [pltpu.roll(win, tc + SUBLANE - tau, 0)[0:tc] if tau else win[0:tc]
                        for tau in range(S5_LAGS)]
            lhs = jnp.concatenate([x.astype(BF16) for x in lags], axis=1)
            w = jnp.dot(lhs, w_ref[0, 0], preferred_element_type=F32)
            wbr[...] = w[:, :S5_SGW]
            wbi[...] = w[:, S5_SGW:]
            if ci == 0:
                r0 = 0 if fwd else tc - SUBLANE
                wbr[r0:r0 + SUBLANE, :] = wbr[r0:r0 + SUBLANE, :] + (pwr * h0r - pwi * h0i)
                wbi[r0:r0 + SUBLANE, :] = wbi[r0:r0 + SUBLANE, :] + (pwr * h0i + pwi * h0r)

            def step(i, carry):
                cr, ci_ = carry
                t = i if fwd else nt - 1 - i
                rows = pl.ds(pl.multiple_of(t * SUBLANE, SUBLANE), SUBLANE)
                nr = a8r * cr - a8i * ci_ + wbr[rows, :]
                ni = a8r * ci_ + a8i * cr + wbi[rows, :]
                wbr[rows, :] = nr
                wbi[rows, :] = ni
                return nr, ni

            hr, hi = lax.fori_loop(0, nt, step, (hr, hi))
            hcat = jnp.concatenate([wbr[...].astype(BF16), wbi[...].astype(BF16)], axis=1)
            y_ref[0, c * tc:(c + 1) * tc, :] = jnp.dot(hcat, c_ref[0, 0], preferred_element_type=F32)
        last = SUBLANE - 1 if fwd else 0
        hfr_ref[0, 0, 0] = hr[last:last + 1]
        hfi_ref[0, 0, 0] = hi[last:last + 1]

    @pl.when(d == 0)
    def _():
        run(True)

    @pl.when(d == 1)
    def _():
        run(False)


def _s5_scan(z, prep, h0r, h0i, *, ctx):
    w_in, w_out, a8r, a8i, pwr, pwi = prep
    seq = SEQ if ctx else DEC_SEQ
    nb = BATCH if ctx else DEC_BATCH
    row0 = 0 if ctx else T_CTX // DEC_SEQ
    tc = min(seq, 512)
    par = lambda d, s, b: (d, s, 0, 0)
    st = lambda d, s, b: (b, d, s, 0, 0)
    return pl.pallas_call(
        functools.partial(_s5_kernel, seq=seq, tc=tc),
        grid=(2, S5_SG, nb),
        in_specs=[pl.BlockSpec((seq, LANE), lambda d, s, b: (row0 + b, Z_SX // LANE + s)),
                  pl.BlockSpec((1, 1, S5_LAGS * LANE, 2 * S5_SGW), par),
                  pl.BlockSpec((1, 1, 2 * S5_SGW, LANE), par),
                  pl.BlockSpec((1, 1, SUBLANE, S5_SGW), par),
                  pl.BlockSpec((1, 1, SUBLANE, S5_SGW), par),
                  pl.BlockSpec((1, 1, SUBLANE, S5_SGW), par),
                  pl.BlockSpec((1, 1, SUBLANE, S5_SGW), par),
                  pl.BlockSpec((1, 1, 1, 1, S5_SGW), st),
                  pl.BlockSpec((1, 1, 1, 1, S5_SGW), st)],
        out_specs=[pl.BlockSpec((1, seq, LANE), lambda d, s, b: (d, b, s)),
                   pl.BlockSpec((1, 1, 1, 1, S5_SGW), st),
                   pl.BlockSpec((1, 1, 1, 1, S5_SGW), st)],
        out_shape=[jax.ShapeDtypeStruct((2, nb * seq, MIX_W), F32),
                   jax.ShapeDtypeStruct((nb, 2, S5_SG, 1, S5_SGW), F32),
                   jax.ShapeDtypeStruct((nb, 2, S5_SG, 1, S5_SGW), F32)],
        scratch_shapes=[pltpu.VMEM((seq + 2 * SUBLANE, LANE), F32),
                        pltpu.VMEM((tc, S5_SGW), F32),
                        pltpu.VMEM((tc, S5_SGW), F32)],
        compiler_params=_cparams(("arbitrary", "arbitrary", "arbitrary")),
        name="s5_ctx" if ctx else "s5_lat",
    )(z, w_in, w_out, a8r, a8i, pwr, pwi, h0r, h0i)


S5_POST_TM = 512


def _s5_post_kernel(y_ref, sx_ref, d_ref, w_ref, b_ref, o_ref):
    y = y_ref[0] + y_ref[1] + d_ref[...] * sx_ref[...]
    y = jax.nn.gelu(y)
    gate = jnp.dot(y.astype(BF16), w_ref[...], preferred_element_type=F32) + b_ref[...]
    o_ref[...] = (y * jax.nn.sigmoid(gate)).astype(BF16)


def _s5_post(y2, z, d_skip, w_glu, b_glu):
    tm = S5_POST_TM
    return pl.pallas_call(
        _s5_post_kernel,
        grid=(T_ALL // tm,),
        in_specs=[pl.BlockSpec((2, tm, MIX_W), lambda i: (0, i, 0)),
                  pl.BlockSpec((tm, MIX_W), lambda i: (i, Z_SX // MIX_W)),
                  pl.BlockSpec((1, MIX_W), lambda i: (0, 0)),
                  pl.BlockSpec((MIX_W, MIX_W), lambda i: (0, 0)),
                  pl.BlockSpec((1, MIX_W), lambda i: (0, 0))],
        out_specs=pl.BlockSpec((tm, MIX_W), lambda i: (i, 0)),
        out_shape=jax.ShapeDtypeStruct((T_ALL, MIX_W), BF16),
        compiler_params=_cparams(("arbitrary",)),
        name="s5_post",
    )(y2, z, d_skip.reshape(1, MIX_W), w_glu.astype(BF16), b_glu.reshape(1, MIX_W))


def _mlstm_kernel(q_ref, k_ref, v_ref, mo_ref, gc_ref, gr_ref, bc_ref, br_ref, c0_ref, n0_ref, m0_ref,
                  ng_ref, h_ref, c_ref, n_ref, m_ref, hs_scr, *, seq, chunk0):
    nc = seq // CHUNK
    ii = lax.broadcasted_iota(jnp.int32, (CHUNK, CHUNK), 0)
    jj = lax.broadcasted_iota(jnp.int32, (CHUNK, CHUNK), 1)
    neg_inf = jnp.float32(-jnp.inf)
    for d in range(2):
        fwd = d == 0
        mask = (jj <= ii) if fwd else (jj >= ii)
        mask_t = (ii <= jj) if fwd else (ii >= jj)
        last = CHUNK - 1 if fwd else 0

        def chunk(ci, carry, fwd=fwd, mask=mask, mask_t=mask_t, last=last, d=d):
            c_st, n_st, m_st = carry
            cidx = ci if fwd else nc - 1 - ci
            rows = pl.ds(pl.multiple_of(cidx * CHUNK, CHUNK), CHUNK)
            q = q_ref[rows, :]
            k = k_ref[rows, :] * (MLSTM_DIM ** -0.5)
            v = v_ref[rows, :]
            gcol = gc_ref[0, rows, :] + bc_ref[0]
            grow = gr_ref[0, chunk0 + cidx] + br_ref[0]
            li_c = gcol[:, 2 * d:2 * d + 1]
            lf_c = jax.nn.log_sigmoid(gcol[:, 2 * d + 1:2 * d + 2])
            li_r = grow[2 * d:2 * d + 1, :]
            lf_r = jax.nn.log_sigmoid(grow[2 * d + 1:2 * d + 2, :])
            b_c = jnp.sum(jnp.where(mask, lf_r, 0.0), axis=1, keepdims=True)
            b_r = jnp.sum(jnp.where(mask_t, lf_c, 0.0), axis=0, keepdims=True)
            dmat = jnp.where(mask, b_c - b_r + li_r, neg_inf)
            inter = b_c + m_st
            m_j = jnp.maximum(inter, jnp.max(dmat, axis=1, keepdims=True))
            qb = q.astype(BF16)
            vb = v.astype(BF16)
            s = lax.dot_general(qb, k.astype(BF16), (((1,), (1,)), ((), ())),
                                preferred_element_type=F32) * jnp.exp(dmat - m_j)
            w_inter = jnp.exp(inter - m_j)
            num = (jnp.dot(s.astype(BF16), vb, preferred_element_type=F32)
                   + w_inter * jnp.dot(qb, c_st.astype(BF16), preferred_element_type=F32))
            den = jnp.sum(s, axis=1, keepdims=True) + w_inter * jnp.sum(q * n_st, axis=1, keepdims=True)
            hs = num / jnp.maximum(jnp.abs(den), jnp.exp(-m_j))
            if fwd:
                hs_scr[rows, :] = hs
            else:
                hs_scr[rows, :] = hs_scr[rows, :] + hs
            m_end = m_j[last:last + 1, :]
            b_end = b_c[last:last + 1, :]
            w_c = jnp.exp(b_end - b_c + li_c - m_end)
            decay = jnp.exp(b_end + m_st - m_end)
            kw = k * w_c
            c_new = decay * c_st + lax.dot_general(kw.astype(BF16), vb, (((0,), (0,)), ((), ())),
                                                   preferred_element_type=F32)
            n_new = decay * n_st + jnp.sum(kw, axis=0, keepdims=True)
            return c_new, n_new, m_end

        init = (c0_ref[0, d, 0], n0_ref[0, d, 0], m0_ref[0, d, 0])
        c_f, n_f, m_f = lax.fori_loop(0, nc, chunk, init)
        c_ref[0, d, 0] = c_f
        n_ref[0, d, 0] = n_f
        m_ref[0, d, 0] = m_f
    h = _rms(hs_scr[...], ng_ref[0])
    h_ref[...] = (h * jax.nn.sigmoid(mo_ref[...])).astype(BF16)


def _mlstm(z, gcol, grow, bias_c, bias_r, c0, n0, m0, norm_g, *, ctx):
    seq = SEQ if ctx else DEC_SEQ
    nb = BATCH if ctx else DEC_BATCH
    row0 = 0 if ctx else T_CTX // DEC_SEQ
    hd = MLSTM_DIM
    nh = MLSTM_HEADS
    zcol = lambda off: (lambda b, h: (row0 + b, off // hd + h))
    st5 = lambda b, h: (b, 0, h, 0, 0)
    return pl.pallas_call(
        functools.partial(_mlstm_kernel, seq=seq, chunk0=0),
        grid=(nb, nh),
        in_specs=[pl.BlockSpec((seq, hd), zcol(Z_MQ)),
                  pl.BlockSpec((seq, hd), zcol(Z_MK)),
                  pl.BlockSpec((seq, hd), zcol(Z_MV)),
                  pl.BlockSpec((seq, hd), zcol(Z_MO)),
                  pl.BlockSpec((1, seq, 4), lambda b, h: (h, row0 + b, 0)),
                  pl.BlockSpec((1, seq // CHUNK, 4, CHUNK), lambda b, h: (h, row0 + b, 0, 0)),
                  pl.BlockSpec((1, 1, 4), lambda b, h: (h, 0, 0)),
                  pl.BlockSpec((1, 4, 1), lambda b, h: (h, 0, 0)),
                  pl.BlockSpec((1, 2, 1, hd, hd), st5),
                  pl.BlockSpec((1, 2, 1, 1, hd), st5),
                  pl.BlockSpec((1, 2, 1, 1, 1), st5),
                  pl.BlockSpec((1, 1, hd), lambda b, h: (h, 0, 0))],
        out_specs=[pl.BlockSpec((seq, hd), lambda b, h: (b, h)),
                   pl.BlockSpec((1, 2, 1, hd, hd), st5),
                   pl.BlockSpec((1, 2, 1, 1, hd), st5),
                   pl.BlockSpec((1, 2, 1, 1, 1), st5)],
        out_shape=[jax.ShapeDtypeStruct((nb * seq, MIX_W), BF16),
                   jax.ShapeDtypeStruct((nb, 2, nh, hd, hd), F32),
                   jax.ShapeDtypeStruct((nb, 2, nh, 1, hd), F32),
                   jax.ShapeDtypeStruct((nb, 2, nh, 1, 1), F32)],
        scratch_shapes=[pltpu.VMEM((seq, hd), F32)],
        compiler_params=_cparams(("arbitrary", "arbitrary")),
        name="mlstm_ctx" if ctx else "mlstm_lat",
    )(z, z, z, z, gcol, grow, bias_c, bias_r, c0, n0, m0, norm_g.reshape(nh, 1, hd))


GMLP_TM = 512


def _gmlp_kernel(gu_ref, gv_ref, ng_ref, ws_ref, bs_ref, o_ref):
    vn = _rms(gv_ref[...], ng_ref[...]).astype(BF16)
    gw = MIX_W // GMLP_GROUPS
    for c in range(GMLP_TM // CHUNK):
        r = slice(c * CHUNK, (c + 1) * CHUNK)
        for g in range(GMLP_GROUPS):
            cs = slice(g * gw, (g + 1) * gw)
            mixed = jnp.dot(ws_ref[g], vn[r, cs], preferred_element_type=F32) + bs_ref[:, g:g + 1]
            o_ref[r, cs] = (gu_ref[r, cs] * mixed).astype(BF16)


def _gmlp(z, norm_g, w_s, b_s):
    tm = GMLP_TM
    return pl.pallas_call(
        _gmlp_kernel,
        grid=(T_ALL // tm,),
        in_specs=[pl.BlockSpec((tm, MIX_W), lambda i: (i, Z_GU // MIX_W)),
                  pl.BlockSpec((tm, MIX_W), lambda i: (i, Z_GV // MIX_W)),
                  pl.BlockSpec((1, MIX_W), lambda i: (0, 0)),
                  pl.BlockSpec((GMLP_GROUPS, CHUNK, CHUNK), lambda i: (0, 0, 0)),
                  pl.BlockSpec((CHUNK, GMLP_GROUPS), lambda i: (0, 0))],
        out_specs=pl.BlockSpec((tm, MIX_W), lambda i: (i, 0)),
        out_shape=jax.ShapeDtypeStruct((T_ALL, MIX_W), BF16),
        compiler_params=_cparams(("arbitrary",)),
        name="gmlp",
    )(z, z, norm_g.reshape(1, MIX_W), w_s.astype(BF16), b_s.T)


OUT_TM = 256


def _out_kernel(*refs, router):
    if router:
        (a_ref, b_ref, c_ref, d_ref, x_ref, mod_ref, g_ref, w_ref, rh_ref, rl_ref, rb_ref,
         x1_ref, h2_ref, rt_ref) = refs
    else:
        a_ref, b_ref, c_ref, d_ref, x_ref, mod_ref, g_ref, w_ref, x1_ref, h2_ref = refs
    mix = jnp.dot(a_ref[...], w_ref[0:MIX_W, :], preferred_element_type=F32)
    mix += jnp.dot(b_ref[...], w_ref[MIX_W:2 * MIX_W, :], preferred_element_type=F32)
    mix += jnp.dot(c_ref[...], w_ref[2 * MIX_W:3 * MIX_W, :], preferred_element_type=F32)
    mix += jnp.dot(d_ref[...], w_ref[3 * MIX_W:4 * MIX_W, :], preferred_element_type=F32)
    x1 = x_ref[...] + mod_ref[0, 2:3, :] * mix
    x1_ref[...] = x1
    h2 = _rms(x1, g_ref[...]) * (1.0 + mod_ref[0, 4:5, :]) + mod_ref[0, 3:4, :]
    hi = h2.astype(BF16)
    h2_ref[...] = hi
    if router:
        lo = (h2 - hi.astype(F32)).astype(BF16)
        logits = (jnp.dot(hi, rh_ref[...], preferred_element_type=F32)
                  + jnp.dot(lo, rh_ref[...], preferred_element_type=F32)
                  + jnp.dot(hi, rl_ref[...], preferred_element_type=F32)) + rb_ref[...]
        lane = lax.broadcasted_iota(jnp.int32, logits.shape, 1)
        neg_inf = jnp.float32(-jnp.inf)
        lg = jnp.where(lane < N_EXPERTS, logits, neg_inf)
        m1 = jnp.max(lg, axis=-1, keepdims=True)
        i1 = jnp.min(jnp.where(lg == m1, lane, LANE), axis=-1, keepdims=True)
        lg2 = jnp.where(lane == i1, neg_inf, lg)
        m2 = jnp.max(lg2, axis=-1, keepdims=True)
        i2 = jnp.min(jnp.where(lg2 == m2, lane, LANE), axis=-1, keepdims=True)
        e = jnp.exp(m2 - m1)
        w1 = 1.0 / (1.0 + e)
        w2 = e / (1.0 + e)
        rt = jnp.where(lane == 0, i1.astype(F32), 0.0)
        rt = jnp.where(lane == 1, i2.astype(F32), rt)
        rt = jnp.where(lane == 2, w1, rt)
        rt = jnp.where(lane == 3, w2, rt)
        rt_ref[...] = rt


def _out_proj(mixers, x, mod_l, g2, w_out, router=None):
    tm = OUT_TM
    row = lambda i: (i, 0)
    fixed = lambda i: (0, 0)
    in_specs = [pl.BlockSpec((tm, MIX_W), row)] * 4 + [
        pl.BlockSpec((tm, D_MODEL), row),
        pl.BlockSpec((1, MOD_CHUNKS, D_MODEL), lambda i: (_mod_group(i, tm), 0, 0)),
        pl.BlockSpec((1, D_MODEL), fixed),
        pl.BlockSpec((D_MODEL, D_MODEL), fixed, pipeline_mode=pl.Buffered(1))]
    args = list(mixers) + [x, mod_l, g2.reshape(1, D_MODEL), w_out]
    out_specs = [pl.BlockSpec((tm, D_MODEL), row), pl.BlockSpec((tm, D_MODEL), row)]
    out_shape = [jax.ShapeDtypeStruct((T_ALL, D_MODEL), F32), jax.ShapeDtypeStruct((T_ALL, D_MODEL), BF16)]
    if router is not None:
        rw, rb = router
        rw = jnp.pad(rw, ((0, 0), (0, LANE - N_EXPERTS)))
        rh = rw.astype(BF16)
        rl = (rw - rh.astype(F32)).astype(BF16)
        in_specs += [pl.BlockSpec((D_MODEL, LANE), fixed)] * 2 + [pl.BlockSpec((1, LANE), fixed)]
        args += [rh, rl, jnp.pad(rb, (0, LANE - N_EXPERTS)).reshape(1, LANE)]
        out_specs.append(pl.BlockSpec((tm, LANE), row))
        out_shape.append(jax.ShapeDtypeStruct((T_ALL, LANE), F32))
    return pl.pallas_call(
        functools.partial(_out_kernel, router=router is not None),
        grid=(T_ALL // tm,),
        in_specs=in_specs, out_specs=out_specs, out_shape=out_shape,
        compiler_params=_cparams(("arbitrary",)),
        name="out_proj_router" if router is not None else "out_proj",
    )(*args)


def _ffn_kernel(te_ref, ns_ref, x_ref, wg_ref, wu_ref, wd_ref, o_ref):
    i = pl.program_id(0)
    f = pl.program_id(1)

    @pl.when(f == 0)
    def _():
        o_ref[...] = jnp.zeros_like(o_ref)

    def sub(s, carry):
        rows = pl.ds(pl.multiple_of(s * FFN_SUB, FFN_SUB), FFN_SUB)
        xs = x_ref[rows, :]
        g = jnp.dot(xs, wg_ref[0], preferred_element_type=F32)
        u = jnp.dot(xs, wu_ref[0], preferred_element_type=F32)
        a = (g * jax.nn.sigmoid(g) * u).astype(BF16)
        o_ref[rows, :] += jnp.dot(a, wd_ref[0], preferred_element_type=F32)
        return carry

    lax.fori_loop(0, ns_ref[i], sub, 0)


def _ffn(x, tile_e, tile_ns, w_gate, w_up, w_down):
    ntiles = x.shape[0] // FFN_TM
    nf = D_FF // FFN_TF

    def fcol(i, f, te, ns):
        return jnp.where(ns[i] > 0, f, nf - 1)

    return pl.pallas_call(
        _ffn_kernel,
        grid_spec=pltpu.PrefetchScalarGridSpec(
            num_scalar_prefetch=2,
            grid=(ntiles, nf),
            in_specs=[pl.BlockSpec((FFN_TM, D_MODEL), lambda i, f, te, ns: (i, 0)),
                      pl.BlockSpec((1, D_MODEL, FFN_TF), lambda i, f, te, ns: (te[i], 0, fcol(i, f, te, ns))),
                      pl.BlockSpec((1, D_MODEL, FFN_TF), lambda i, f, te, ns: (te[i], 0, fcol(i, f, te, ns))),
                      pl.BlockSpec((1, FFN_TF, D_MODEL), lambda i, f, te, ns: (te[i], fcol(i, f, te, ns), 0))],
            out_specs=pl.BlockSpec((FFN_TM, D_MODEL), lambda i, f, te, ns: (i, 0))),
        out_shape=jax.ShapeDtypeStruct((ntiles * FFN_TM, D_MODEL), F32),
        compiler_params=_cparams(("arbitrary", "arbitrary")),
        name="ffn",
    )(tile_e, tile_ns, x, w_gate, w_up, w_down)


def _gather_kernel(idx_ref, src_ref, dst_ref, sems, *, n):
    def row_copy(i, src_row):
        return pltpu.make_async_copy(src_ref.at[src_row], dst_ref.at[i], sems.at[i % GATHER_WINDOW])

    def body(i, carry):
        @pl.when(i >= GATHER_WINDOW)
        def _():
            row_copy(i - GATHER_WINDOW, 0).wait()

        row_copy(i, idx_ref[i]).start()
        return carry

    lax.fori_loop(0, n, body, 0)

    def drain(i, carry):
        row_copy(i, 0).wait()
        return carry

    lax.fori_loop(n - GATHER_WINDOW, n, drain, 0)


def _gather_rows(src, idx):
    n = idx.shape[0]
    src3 = src.reshape(src.shape[0], D_MODEL // LANE, LANE)
    out = pl.pallas_call(
        functools.partial(_gather_kernel, n=n),
        grid_spec=pltpu.PrefetchScalarGridSpec(
            num_scalar_prefetch=1,
            grid=(1,),
            in_specs=[pl.BlockSpec(memory_space=pl.ANY)],
            out_specs=pl.BlockSpec(memory_space=pl.ANY),
            scratch_shapes=[pltpu.SemaphoreType.DMA((GATHER_WINDOW,))]),
        out_shape=jax.ShapeDtypeStruct((n, D_MODEL // LANE, LANE), src.dtype),
        compiler_params=pltpu.CompilerParams(dimension_semantics=("arbitrary",), has_side_effects=True),
        name="gather_rows",
    )(idx, src3)
    return out.reshape(n, D_MODEL)


RES_TM = 256


def _res_kernel(*refs, moe, final):
    if moe:
        x_ref, ya_ref, yb_ref, rt_ref, mod_ref, g_ref, o_ref = refs
        rt = rt_ref[...]
        ffn = rt[:, 2:3] * ya_ref[...] + rt[:, 3:4] * yb_ref[...]
    else:
        x_ref, ya_ref, mod_ref, g_ref, o_ref = refs
        ffn = ya_ref[...]
    x2 = x_ref[...] + mod_ref[0, 5:6, :] * ffn
    o_ref[...] = _rms(x2, g_ref[...]) if final else x2


def _residual(x1, y, mod_l, final_g, *, route=None, final):
    tm = RES_TM
    nt = T_ALL // tm
    row = lambda i: (i, 0)
    in_specs = [pl.BlockSpec((tm, D_MODEL), row), pl.BlockSpec((tm, D_MODEL), row)]
    args = [x1, y]
    if route is not None:
        in_specs += [pl.BlockSpec((tm, D_MODEL), lambda i: (i + nt, 0)), pl.BlockSpec((tm, LANE), row)]
        args += [y, route]
    in_specs += [pl.BlockSpec((1, MOD_CHUNKS, D_MODEL), lambda i: (_mod_group(i, tm), 0, 0)),
                 pl.BlockSpec((1, D_MODEL), lambda i: (0, 0))]
    args += [mod_l, final_g.reshape(1, D_MODEL)]
    return pl.pallas_call(
        functools.partial(_res_kernel, moe=route is not None, final=final),
        grid=(nt,),
        in_specs=in_specs,
        out_specs=pl.BlockSpec((tm, D_MODEL), row),
        out_shape=jax.ShapeDtypeStruct((T_ALL, D_MODEL), F32),
        compiler_params=_cparams(("arbitrary",)),
        name="ffn_residual",
    )(*args)


def _route_plan(route):
    e_flat = jnp.concatenate([route[:, 0], route[:, 1]]).astype(jnp.int32)
    onehot = (e_flat[:, None] == jnp.arange(N_EXPERTS, dtype=jnp.int32)[None, :]).astype(jnp.int32)
    ranks = jnp.cumsum(onehot, axis=0) - onehot
    rank = jnp.sum(ranks * onehot, axis=1)
    counts = jnp.sum(onehot, axis=0)
    tiles = (counts + FFN_TM - 1) // FFN_TM
    tile_start = jnp.cumsum(tiles) - tiles
    slot = tile_start[e_flat] * FFN_TM + rank
    n_slots = MOE_TILES * FFN_TM
    tok = jnp.concatenate([jnp.arange(T_ALL, dtype=jnp.int32)] * 2)
    src = jnp.zeros((n_slots,), jnp.int32).at[slot].set(tok)
    tile_ids = jnp.arange(MOE_TILES, dtype=jnp.int32)
    ends = jnp.cumsum(tiles)
    tile_e = jnp.sum((tile_ids[:, None] >= ends[None, :]).astype(jnp.int32), axis=1)
    used = tile_e < N_EXPERTS
    last_e = jnp.max(jnp.where(counts > 0, jnp.arange(N_EXPERTS, dtype=jnp.int32), 0))
    tile_e = jnp.where(used, tile_e, last_e)
    rows_in_tile = jnp.clip(counts[tile_e] - (tile_ids - tile_start[tile_e]) * FFN_TM, 0, FFN_TM)
    tile_ns = jnp.where(used, (rows_in_tile + FFN_SUB - 1) // FFN_SUB, 0).astype(jnp.int32)
    return src, slot, tile_e.astype(jnp.int32), tile_ns


def _rope_tables():
    length = DEC_SEQ
    r = jnp.repeat(jnp.arange(length // GRID_W, dtype=F32), GRID_W)
    col = (jnp.arange(length) % GRID_W).astype(F32)
    half = HEAD_DIM // 2
    inv = ROPE_THETA ** (-jnp.arange(0, half, 2, dtype=F32) / half)
    ar, ac = r[:, None] * inv, col[:, None] * inv
    cos = jnp.concatenate([jnp.cos(ar), jnp.cos(ar), jnp.cos(ac), jnp.cos(ac)], axis=-1)
    sin = jnp.concatenate([-jnp.sin(ar), jnp.sin(ar), -jnp.sin(ac), jnp.sin(ac)], axis=-1)
    return cos, sin


def kernel(x_prompt, x_sample, c, c_ctx, cache_attn_k, cache_attn_v, state_s5_re, state_s5_im, state_mlstm_c, state_mlstm_n, state_mlstm_m, norm1_g, norm2_g, w_mod, b_mod, w_in, w_out, q_norm_g, k_norm_g, s5_a_re, s5_a_im, s5_log_dt, s5_b_re, s5_b_im, s5_c_re, s5_c_im, s5_d, s5_w_glu, s5_b_glu, mlstm_i_bias, mlstm_f_bias, mlstm_norm_g, gmlp_norm_g, gmlp_w_s, gmlp_b_s, ffn_w_gate, ffn_w_up, ffn_w_down, moe_router, moe_router_bias, moe_w_gate, moe_w_up, moe_w_down, final_norm_g):
    x = jnp.concatenate([x_prompt.reshape(T_CTX, D_MODEL), x_sample.reshape(T_LAT, D_MODEL)], axis=0)
    cond = jnp.concatenate([c_ctx[None, :], c], axis=0)
    mod = _modulation(cond, w_mod, b_mod)
    cos, sin = _rope_tables()
    nh = MLSTM_HEADS
    zeros_s5 = jnp.zeros((BATCH, 2, S5_SG, 1, S5_SGW), F32)
    zeros_c = jnp.zeros((BATCH, 2, nh, MLSTM_DIM, MLSTM_DIM), F32)
    zeros_n = jnp.zeros((BATCH, 2, nh, 1, MLSTM_DIM), F32)
    zeros_m = jnp.zeros((BATCH, 2, nh, 1, 1), F32)
    dense_e = jnp.zeros((T_ALL // FFN_TM,), jnp.int32)
    dense_ns = jnp.full((T_ALL // FFN_TM,), FFN_TM // FFN_SUB, jnp.int32)

    ctx_states = []
    for l in range(DEPTH):
        use_moe = l % 2 == 1
        j = l // 2
        wl = w_in[l]
        w_p = jnp.concatenate([wl[:, :3584], wl[:, 3600:], wl[:, 3584:3600],
                               jnp.zeros((D_MODEL, Z_COLS - 4624), F32)], axis=1).astype(BF16)
        z = _in_proj(x, mod[l], norm1_g[l], w_p)

        att_c, k_new, v_new = _attention(z, None, q_norm_g[l], k_norm_g[l], ctx=True)
        (att_l,) = _attention(z, None, q_norm_g[l], k_norm_g[l], ctx=False,
                              kpast=cache_attn_k[:, l], vpast=cache_attn_v[:, l], cos=cos, sin=sin)
        att = jnp.concatenate([att_c, att_l], axis=0)

        prep = _s5_prep(s5_a_re[l], s5_a_im[l], s5_log_dt[l], s5_b_re[l], s5_b_im[l], s5_c_re[l], s5_c_im[l])
        y_c, hf_re, hf_im = _s5_scan(z, prep, zeros_s5, zeros_s5, ctx=True)
        st = lambda s: s[:, l].reshape(DEC_BATCH, 2, S5_SG, 1, S5_SGW)
        y_l, _, _ = _s5_scan(z, prep, st(state_s5_re), st(state_s5_im), ctx=False)
        s5_out = _s5_post(jnp.concatenate([y_c, y_l], axis=1), z, s5_d[l], s5_w_glu[l], s5_b_glu[l])

        mg = z[:, Z_MG:Z_MG + 16].reshape(T_ALL, 2, 2, nh)
        gcol = mg.transpose(3, 0, 1, 2).reshape(nh, T_ALL, 4)
        grow = mg.reshape(T_ALL // CHUNK, CHUNK, 4, nh).transpose(3, 0, 2, 1)
        bias = jnp.stack([mlstm_i_bias[l], mlstm_f_bias[l]], axis=1)
        bias_c = bias.transpose(2, 0, 1).reshape(nh, 1, 4)
        bias_r = bias_c.reshape(nh, 4, 1)
        ml_c_out, c_new, n_new, m_new = _mlstm(z, gcol, grow, bias_c, bias_r, zeros_c, zeros_n, zeros_m,
                                               mlstm_norm_g[l], ctx=True)
        ml_l_out, _, _, _ = _mlstm(z, gcol, grow, bias_c, bias_r, state_mlstm_c[:, l],
                                   state_mlstm_n[:, l].reshape(DEC_BATCH, 2, nh, 1, MLSTM_DIM),
                                   state_mlstm_m[:, l].reshape(DEC_BATCH, 2, nh, 1, 1),
                                   mlstm_norm_g[l], ctx=False)
        ml_out = jnp.concatenate([ml_c_out, ml_l_out], axis=0)

        gm_out = _gmlp(z, gmlp_norm_g[l], gmlp_w_s[l], gmlp_b_s[l])

        ctx_states.append((k_new, v_new,
                           hf_re.reshape(BATCH, 2, S5_GROUPS, S5_STATE), hf_im.reshape(BATCH, 2, S5_GROUPS, S5_STATE),
                           c_new, n_new.reshape(BATCH, 2, nh, MLSTM_DIM), m_new.reshape(BATCH, 2, nh)))

        w_o = w_out[l].astype(BF16)
        final = l == DEPTH - 1
        if use_moe:
            x1, h2, route = _out_proj((att, s5_out, ml_out, gm_out), x, mod[l], norm2_g[l], w_o,
                                      router=(moe_router[j], moe_router_bias[j]))
            src, slot, tile_e, tile_ns = _route_plan(route)
            xs = _gather_rows(h2, src)
            ys = _ffn(xs, tile_e, tile_ns, moe_w_gate[j].astype(BF16), moe_w_up[j].astype(BF16),
                      moe_w_down[j].astype(BF16))
            y2 = _gather_rows(ys, slot)
            x = _residual(x1, y2, mod[l], final_norm_g, route=route, final=final)
        else:
            x1, h2 = _out_proj((att, s5_out, ml_out, gm_out), x, mod[l], norm2_g[l], w_o)
            y = _ffn(h2, dense_e, dense_ns, ffn_w_gate[j][None].astype(BF16), ffn_w_up[j][None].astype(BF16),
                     ffn_w_down[j][None].astype(BF16))
            x = _residual(x1, y, mod[l], final_norm_g, final=final)

    if DEPTH % 2 == 0 or True:
        pass
    y_prompt = x[:T_CTX].reshape(BATCH, SEQ, D_MODEL)
    y_sample = x[T_CTX:].reshape(DEC_BATCH, DEC_SEQ, D_MODEL)
    stack = lambda i: jnp.stack([s[i] for s in ctx_states], axis=1)
    return (y_prompt, y_sample, stack(0), stack(1), stack(2), stack(3), stack(4), stack(5), stack(6))
```

```python
import functools
import math

import numpy as np
import jax
import jax.numpy as jnp
from jax import lax
from jax.experimental import pallas as pl
from jax.experimental.pallas import tpu as pltpu

F32 = jnp.float32
BF16 = jnp.bfloat16

D_MODEL = 2048
BATCH = 16
SEQ = 256
DEPTH = 2
DEC_BATCH = 2
DEC_SEQ = 2048
PAST_LEN = 256
GRID_W = 64
MIX_W = 512
ATT_HEADS = 4
ATT_KV_HEADS = 2
HEAD_DIM = 128
ROPE_THETA = 10000.0
S5_CH = 16
S5_GROUPS = 32
S5_STATE = 64
MLSTM_HEADS = 4
MLSTM_DIM = 128
CHUNK = 128
GMLP_GROUPS = 4
D_FF = 7168
N_EXPERTS = 8
MOD_CHUNKS = 6
EPS = 1e-6

T_CTX = BATCH * SEQ
T_LAT = DEC_BATCH * DEC_SEQ
T_ALL = T_CTX + T_LAT
N_GROUPS_MOD = 1 + DEC_BATCH

Z_AQ, Z_AK, Z_AV, Z_SX = 0, 512, 768, 1024
Z_MQ, Z_MK, Z_MV, Z_MO = 1536, 2048, 2560, 3072
Z_GU, Z_GV, Z_MG = 3584, 4096, 4608
Z_COLS = 4736
LANE = 128
SUBLANE = 8

VMEM_LIMIT = 56 * 1024 * 1024

S5_SG = 4
S5_SGW = 8 * S5_STATE
S5_LAGS = 8

FFN_TM = 1024
FFN_SUB = 256
FFN_TF = 512
MOE_TILES = 2 * T_ALL // FFN_TM + N_EXPERTS


def _cparams(sem=None):
    return pltpu.CompilerParams(dimension_semantics=sem, vmem_limit_bytes=VMEM_LIMIT)


def _mod_group(i, tm):
    return jnp.maximum(i * tm // DEC_SEQ - (T_CTX // DEC_SEQ - 1), 0)


def _rms(x, g):
    return x * lax.rsqrt(jnp.mean(x * x, axis=-1, keepdims=True) + EPS) * g


MOD_TN = 512


def _mod_kernel(cb_ref, w_ref, b_ref, o_ref):
    w = w_ref[0]
    rows = []
    for r in range(N_GROUPS_MOD):
        c = cb_ref[r]
        s = c * jax.nn.sigmoid(c)
        parts = [jnp.sum(w[:, j * LANE:(j + 1) * LANE] * s, axis=0, keepdims=True)
                 for j in range(MOD_TN // LANE)]
        rows.append(jnp.concatenate(parts, axis=1))
    rows.append(jnp.zeros((SUBLANE - N_GROUPS_MOD, MOD_TN), F32))
    o_ref[0] = jnp.concatenate(rows, axis=0) + b_ref[0]


def _modulation(cond, w_mod, b_mod):
    cb = jnp.broadcast_to(cond[:, :, None], (N_GROUPS_MOD, D_MODEL, LANE))
    n = MOD_CHUNKS * D_MODEL
    out = pl.pallas_call(
        _mod_kernel,
        grid=(DEPTH, n // MOD_TN),
        in_specs=[pl.BlockSpec((N_GROUPS_MOD, D_MODEL, LANE), lambda l, j: (0, 0, 0)),
                  pl.BlockSpec((1, D_MODEL, MOD_TN), lambda l, j: (l, 0, j)),
                  pl.BlockSpec((1, 1, MOD_TN), lambda l, j: (l, 0, j))],
        out_specs=pl.BlockSpec((1, SUBLANE, MOD_TN), lambda l, j: (l, 0, j)),
        out_shape=jax.ShapeDtypeStruct((DEPTH, SUBLANE, n), F32),
        compiler_params=_cparams(("arbitrary", "arbitrary")),
        name="adaln_mod",
    )(cb, w_mod, b_mod.reshape(DEPTH, 1, n))
    return out[:, :N_GROUPS_MOD].reshape(DEPTH, N_GROUPS_MOD, MOD_CHUNKS, D_MODEL)


IN_TM = 256
IN_CHUNK = 512


def _in_kernel(x_ref, mod_ref, g_ref, w_ref, z_ref):
    x = x_ref[...]
    shift = mod_ref[0, 0:1, :]
    scale = mod_ref[0, 1:2, :]
    h = (_rms(x, g_ref[...]) * (1.0 + scale) + shift).astype(BF16)
    for c0 in range(0, Z_COLS, IN_CHUNK):
        cw = min(IN_CHUNK, Z_COLS - c0)
        z_ref[:, c0:c0 + cw] = jnp.dot(h, w_ref[:, c0:c0 + cw], preferred_element_type=F32)


def _in_proj(x, mod_l, g, w_p):
    return pl.pallas_call(
        _in_kernel,
        grid=(T_ALL // IN_TM,),
        in_specs=[pl.BlockSpec((IN_TM, D_MODEL), lambda i: (i, 0)),
                  pl.BlockSpec((1, MOD_CHUNKS, D_MODEL), lambda i: (_mod_group(i, IN_TM), 0, 0)),
                  pl.BlockSpec((1, D_MODEL), lambda i: (0, 0)),
                  pl.BlockSpec((D_MODEL, Z_COLS), lambda i: (0, 0), pipeline_mode=pl.Buffered(1))],
        out_specs=pl.BlockSpec((IN_TM, Z_COLS), lambda i: (i, 0)),
        out_shape=jax.ShapeDtypeStruct((T_ALL, Z_COLS), F32),
        compiler_params=_cparams(("arbitrary",)),
        name="in_proj",
    )(x, mod_l, g.reshape(1, D_MODEL), w_p)


def _rope(t, c, s):
    lane = lax.broadcasted_iota(jnp.int32, t.shape, 1)
    first = (lane % (HEAD_DIM // 2)) < (HEAD_DIM // 4)
    swapped = jnp.where(first, pltpu.roll(t, HEAD_DIM - HEAD_DIM // 4, 1), pltpu.roll(t, HEAD_DIM // 4, 1))
    return t * c + swapped * s


def _attn_kernel(*refs, seq, past, rope):
    if rope:
        (aq_ref, ak_ref, av_ref, kp_ref, vp_ref, cos_ref, sin_ref, qg_ref, kg_ref,
         att_ref, kb_scr, vb_scr) = refs
    else:
        aq_ref, ak_ref, av_ref, qg_ref, kg_ref, att_ref, knew_ref, vnew_ref, kb_scr, vb_scr = refs
    kn = _rms(ak_ref[...], kg_ref[...])
    v = av_ref[...]
    if rope:
        kn = _rope(kn, cos_ref[...], sin_ref[...])
        kb_scr[seq:seq + past, :] = kp_ref[0, 0].astype(BF16)
        vb_scr[seq:seq + past, :] = vp_ref[0, 0].astype(BF16)
    else:
        knew_ref[0, 0] = kn
        vnew_ref[0, 0] = v
    kb_scr[0:seq, :] = kn.astype(BF16)
    vb_scr[0:seq, :] = v.astype(BF16)
    grp = ATT_HEADS // ATT_KV_HEADS

    def q_block(qb, carry):
        rows = pl.ds(pl.multiple_of(qb * CHUNK, CHUNK), CHUNK)
        qs = []
        for g in range(grp):
            q = _rms(aq_ref[rows, g * HEAD_DIM:(g + 1) * HEAD_DIM], qg_ref[...])
            if rope:
                q = _rope(q, cos_ref[rows, :], sin_ref[rows, :])
            qs.append(q)
        q2 = jnp.concatenate(qs, axis=0).astype(BF16)
        s = lax.dot_general(q2, kb_scr[...], (((1,), (1,)), ((), ())),
                            preferred_element_type=F32) * (HEAD_DIM ** -0.5)
        m = jnp.max(s, axis=-1, keepdims=True)
        p = jnp.exp(s - m)
        den = jnp.sum(p, axis=-1, keepdims=True)
        o = jnp.dot(p.astype(BF16), vb_scr[...], preferred_element_type=F32) / den
        for g in range(grp):
            att_ref[rows, g * HEAD_DIM:(g + 1) * HEAD_DIM] = o[g * CHUNK:(g + 1) * CHUNK].astype(BF16)
        return carry

    lax.fori_loop(0, seq // CHUNK, q_block, 0)


def _attention(z, att_out_shape, qg, kg, *, ctx, kpast=None, vpast=None, cos=None, sin=None):
    seq = SEQ if ctx else DEC_SEQ
    nb = BATCH if ctx else DEC_BATCH
    row0 = 0 if ctx else T_CTX // DEC_SEQ
    past = 0 if ctx else PAST_LEN
    qw = HEAD_DIM * (ATT_HEADS // ATT_KV_HEADS)
    in_specs = [pl.BlockSpec((seq, qw), lambda b, h: (row0 + b, Z_AQ // qw + h)),
                pl.BlockSpec((seq, HEAD_DIM), lambda b, h: (row0 + b, Z_AK // HEAD_DIM + h)),
                pl.BlockSpec((seq, HEAD_DIM), lambda b, h: (row0 + b, Z_AV // HEAD_DIM + h))]
    args = [z, z, z]
    if not ctx:
        in_specs += [pl.BlockSpec((1, 1, past, HEAD_DIM), lambda b, h: (b, h, 0, 0)),
                     pl.BlockSpec((1, 1, past, HEAD_DIM), lambda b, h: (b, h, 0, 0)),
                     pl.BlockSpec((seq, HEAD_DIM), lambda b, h: (0, 0)),
                     pl.BlockSpec((seq, HEAD_DIM), lambda b, h: (0, 0))]
        args += [kpast, vpast, cos, sin]
    in_specs += [pl.BlockSpec((1, HEAD_DIM), lambda b, h: (0, 0)),
                 pl.BlockSpec((1, HEAD_DIM), lambda b, h: (0, 0))]
    args += [qg.reshape(1, HEAD_DIM), kg.reshape(1, HEAD_DIM)]
    out_specs = [pl.BlockSpec((seq, qw), lambda b, h: (b, h))]
    out_shape = [jax.ShapeDtypeStruct((nb * seq, MIX_W), BF16)]
    if ctx:
        out_specs += [pl.BlockSpec((1, 1, seq, HEAD_DIM), lambda b, h: (b, h, 0, 0))] * 2
        out_shape += [jax.ShapeDtypeStruct((nb, ATT_KV_HEADS, seq, HEAD_DIM), F32)] * 2
    return pl.pallas_call(
        functools.partial(_attn_kernel, seq=seq, past=past, rope=not ctx),
        grid=(nb, ATT_KV_HEADS),
        in_specs=in_specs, out_specs=out_specs, out_shape=out_shape,
        scratch_shapes=[pltpu.VMEM((seq + past, HEAD_DIM), BF16),
                        pltpu.VMEM((seq + past, HEAD_DIM), BF16)],
        compiler_params=_cparams(("arbitrary", "arbitrary")),
        name="attn_ctx" if ctx else "attn_lat",
    )(*args)


def _s5_prep_kernel(are_ref, aim_ref, ldt_ref, bre_ref, bim_ref, pre_ref, pim_ref, wre_ref, wim_ref):
    a_re = are_ref[...]
    a_im = aim_ref[...]
    dt = jnp.exp(ldt_ref[...])
    pows = []
    for tau in range(S5_LAGS + 1):
        mag = jnp.exp((tau * dt) * a_re)
        ang = (tau * dt) * a_im
        pr, pi = mag * jnp.cos(ang), mag * jnp.sin(ang)
        pre_ref[tau] = pr
        pim_ref[tau] = pi
        pows.append((pr, pi))
    nr, ni = pows[1][0] - 1.0, pows[1][1]
    den = a_re * a_re + a_im * a_im
    cr = (nr * a_re + ni * a_im) / den
    ci = (ni * a_re - nr * a_im) / den
    for d in range(2):
        b_r, b_i = bre_ref[d], bim_ref[d]
        bb_r = cr[d:d + 1] * b_r - ci[d:d + 1] * b_i
        bb_i = cr[d:d + 1] * b_i + ci[d:d + 1] * b_r
        for tau in range(S5_LAGS):
            pr, pi = pows[tau][0][d:d + 1], pows[tau][1][d:d + 1]
            wre_ref[d, tau] = pr * bb_r - pi * bb_i
            wim_ref[d, tau] = pr * bb_i + pi * bb_r


def _s5_prep(a_re, a_im, log_dt, b_re, b_im, c_re, c_im):
    gp = S5_GROUPS * S5_STATE
    ldt = jnp.broadcast_to(log_dt[:, :, None], (2, S5_GROUPS, S5_STATE)).reshape(2, gp)
    bt = lambda b: b.transpose(0, 3, 1, 2).reshape(2, S5_CH, gp)
    pre, pim, wre, wim = pl.pallas_call(
        _s5_prep_kernel,
        out_shape=[jax.ShapeDtypeStruct((S5_LAGS + 1, 2, gp), F32)] * 2
        + [jax.ShapeDtypeStruct((2, S5_LAGS, S5_CH, gp), F32)] * 2,
        name="s5_prep",
    )(a_re.reshape(2, gp), a_im.reshape(2, gp), ldt, bt(b_re), bt(b_im))
    eye = jnp.eye(8, dtype=F32)

    def w_layout(w):
        w = w.reshape(2, S5_LAGS, S5_CH, S5_SG, 8, S5_STATE).transpose(0, 3, 1, 4, 2, 5)
        w = w[:, :, :, :, :, None, :] * eye[None, None, None, :, None, :, None]
        return w.reshape(2, S5_SG, S5_LAGS * LANE, S5_SGW)

    w_in = jnp.concatenate([w_layout(wre), w_layout(wim)], axis=-1).astype(BF16)

    def c_layout(c):
        c = c.reshape(2, S5_SG, 8, S5_CH, S5_STATE).transpose(0, 1, 2, 4, 3)
        c = c[:, :, :, :, None, :] * eye[None, None, :, None, :, None]
        return c.reshape(2, S5_SG, S5_SGW, LANE)

    w_out = jnp.concatenate([c_layout(c_re), -c_layout(c_im)], axis=2).astype(BF16)

    def p_layout(p):
        p = p[1:].reshape(S5_LAGS, 2, S5_SG, S5_SGW).transpose(1, 2, 0, 3)
        return jnp.stack([p[0], p[1, :, ::-1]], axis=0)

    def a8_layout(p):
        p = p[S5_LAGS].reshape(2, S5_SG, 1, S5_SGW)
        return jnp.broadcast_to(p, (2, S5_SG, SUBLANE, S5_SGW))

    return w_in, w_out, a8_layout(pre), a8_layout(pim), p_layout(pre), p_layout(pim)


def _s5_kernel(u_ref, w_ref, c_ref, a8r_ref, a8i_ref, pwr_ref, pwi_ref, h0r_ref, h0i_ref,
               y_ref, hfr_ref, hfi_ref, upad, wbr, wbi, *, seq, tc):
    d = pl.program_id(0)
    zeros = jnp.zeros((SUBLANE, LANE), F32)
    upad[0:SUBLANE, :] = zeros
    upad[SUBLANE:seq + SUBLANE, :] = u_ref[...]
    upad[seq + SUBLANE:seq + 2 * SUBLANE, :] = zeros
    nch = seq // tc
    nt = tc // SUBLANE
    a8r = a8r_ref[0, 0]
    a8i = a8i_ref[0, 0]

    def run(fwd):
        h0r = h0r_ref[0, 0, 0]
        h0i = h0i_ref[0, 0, 0]
        pwr = pwr_ref[0, 0]
        pwi = pwi_ref[0, 0]
        hr = jnp.zeros((SUBLANE, S5_SGW), F32)
        hi = jnp.zeros((SUBLANE, S5_SGW), F32)
        for ci in range(nch):
            c = ci if fwd else nch - 1 - ci
            if fwd:
                win = upad[c * tc:c * tc + tc + SUBLANE, :]
                lags = [pltpu.roll(win, tau, 0)[SUBLANE:SUBLANE + tc] if tau else win[SUBLANE:SUBLANE + tc]
                        for tau in range(S5_LAGS)]
            else:
                win = upad[c * tc + SUBLANE:c * tc + tc + 2 * SUBLANE, :]
                lags = [pltpu.roll(win, tc + SUBLANE - tau, 0)[0:tc] if tau else win[0:tc]
                        for tau in range(S5_LAGS)]
            lhs = jnp.concatenate([x.astype(BF16) for x in lags], axis=1)
            w = jnp.dot(lhs, w_ref[0, 0], preferred_element_type=F32)
            wbr[...] = w[:, :S5_SGW]
            wbi[...] = w[:, S5_SGW:]
            if ci == 0:
                r0 = 0 if fwd else tc - SUBLANE
                wbr[r0:r0 + SUBLANE, :] = wbr[r0:r0 + SUBLANE, :] + (pwr * h0r - pwi * h0i)
                wbi[r0:r0 + SUBLANE, :] = wbi[r0:r0 + SUBLANE, :] + (pwr * h0i + pwi * h0r)

            def step(i, carry):
                cr, ci_ = carry
                t = i if fwd else nt - 1 - i
                rows = pl.ds(pl.multiple_of(t * SUBLANE, SUBLANE), SUBLANE)
                nr = a8r * cr - a8i * ci_ + wbr[rows, :]
                ni = a8r * ci_ + a8i * cr + wbi[rows, :]
                wbr[rows, :] = nr
                wbi[rows, :] = ni
                return nr, ni

            hr, hi = lax.fori_loop(0, nt, step, (hr, hi))
            hcat = jnp.concatenate([wbr[...].astype(BF16), wbi[...].astype(BF16)], axis=1)
            y_ref[0, c * tc:(c + 1) * tc, :] = jnp.dot(hcat, c_ref[0, 0], preferred_element_type=F32)
        last = SUBLANE - 1 if fwd else 0
        hfr_ref[0, 0, 0] = hr[last:last + 1]
        hfi_ref[0, 0, 0] = hi[last:last + 1]

    @pl.when(d == 0)
    def _():
        run(True)

    @pl.when(d == 1)
    def _():
        run(False)


def _s5_scan(z, prep, h0r, h0i, *, ctx):
    w_in, w_out, a8r, a8i, pwr, pwi = prep
    seq = SEQ if ctx else DEC_SEQ
    nb = BATCH if ctx else DEC_BATCH
    row0 = 0 if ctx else T_CTX // DEC_SEQ
    tc = min(seq, 512)
    par = lambda d, s, b: (d, s, 0, 0)
    st = lambda d, s, b: (b, d, s, 0, 0)
    return pl.pallas_call(
        functools.partial(_s5_kernel, seq=seq, tc=tc),
        grid=(2, S5_SG, nb),
        in_specs=[pl.BlockSpec((seq, LANE), lambda d, s, b: (row0 + b, Z_SX // LANE + s)),
                  pl.BlockSpec((1, 1, S5_LAGS * LANE, 2 * S5_SGW), par),
                  pl.BlockSpec((1, 1, 2 * S5_SGW, LANE), par),
                  pl.BlockSpec((1, 1, SUBLANE, S5_SGW), par),
                  pl.BlockSpec((1, 1, SUBLANE, S5_SGW), par),
                  pl.BlockSpec((1, 1, SUBLANE, S5_SGW), par),
                  pl.BlockSpec((1, 1, SUBLANE, S5_SGW), par),
                  pl.BlockSpec((1, 1, 1, 1, S5_SGW), st),
                  pl.BlockSpec((1, 1, 1, 1, S5_SGW), st)],
        out_specs=[pl.BlockSpec((1, seq, LANE), lambda d, s, b: (d, b, s)),
                   pl.BlockSpec((1, 1, 1, 1, S5_SGW), st),
                   pl.BlockSpec((1, 1, 1, 1, S5_SGW), st)],
        out_shape=[jax.ShapeDtypeStruct((2, nb * seq, MIX_W), F32),
                   jax.ShapeDtypeStruct((nb, 2, S5_SG, 1, S5_SGW), F32),
                   jax.ShapeDtypeStruct((nb, 2, S5_SG, 1, S5_SGW), F32)],
        scratch_shapes=[pltpu.VMEM((seq + 2 * SUBLANE, LANE), F32),
                        pltpu.VMEM((tc, S5_SGW), F32),
                        pltpu.VMEM((tc, S5_SGW), F32)],
        compiler_params=_cparams(("arbitrary", "arbitrary", "arbitrary")),
        name="s5_ctx" if ctx else "s5_lat",
    )(z, w_in, w_out, a8r, a8i, pwr, pwi, h0r, h0i)


S5_POST_TM = 512


def _s5_post_kernel(y_ref, sx_ref, d_ref, w_ref, b_ref, o_ref):
    y = y_ref[0] + y_ref[1] + d_ref[...] * sx_ref[...]
    y = jax.nn.gelu(y)
    gate = jnp.dot(y.astype(BF16), w_ref[...], preferred_element_type=F32) + b_ref[...]
    o_ref[...] = (y * jax.nn.sigmoid(gate)).astype(BF16)


def _s5_post(y2, z, d_skip, w_glu, b_glu):
    tm = S5_POST_TM
    return pl.pallas_call(
        _s5_post_kernel,
        grid=(T_ALL // tm,),
        in_specs=[pl.BlockSpec((2, tm, MIX_W), lambda i: (0, i, 0)),
                  pl.BlockSpec((tm, MIX_W), lambda i: (i, Z_SX // MIX_W)),
                  pl.BlockSpec((1, MIX_W), lambda i: (0, 0)),
                  pl.BlockSpec((MIX_W, MIX_W), lambda i: (0, 0)),
                  pl.BlockSpec((1, MIX_W), lambda i: (0, 0))],
        out_specs=pl.BlockSpec((tm, MIX_W), lambda i: (i, 0)),
        out_shape=jax.ShapeDtypeStruct((T_ALL, MIX_W), BF16),
        compiler_params=_cparams(("arbitrary",)),
        name="s5_post",
    )(y2, z, d_skip.reshape(1, MIX_W), w_glu.astype(BF16), b_glu.reshape(1, MIX_W))


def _mlstm_kernel(q_ref, k_ref, v_ref, mo_ref, gc_ref, gr_ref, bc_ref, br_ref, c0_ref, n0_ref, m0_ref,
                  ng_ref, h_ref, c_ref, n_ref, m_ref, hs_scr, *, seq, chunk0):
    nc = seq // CHUNK
    ii = lax.broadcasted_iota(jnp.int32, (CHUNK, CHUNK), 0)
    jj = lax.broadcasted_iota(jnp.int32, (CHUNK, CHUNK), 1)
    neg_inf = jnp.float32(-jnp.inf)
    for d in range(2):
        fwd = d == 0
        mask = (jj <= ii) if fwd else (jj >= ii)
        mask_t = (ii <= jj) if fwd else (ii >= jj)
        last = CHUNK - 1 if fwd else 0

        def chunk(ci, carry, fwd=fwd, mask=mask, mask_t=mask_t, last=last, d=d):
            c_st, n_st, m_st = carry
            cidx = ci if fwd else nc - 1 - ci
            rows = pl.ds(pl.multiple_of(cidx * CHUNK, CHUNK), CHUNK)
            q = q_ref[rows, :]
            k = k_ref[rows, :] * (MLSTM_DIM ** -0.5)
            v = v_ref[rows, :]
            gcol = gc_ref[0, rows, :] + bc_ref[0]
            grow = gr_ref[0, chunk0 + cidx] + br_ref[0]
            li_c = gcol[:, 2 * d:2 * d + 1]
            lf_c = jax.nn.log_sigmoid(gcol[:, 2 * d + 1:2 * d + 2])
            li_r = grow[2 * d:2 * d + 1, :]
            lf_r = jax.nn.log_sigmoid(grow[2 * d + 1:2 * d + 2, :])
            b_c = jnp.sum(jnp.where(mask, lf_r, 0.0), axis=1, keepdims=True)
            b_r = jnp.sum(jnp.where(mask_t, lf_c, 0.0), axis=0, keepdims=True)
            dmat = jnp.where(mask, b_c - b_r + li_r, neg_inf)
            inter = b_c + m_st
            m_j = jnp.maximum(inter, jnp.max(dmat, axis=1, keepdims=True))
            qb = q.astype(BF16)
            vb = v.astype(BF16)
            s = lax.dot_general(qb, k.astype(BF16), (((1,), (1,)), ((), ())),
                                preferred_element_type=F32) * jnp.exp(dmat - m_j)
            w_inter = jnp.exp(inter - m_j)
            num = (jnp.dot(s.astype(BF16), vb, preferred_element_type=F32)
                   + w_inter * jnp.dot(qb, c_st.astype(BF16), preferred_element_type=F32))
            den = jnp.sum(s, axis=1, keepdims=True) + w_inter * jnp.sum(q * n_st, axis=1, keepdims=True)
            hs = num / jnp.maximum(jnp.abs(den), jnp.exp(-m_j))
            if fwd:
                hs_scr[rows, :] = hs
            else:
                hs_scr[rows, :] = hs_scr[rows, :] + hs
            m_end = m_j[last:last + 1, :]
            b_end = b_c[last:last + 1, :]
            w_c = jnp.exp(b_end - b_c + li_c - m_end)
            decay = jnp.exp(b_end + m_st - m_end)
            kw = k * w_c
            c_new = decay * c_st + lax.dot_general(kw.astype(BF16), vb, (((0,), (0,)), ((), ())),
                                                   preferred_element_type=F32)
            n_new = decay * n_st + jnp.sum(kw, axis=0, keepdims=True)
            return c_new, n_new, m_end

        init = (c0_ref[0, d, 0], n0_ref[0, d, 0], m0_ref[0, d, 0])
        c_f, n_f, m_f = lax.fori_loop(0, nc, chunk, init)
        c_ref[0, d, 0] = c_f
        n_ref[0, d, 0] = n_f
        m_ref[0, d, 0] = m_f
    h = _rms(hs_scr[...], ng_ref[0])
    h_ref[...] = (h * jax.nn.sigmoid(mo_ref[...])).astype(BF16)


def _mlstm(z, gcol, grow, bias_c, bias_r, c0, n0, m0, norm_g, *, ctx):
    seq = SEQ if ctx else DEC_SEQ
    nb = BATCH if ctx else DEC_BATCH
    row0 = 0 if ctx else T_CTX // DEC_SEQ
    hd = MLSTM_DIM
    nh = MLSTM_HEADS
    zcol = lambda off: (lambda b, h: (row0 + b, off // hd + h))
    st5 = lambda b, h: (b, 0, h, 0, 0)
    return pl.pallas_call(
        functools.partial(_mlstm_kernel, seq=seq, chunk0=0),
        grid=(nb, nh),
        in_specs=[pl.BlockSpec((seq, hd), zcol(Z_MQ)),
                  pl.BlockSpec((seq, hd), zcol(Z_MK)),
                  pl.BlockSpec((seq, hd), zcol(Z_MV)),
                  pl.BlockSpec((seq, hd), zcol(Z_MO)),
                  pl.BlockSpec((1, seq, 4), lambda b, h: (h, row0 + b, 0)),
                  pl.BlockSpec((1, seq // CHUNK, 4, CHUNK), lambda b, h: (h, row0 + b, 0, 0)),
                  pl.BlockSpec((1, 1, 4), lambda b, h: (h, 0, 0)),
                  pl.BlockSpec((1, 4, 1), lambda b, h: (h, 0, 0)),
                  pl.BlockSpec((1, 2, 1, hd, hd), st5),
                  pl.BlockSpec((1, 2, 1, 1, hd), st5),
                  pl.BlockSpec((1, 2, 1, 1, 1), st5),
                  pl.BlockSpec((1, 1, hd), lambda b, h: (h, 0, 0))],
        out_specs=[pl.BlockSpec((seq, hd), lambda b, h: (b, h)),
                   pl.BlockSpec((1, 2, 1, hd, hd), st5),
                   pl.BlockSpec((1, 2, 1, 1, hd), st5),
                   pl.BlockSpec((1, 2, 1, 1, 1), st5)],
        out_shape=[jax.ShapeDtypeStruct((nb * seq, MIX_W), BF16),
                   jax.ShapeDtypeStruct((nb, 2, nh, hd, hd), F32),
                   jax.ShapeDtypeStruct((nb, 2, nh, 1, hd), F32),
                   jax.ShapeDtypeStruct((nb, 2, nh, 1, 1), F32)],
        scratch_shapes=[pltpu.VMEM((seq, hd), F32)],
        compiler_params=_cparams(("arbitrary", "arbitrary")),
        name="mlstm_ctx" if ctx else "mlstm_lat",
    )(z, z, z, z, gcol, grow, bias_c, bias_r, c0, n0, m0, norm_g.reshape(nh, 1, hd))


GMLP_TM = 512


def _gmlp_kernel(gu_ref, gv_ref, ng_ref, ws_ref, bs_ref, o_ref):
    vn = _rms(gv_ref[...], ng_ref[...]).astype(BF16)
    gw = MIX_W // GMLP_GROUPS
    for c in range(GMLP_TM // CHUNK):
        r = slice(c * CHUNK, (c + 1) * CHUNK)
        for g in range(GMLP_GROUPS):
            cs = slice(g * gw, (g + 1) * gw)
            mixed = jnp.dot(ws_ref[g], vn[r, cs], preferred_element_type=F32) + bs_ref[:, g:g + 1]
            o_ref[r, cs] = (gu_ref[r, cs] * mixed).astype(BF16)


def _gmlp(z, norm_g, w_s, b_s):
    tm = GMLP_TM
    return pl.pallas_call(
        _gmlp_kernel,
        grid=(T_ALL // tm,),
        in_specs=[pl.BlockSpec((tm, MIX_W), lambda i: (i, Z_GU // MIX_W)),
                  pl.BlockSpec((tm, MIX_W), lambda i: (i, Z_GV // MIX_W)),
                  pl.BlockSpec((1, MIX_W), lambda i: (0, 0)),
                  pl.BlockSpec((GMLP_GROUPS, CHUNK, CHUNK), lambda i: (0, 0, 0)),
                  pl.BlockSpec((CHUNK, GMLP_GROUPS), lambda i: (0, 0))],
        out_specs=pl.BlockSpec((tm, MIX_W), lambda i: (i, 0)),
        out_shape=jax.ShapeDtypeStruct((T_ALL, MIX_W), BF16),
        compiler_params=_cparams(("arbitrary",)),
        name="gmlp",
    )(z, z, norm_g.reshape(1, MIX_W), w_s.astype(BF16), b_s.T)


OUT_TM = 256


def _out_kernel(*refs, router):
    if router:
        (a_ref, b_ref, c_ref, d_ref, x_ref, mod_ref, g_ref, w_ref, rh_ref, rl_ref, rb_ref,
         x1_ref, h2_ref, rt_ref) = refs
    else:
        a_ref, b_ref, c_ref, d_ref, x_ref, mod_ref, g_ref, w_ref, x1_ref, h2_ref = refs
    mix = jnp.dot(a_ref[...], w_ref[0:MIX_W, :], preferred_element_type=F32)
    mix += jnp.dot(b_ref[...], w_ref[MIX_W:2 * MIX_W, :], preferred_element_type=F32)
    mix += jnp.dot(c_ref[...], w_ref[2 * MIX_W:3 * MIX_W, :], preferred_element_type=F32)
    mix += jnp.dot(d_ref[...], w_ref[3 * MIX_W:4 * MIX_W, :], preferred_element_type=F32)
    x1 = x_ref[...] + mod_ref[0, 2:3, :] * mix
    x1_ref[...] = x1
    h2 = _rms(x1, g_ref[...]) * (1.0 + mod_ref[0, 4:5, :]) + mod_ref[0, 3:4, :]
    hi = h2.astype(BF16)
    h2_ref[...] = h2 if router else hi
    if router:
        lo = (h2 - hi.astype(F32)).astype(BF16)
        logits = (jnp.dot(hi, rh_ref[...], preferred_element_type=F32)
                  + jnp.dot(lo, rh_ref[...], preferred_element_type=F32)
                  + jnp.dot(hi, rl_ref[...], preferred_element_type=F32)) + rb_ref[...]
        lane = lax.broadcasted_iota(jnp.int32, logits.shape, 1)
        neg_inf = jnp.float32(-jnp.inf)
        lg = jnp.where(lane < N_EXPERTS, logits, neg_inf)
        m1 = jnp.max(lg, axis=-1, keepdims=True)
        i1 = jnp.min(jnp.where(lg == m1, lane, LANE), axis=-1, keepdims=True)
        lg2 = jnp.where(lane == i1, neg_inf, lg)
        m2 = jnp.max(lg2, axis=-1, keepdims=True)
        i2 = jnp.min(jnp.where(lg2 == m2, lane, LANE), axis=-1, keepdims=True)
        e = jnp.exp(m2 - m1)
        w1 = 1.0 / (1.0 + e)
        w2 = e / (1.0 + e)
        rt = jnp.where(lane == 0, i1.astype(F32), 0.0)
        rt = jnp.where(lane == 1, i2.astype(F32), rt)
        rt = jnp.where(lane == 2, w1, rt)
        rt = jnp.where(lane == 3, w2, rt)
        rt_ref[...] = rt


def _out_proj(mixers, x, mod_l, g2, w_out, router=None):
    tm = OUT_TM
    row = lambda i: (i, 0)
    fixed = lambda i: (0, 0)
    in_specs = [pl.BlockSpec((tm, MIX_W), row)] * 4 + [
        pl.BlockSpec((tm, D_MODEL), row),
        pl.BlockSpec((1, MOD_CHUNKS, D_MODEL), lambda i: (_mod_group(i, tm), 0, 0)),
        pl.BlockSpec((1, D_MODEL), fixed),
        pl.BlockSpec((D_MODEL, D_MODEL), fixed, pipeline_mode=pl.Buffered(1))]
    args = list(mixers) + [x, mod_l, g2.reshape(1, D_MODEL), w_out]
    out_specs = [pl.BlockSpec((tm, D_MODEL), row), pl.BlockSpec((tm, D_MODEL), row)]
    out_shape = [jax.ShapeDtypeStruct((T_ALL, D_MODEL), F32),
                 jax.ShapeDtypeStruct((T_ALL, D_MODEL), BF16 if router is None else F32)]
    if router is not None:
        rw, rb = router
        rw = jnp.pad(rw, ((0, 0), (0, LANE - N_EXPERTS)))
        rh = rw.astype(BF16)
        rl = (rw - rh.astype(F32)).astype(BF16)
        in_specs += [pl.BlockSpec((D_MODEL, LANE), fixed)] * 2 + [pl.BlockSpec((1, LANE), fixed)]
        args += [rh, rl, jnp.pad(rb, (0, LANE - N_EXPERTS)).reshape(1, LANE)]
        out_specs.append(pl.BlockSpec((tm, LANE), row))
        out_shape.append(jax.ShapeDtypeStruct((T_ALL, LANE), F32))
    return pl.pallas_call(
        functools.partial(_out_kernel, router=router is not None),
        grid=(T_ALL // tm,),
        in_specs=in_specs, out_specs=out_specs, out_shape=out_shape,
        compiler_params=_cparams(("arbitrary",)),
        name="out_proj_router" if router is not None else "out_proj",
    )(*args)


def _ffn_kernel(te_ref, ns_ref, x_ref, wg_ref, wu_ref, wd_ref, o_ref):
    i = pl.program_id(0)
    f = pl.program_id(1)

    @pl.when(f == 0)
    def _():
        o_ref[...] = jnp.zeros_like(o_ref)

    def sub(s, carry):
        rows = pl.ds(pl.multiple_of(s * FFN_SUB, FFN_SUB), FFN_SUB)
        xs = x_ref[rows, :]
        g = jnp.dot(xs, wg_ref[0], preferred_element_type=F32)
        u = jnp.dot(xs, wu_ref[0], preferred_element_type=F32)
        a = (g * jax.nn.sigmoid(g) * u).astype(BF16)
        o_ref[rows, :] += jnp.dot(a, wd_ref[0], preferred_element_type=F32)
        return carry

    lax.fori_loop(0, ns_ref[i], sub, 0)


def _ffn(x, tile_e, tile_ns, w_gate, w_up, w_down):
    ntiles = x.shape[0] // FFN_TM
    nf = D_FF // FFN_TF

    def fcol(i, f, te, ns):
        return jnp.where(ns[i] > 0, f, nf - 1)

    return pl.pallas_call(
        _ffn_kernel,
        grid_spec=pltpu.PrefetchScalarGridSpec(
            num_scalar_prefetch=2,
            grid=(ntiles, nf),
            in_specs=[pl.BlockSpec((FFN_TM, D_MODEL), lambda i, f, te, ns: (i, 0)),
                      pl.BlockSpec((1, D_MODEL, FFN_TF), lambda i, f, te, ns: (te[i], 0, fcol(i, f, te, ns))),
                      pl.BlockSpec((1, D_MODEL, FFN_TF), lambda i, f, te, ns: (te[i], 0, fcol(i, f, te, ns))),
                      pl.BlockSpec((1, FFN_TF, D_MODEL), lambda i, f, te, ns: (te[i], fcol(i, f, te, ns), 0))],
            out_specs=pl.BlockSpec((FFN_TM, D_MODEL), lambda i, f, te, ns: (i, 0))),
        out_shape=jax.ShapeDtypeStruct((ntiles * FFN_TM, D_MODEL), F32),
        compiler_params=_cparams(("arbitrary", "arbitrary")),
        name="ffn",
    )(tile_e, tile_ns, x, w_gate, w_up, w_down)


def _moe_ffn_kernel(te_ref, ns_ref, src_ref, x_hbm, wg_ref, wu_ref, wd_ref, o_ref, xf, xb, sems):
    i = pl.program_id(0)
    f = pl.program_id(1)
    ntiles = pl.num_programs(0)

    def row_copy(src_row, slot, r):
        return pltpu.make_async_copy(x_hbm.at[pl.ds(src_row, 1)], xf.at[slot, pl.ds(r, 1)], sems.at[slot])

    def issue(tile, slot):
        def body(r, carry):
            row_copy(src_ref[tile * FFN_TM + r], slot, r).start()
            return carry

        lax.fori_loop(0, ns_ref[tile] * FFN_SUB, body, 0)

    def wait_rows(tile, slot):
        def body(r, carry):
            row_copy(0, slot, r).wait()
            return carry

        lax.fori_loop(0, ns_ref[tile] * FFN_SUB, body, 0)

    @pl.when(f == 0)
    def _():
        slot = i % 2

        @pl.when(i == 0)
        def _():
            issue(0, 0)

        wait_rows(i, slot)

        @pl.when(i + 1 < ntiles)
        def _():
            issue(i + 1, 1 - slot)

        def cast(s, carry):
            rows = pl.ds(pl.multiple_of(s * FFN_SUB, FFN_SUB), FFN_SUB)
            xb[rows, :] = xf[slot, rows, :].astype(BF16)
            return carry

        lax.fori_loop(0, ns_ref[i], cast, 0)
        o_ref[...] = jnp.zeros_like(o_ref)

    def sub(s, carry):
        rows = pl.ds(pl.multiple_of(s * FFN_SUB, FFN_SUB), FFN_SUB)
        xs = xb[rows, :]
        g = jnp.dot(xs, wg_ref[0], preferred_element_type=F32)
        u = jnp.dot(xs, wu_ref[0], preferred_element_type=F32)
        a = (g * jax.nn.sigmoid(g) * u).astype(BF16)
        o_ref[rows, :] += jnp.dot(a, wd_ref[0], preferred_element_type=F32)
        return carry

    lax.fori_loop(0, ns_ref[i], sub, 0)


def _moe_ffn(h2, tile_e, tile_ns, src, w_gate, w_up, w_down):
    nf = D_FF // FFN_TF

    def fcol(i, f, ns):
        return jnp.where(ns[i] > 0, f, nf - 1)

    return pl.pallas_call(
        _moe_ffn_kernel,
        grid_spec=pltpu.PrefetchScalarGridSpec(
            num_scalar_prefetch=3,
            grid=(MOE_TILES, nf),
            in_specs=[pl.BlockSpec(memory_space=pl.ANY),
                      pl.BlockSpec((1, D_MODEL, FFN_TF), lambda i, f, te, ns, sr: (te[i], 0, fcol(i, f, ns))),
                      pl.BlockSpec((1, D_MODEL, FFN_TF), lambda i, f, te, ns, sr: (te[i], 0, fcol(i, f, ns))),
                      pl.BlockSpec((1, FFN_TF, D_MODEL), lambda i, f, te, ns, sr: (te[i], fcol(i, f, ns), 0))],
            out_specs=pl.BlockSpec((FFN_TM, D_MODEL), lambda i, f, te, ns, sr: (i, 0)),
            scratch_shapes=[pltpu.VMEM((2, FFN_TM, D_MODEL), F32),
                            pltpu.VMEM((FFN_TM, D_MODEL), BF16),
                            pltpu.SemaphoreType.DMA((2,))]),
        out_shape=jax.ShapeDtypeStruct((MOE_TILES * FFN_TM, D_MODEL), F32),
        compiler_params=_cparams(("arbitrary", "arbitrary")),
        name="moe_ffn",
    )(tile_e, tile_ns, src, h2, w_gate, w_up, w_down)


RES_TM = 256


def _res_kernel(x_ref, y_ref, mod_ref, g_ref, o_ref, *, final):
    x2 = x_ref[...] + mod_ref[0, 5:6, :] * y_ref[...]
    o_ref[...] = _rms(x2, g_ref[...]) if final else x2


def _moe_res_kernel(slot_ref, x_ref, rt_ref, mod_ref, g_ref, ys_hbm, o_ref, ybuf, sems, *, final):
    i = pl.program_id(0)
    nt = pl.num_programs(0)

    def row_copy(src_row, buf, k, r):
        return pltpu.make_async_copy(ys_hbm.at[pl.ds(src_row, 1)], ybuf.at[buf, k, pl.ds(r, 1)], sems.at[buf])

    def issue(tile, buf):
        def body(r, carry):
            for k in range(2):
                row_copy(slot_ref[k * T_ALL + tile * RES_TM + r], buf, k, r).start()
            return carry

        lax.fori_loop(0, RES_TM, body, 0)

    buf = i % 2

    @pl.when(i == 0)
    def _():
        issue(0, 0)

    def wait_body(r, carry):
        for k in range(2):
            row_copy(0, buf, k, r).wait()
        return carry

    lax.fori_loop(0, RES_TM, wait_body, 0)

    @pl.when(i + 1 < nt)
    def _():
        issue(i + 1, 1 - buf)

    rt = rt_ref[...]
    ffn = rt[:, 2:3] * ybuf[buf, 0] + rt[:, 3:4] * ybuf[buf, 1]
    x2 = x_ref[...] + mod_ref[0, 5:6, :] * ffn
    o_ref[...] = _rms(x2, g_ref[...]) if final else x2


def _residual(x1, y, mod_l, final_g, *, route=None, slot=None, final):
    tm = RES_TM
    nt = T_ALL // tm
    out_shape = jax.ShapeDtypeStruct((T_ALL, D_MODEL), F32)
    if route is None:
        row = lambda i: (i, 0)
        return pl.pallas_call(
            functools.partial(_res_kernel, final=final),
            grid=(nt,),
            in_specs=[pl.BlockSpec((tm, D_MODEL), row), pl.BlockSpec((tm, D_MODEL), row),
                      pl.BlockSpec((1, MOD_CHUNKS, D_MODEL), lambda i: (_mod_group(i, tm), 0, 0)),
                      pl.BlockSpec((1, D_MODEL), lambda i: (0, 0))],
            out_specs=pl.BlockSpec((tm, D_MODEL), row),
            out_shape=out_shape,
            compiler_params=_cparams(("arbitrary",)),
            name="ffn_residual",
        )(x1, y, mod_l, final_g.reshape(1, D_MODEL))
    row = lambda i, sl: (i, 0)
    return pl.pallas_call(
        functools.partial(_moe_res_kernel, final=final),
        grid_spec=pltpu.PrefetchScalarGridSpec(
            num_scalar_prefetch=1,
            grid=(nt,),
            in_specs=[pl.BlockSpec((tm, D_MODEL), row), pl.BlockSpec((tm, LANE), row),
                      pl.BlockSpec((1, MOD_CHUNKS, D_MODEL), lambda i, sl: (_mod_group(i, tm), 0, 0)),
                      pl.BlockSpec((1, D_MODEL), lambda i, sl: (0, 0)),
                      pl.BlockSpec(memory_space=pl.ANY)],
            out_specs=pl.BlockSpec((tm, D_MODEL), row),
            scratch_shapes=[pltpu.VMEM((2, 2, tm, D_MODEL), F32), pltpu.SemaphoreType.DMA((2,))]),
        out_shape=out_shape,
        compiler_params=_cparams(("arbitrary",)),
        name="moe_residual",
    )(slot, x1, route, mod_l, final_g.reshape(1, D_MODEL), y)


def _route_plan(route):
    e_flat = jnp.concatenate([route[:, 0], route[:, 1]]).astype(jnp.int32)
    onehot = (e_flat[:, None] == jnp.arange(N_EXPERTS, dtype=jnp.int32)[None, :]).astype(jnp.int32)
    ranks = jnp.cumsum(onehot, axis=0) - onehot
    rank = jnp.sum(ranks * onehot, axis=1)
    counts = jnp.sum(onehot, axis=0)
    tiles = (counts + FFN_TM - 1) // FFN_TM
    tile_start = jnp.cumsum(tiles) - tiles
    slot = tile_start[e_flat] * FFN_TM + rank
    n_slots = MOE_TILES * FFN_TM
    tok = jnp.concatenate([jnp.arange(T_ALL, dtype=jnp.int32)] * 2)
    src = jnp.zeros((n_slots,), jnp.int32).at[slot].set(tok)
    tile_ids = jnp.arange(MOE_TILES, dtype=jnp.int32)
    ends = jnp.cumsum(tiles)
    tile_e = jnp.sum((tile_ids[:, None] >= ends[None, :]).astype(jnp.int32), axis=1)
    used = tile_e < N_EXPERTS
    last_e = jnp.max(jnp.where(counts > 0, jnp.arange(N_EXPERTS, dtype=jnp.int32), 0))
    tile_e = jnp.where(used, tile_e, last_e)
    rows_in_tile = jnp.clip(counts[tile_e] - (tile_ids - tile_start[tile_e]) * FFN_TM, 0, FFN_TM)
    tile_ns = jnp.where(used, (rows_in_tile + FFN_SUB - 1) // FFN_SUB, 0).astype(jnp.int32)
    return src, slot, tile_e.astype(jnp.int32), tile_ns


def _rope_tables():
    length = DEC_SEQ
    r = jnp.repeat(jnp.arange(length // GRID_W, dtype=F32), GRID_W)
    col = (jnp.arange(length) % GRID_W).astype(F32)
    half = HEAD_DIM // 2
    inv = ROPE_THETA ** (-jnp.arange(0, half, 2, dtype=F32) / half)
    ar, ac = r[:, None] * inv, col[:, None] * inv
    cos = jnp.concatenate([jnp.cos(ar), jnp.cos(ar), jnp.cos(ac), jnp.cos(ac)], axis=-1)
    sin = jnp.concatenate([-jnp.sin(ar), jnp.sin(ar), -jnp.sin(ac), jnp.sin(ac)], axis=-1)
    return cos, sin


def kernel(x_prompt, x_sample, c, c_ctx, cache_attn_k, cache_attn_v, state_s5_re, state_s5_im, state_mlstm_c, state_mlstm_n, state_mlstm_m, norm1_g, norm2_g, w_mod, b_mod, w_in, w_out, q_norm_g, k_norm_g, s5_a_re, s5_a_im, s5_log_dt, s5_b_re, s5_b_im, s5_c_re, s5_c_im, s5_d, s5_w_glu, s5_b_glu, mlstm_i_bias, mlstm_f_bias, mlstm_norm_g, gmlp_norm_g, gmlp_w_s, gmlp_b_s, ffn_w_gate, ffn_w_up, ffn_w_down, moe_router, moe_router_bias, moe_w_gate, moe_w_up, moe_w_down, final_norm_g):
    x = jnp.concatenate([x_prompt.reshape(T_CTX, D_MODEL), x_sample.reshape(T_LAT, D_MODEL)], axis=0)
    cond = jnp.concatenate([c_ctx[None, :], c], axis=0)
    mod = _modulation(cond, w_mod, b_mod)
    cos, sin = _rope_tables()
    nh = MLSTM_HEADS
    zeros_s5 = jnp.zeros((BATCH, 2, S5_SG, 1, S5_SGW), F32)
    zeros_c = jnp.zeros((BATCH, 2, nh, MLSTM_DIM, MLSTM_DIM), F32)
    zeros_n = jnp.zeros((BATCH, 2, nh, 1, MLSTM_DIM), F32)
    zeros_m = jnp.zeros((BATCH, 2, nh, 1, 1), F32)
    dense_e = jnp.zeros((T_ALL // FFN_TM,), jnp.int32)
    dense_ns = jnp.full((T_ALL // FFN_TM,), FFN_TM // FFN_SUB, jnp.int32)

    ctx_states = []
    for l in range(DEPTH):
        use_moe = l % 2 == 1
        j = l // 2
        wl = w_in[l]
        w_p = jnp.concatenate([wl[:, :3584], wl[:, 3600:], wl[:, 3584:3600],
                               jnp.zeros((D_MODEL, Z_COLS - 4624), F32)], axis=1).astype(BF16)
        z = _in_proj(x, mod[l], norm1_g[l], w_p)

        att_c, k_new, v_new = _attention(z, None, q_norm_g[l], k_norm_g[l], ctx=True)
        (att_l,) = _attention(z, None, q_norm_g[l], k_norm_g[l], ctx=False,
                              kpast=cache_attn_k[:, l], vpast=cache_attn_v[:, l], cos=cos, sin=sin)
        att = jnp.concatenate([att_c, att_l], axis=0)

        prep = _s5_prep(s5_a_re[l], s5_a_im[l], s5_log_dt[l], s5_b_re[l], s5_b_im[l], s5_c_re[l], s5_c_im[l])
        y_c, hf_re, hf_im = _s5_scan(z, prep, zeros_s5, zeros_s5, ctx=True)
        st = lambda s: s[:, l].reshape(DEC_BATCH, 2, S5_SG, 1, S5_SGW)
        y_l, _, _ = _s5_scan(z, prep, st(state_s5_re), st(state_s5_im), ctx=False)
        s5_out = _s5_post(jnp.concatenate([y_c, y_l], axis=1), z, s5_d[l], s5_w_glu[l], s5_b_glu[l])

        mg = z[:, Z_MG:Z_MG + 16].reshape(T_ALL, 2, 2, nh)
        gcol = mg.transpose(3, 0, 1, 2).reshape(nh, T_ALL, 4)
        grow = mg.reshape(T_ALL // CHUNK, CHUNK, 4, nh).transpose(3, 0, 2, 1)
        bias = jnp.stack([mlstm_i_bias[l], mlstm_f_bias[l]], axis=1)
        bias_c = bias.transpose(2, 0, 1).reshape(nh, 1, 4)
        bias_r = bias_c.reshape(nh, 4, 1)
        ml_c_out, c_new, n_new, m_new = _mlstm(z, gcol, grow, bias_c, bias_r, zeros_c, zeros_n, zeros_m,
                                               mlstm_norm_g[l], ctx=True)
        ml_l_out, _, _, _ = _mlstm(z, gcol, grow, bias_c, bias_r, state_mlstm_c[:, l],
                                   state_mlstm_n[:, l].reshape(DEC_BATCH, 2, nh, 1, MLSTM_DIM),
                                   state_mlstm_m[:, l].reshape(DEC_BATCH, 2, nh, 1, 1),
                                   mlstm_norm_g[l], ctx=False)
        ml_out = jnp.concatenate([ml_c_out, ml_l_out], axis=0)

        gm_out = _gmlp(z, gmlp_norm_g[l], gmlp_w_s[l], gmlp_b_s[l])

        ctx_states.append((k_new, v_new,
                           hf_re.reshape(BATCH, 2, S5_GROUPS, S5_STATE), hf_im.reshape(BATCH, 2, S5_GROUPS, S5_STATE),
                           c_new, n_new.reshape(BATCH, 2, nh, MLSTM_DIM), m_new.reshape(BATCH, 2, nh)))

        w_o = w_out[l].astype(BF16)
        final = l == DEPTH - 1
        if use_moe:
            x1, h2, route = _out_proj((att, s5_out, ml_out, gm_out), x, mod[l], norm2_g[l], w_o,
                                      router=(moe_router[j], moe_router_bias[j]))
            src, slot, tile_e, tile_ns = _route_plan(route)
            ys = _moe_ffn(h2, tile_e, tile_ns, src, moe_w_gate[j].astype(BF16), moe_w_up[j].astype(BF16),
                          moe_w_down[j].astype(BF16))
            x = _residual(x1, ys, mod[l], final_norm_g, route=route, slot=slot, final=final)
        else:
            x1, h2 = _out_proj((att, s5_out, ml_out, gm_out), x, mod[l], norm2_g[l], w_o)
            y = _ffn(h2, dense_e, dense_ns, ffn_w_gate[j][None].astype(BF16), ffn_w_up[j][None].astype(BF16),
                     ffn_w_down[j][None].astype(BF16))
            x = _residual(x1, y, mod[l], final_norm_g, final=final)

    if DEPTH % 2 == 0 or True:
        pass
    y_prompt = x[:T_CTX].reshape(BATCH, SEQ, D_MODEL)
    y_sample = x[T_CTX:].reshape(DEC_BATCH, DEC_SEQ, D_MODEL)
    stack = lambda i: jnp.stack([s[i] for s in ctx_states], axis=1)
    return (y_prompt, y_sample, stack(0), stack(1), stack(2), stack(3), stack(4), stack(5), stack(6))
```

```python
import functools
import math

import numpy as np
import jax
import jax.numpy as jnp
from jax import lax
from jax.experimental import pallas as pl
from jax.experimental.pallas import tpu as pltpu

F32 = jnp.float32
BF16 = jnp.bfloat16

D_MODEL = 2048
BATCH = 16
SEQ = 256
DEPTH = 2
DEC_BATCH = 2
DEC_SEQ = 2048
PAST_LEN = 256
GRID_W = 64
MIX_W = 512
ATT_HEADS = 4
ATT_KV_HEADS = 2
HEAD_DIM = 128
ROPE_THETA = 10000.0
S5_CH = 16
S5_GROUPS = 32
S5_STATE = 64
MLSTM_HEADS = 4
MLSTM_DIM = 128
CHUNK = 128
GMLP_GROUPS = 4
D_FF = 7168
N_EXPERTS = 8
MOD_CHUNKS = 6
EPS = 1e-6

T_CTX = BATCH * SEQ
T_LAT = DEC_BATCH * DEC_SEQ
T_ALL = T_CTX + T_LAT
N_GROUPS_MOD = 1 + DEC_BATCH

Z_AQ, Z_AK, Z_AV, Z_SX = 0, 512, 768, 1024
Z_MQ, Z_MK, Z_MV, Z_MO = 1536, 2048, 2560, 3072
Z_GU, Z_GV, Z_MG = 3584, 4096, 4608
Z_COLS = 4736
LANE = 128
SUBLANE = 8

VMEM_LIMIT = 56 * 1024 * 1024

S5_SG = 4
S5_SGW = 8 * S5_STATE
S5_LAGS = 8

FFN_TM = 1024
FFN_SUB = 256
FFN_DENSE_SUB = 512
FFN_TF = 512
MOE_TILES = 2 * T_ALL // FFN_TM + N_EXPERTS


def _cparams(sem=None):
    return pltpu.CompilerParams(dimension_semantics=sem, vmem_limit_bytes=VMEM_LIMIT)


def _mod_group(i, tm):
    return jnp.maximum(i * tm // DEC_SEQ - (T_CTX // DEC_SEQ - 1), 0)


def _rms(x, g):
    return x * lax.rsqrt(jnp.mean(x * x, axis=-1, keepdims=True) + EPS) * g


MOD_TN = 512


def _mod_kernel(cb_ref, w_ref, b_ref, o_ref):
    w = w_ref[0]
    rows = []
    for r in range(N_GROUPS_MOD):
        c = cb_ref[r]
        s = c * jax.nn.sigmoid(c)
        parts = [jnp.sum(w[:, j * LANE:(j + 1) * LANE] * s, axis=0, keepdims=True)
                 for j in range(MOD_TN // LANE)]
        rows.append(jnp.concatenate(parts, axis=1))
    rows.append(jnp.zeros((SUBLANE - N_GROUPS_MOD, MOD_TN), F32))
    o_ref[0] = jnp.concatenate(rows, axis=0) + b_ref[0]


def _modulation(cond, w_mod, b_mod):
    cb = jnp.broadcast_to(cond[:, :, None], (N_GROUPS_MOD, D_MODEL, LANE))
    n = MOD_CHUNKS * D_MODEL
    out = pl.pallas_call(
        _mod_kernel,
        grid=(DEPTH, n // MOD_TN),
        in_specs=[pl.BlockSpec((N_GROUPS_MOD, D_MODEL, LANE), lambda l, j: (0, 0, 0)),
                  pl.BlockSpec((1, D_MODEL, MOD_TN), lambda l, j: (l, 0, j)),
                  pl.BlockSpec((1, 1, MOD_TN), lambda l, j: (l, 0, j))],
        out_specs=pl.BlockSpec((1, SUBLANE, MOD_TN), lambda l, j: (l, 0, j)),
        out_shape=jax.ShapeDtypeStruct((DEPTH, SUBLANE, n), F32),
        compiler_params=_cparams(("arbitrary", "arbitrary")),
        name="adaln_mod",
    )(cb, w_mod, b_mod.reshape(DEPTH, 1, n))
    return out[:, :N_GROUPS_MOD].reshape(DEPTH, N_GROUPS_MOD, MOD_CHUNKS, D_MODEL)


IN_TM = 256
IN_CHUNK = 512


def _pair_specs(tm, block, rows_dim=0):
    nc = T_CTX // tm

    def index(row):
        return tuple(row if d == rows_dim else 0 for d in range(len(block)))

    return [pl.BlockSpec(block, lambda i, *_: index(jnp.minimum(i, nc - 1))),
            pl.BlockSpec(block, lambda i, *_: index(jnp.maximum(i - nc, 0)))]


def _pick(tm, ctx_ref, lat_ref):
    return jnp.where(pl.program_id(0) < T_CTX // tm, ctx_ref[...], lat_ref[...])


def _in_kernel(xc_ref, xl_ref, mod_ref, g_ref, w_ref, z_ref):
    x = _pick(IN_TM, xc_ref, xl_ref)
    shift = mod_ref[0, 0:1, :]
    scale = mod_ref[0, 1:2, :]
    h = (_rms(x, g_ref[...]) * (1.0 + scale) + shift).astype(BF16)
    for c0 in range(0, Z_COLS, IN_CHUNK):
        cw = min(IN_CHUNK, Z_COLS - c0)
        z_ref[:, c0:c0 + cw] = jnp.dot(h, w_ref[:, c0:c0 + cw], preferred_element_type=F32)


def _in_proj(x_pair, mod_l, g, w_p):
    return pl.pallas_call(
        _in_kernel,
        grid=(T_ALL // IN_TM,),
        in_specs=_pair_specs(IN_TM, (IN_TM, D_MODEL)) + [
                  pl.BlockSpec((1, MOD_CHUNKS, D_MODEL), lambda i: (_mod_group(i, IN_TM), 0, 0)),
                  pl.BlockSpec((1, D_MODEL), lambda i: (0, 0)),
                  pl.BlockSpec((D_MODEL, Z_COLS), lambda i: (0, 0), pipeline_mode=pl.Buffered(1))],
        out_specs=pl.BlockSpec((IN_TM, Z_COLS), lambda i: (i, 0)),
        out_shape=jax.ShapeDtypeStruct((T_ALL, Z_COLS), F32),
        compiler_params=_cparams(("arbitrary",)),
        name="in_proj",
    )(*x_pair, mod_l, g.reshape(1, D_MODEL), w_p)


def _rope(t, c, s):
    lane = lax.broadcasted_iota(jnp.int32, t.shape, 1)
    first = (lane % (HEAD_DIM // 2)) < (HEAD_DIM // 4)
    swapped = jnp.where(first, pltpu.roll(t, HEAD_DIM - HEAD_DIM // 4, 1), pltpu.roll(t, HEAD_DIM // 4, 1))
    return t * c + swapped * s


def _attn_kernel(*refs, seq, past, rope):
    if rope:
        (aq_ref, ak_ref, av_ref, kp_ref, vp_ref, cos_ref, sin_ref, qg_ref, kg_ref,
         att_ref, kb_scr, vb_scr) = refs
    else:
        aq_ref, ak_ref, av_ref, qg_ref, kg_ref, att_ref, knew_ref, vnew_ref, kb_scr, vb_scr = refs
    kn = _rms(ak_ref[...], kg_ref[...])
    v = av_ref[...]
    if rope:
        kn = _rope(kn, cos_ref[...], sin_ref[...])
        kb_scr[seq:seq + past, :] = kp_ref[0, 0].astype(BF16)
        vb_scr[seq:seq + past, :] = vp_ref[0, 0].astype(BF16)
    else:
        knew_ref[0, 0] = kn
        vnew_ref[0, 0] = v
    kb_scr[0:seq, :] = kn.astype(BF16)
    vb_scr[0:seq, :] = v.astype(BF16)
    grp = ATT_HEADS // ATT_KV_HEADS

    def q_block(qb, carry):
        rows = pl.ds(pl.multiple_of(qb * CHUNK, CHUNK), CHUNK)
        qs = []
        for g in range(grp):
            q = _rms(aq_ref[rows, g * HEAD_DIM:(g + 1) * HEAD_DIM], qg_ref[...])
            if rope:
                q = _rope(q, cos_ref[rows, :], sin_ref[rows, :])
            qs.append(q)
        q2 = jnp.concatenate(qs, axis=0).astype(BF16)
        s = lax.dot_general(q2, kb_scr[...], (((1,), (1,)), ((), ())),
                            preferred_element_type=F32) * (HEAD_DIM ** -0.5)
        m = jnp.max(s, axis=-1, keepdims=True)
        p = jnp.exp(s - m)
        den = jnp.sum(p, axis=-1, keepdims=True)
        o = jnp.dot(p.astype(BF16), vb_scr[...], preferred_element_type=F32) / den
        for g in range(grp):
            att_ref[rows, g * HEAD_DIM:(g + 1) * HEAD_DIM] = o[g * CHUNK:(g + 1) * CHUNK].astype(BF16)
        return carry

    lax.fori_loop(0, seq // CHUNK, q_block, 0)


def _attention(z, att_out_shape, qg, kg, *, ctx, kpast=None, vpast=None, cos=None, sin=None):
    seq = SEQ if ctx else DEC_SEQ
    nb = BATCH if ctx else DEC_BATCH
    row0 = 0 if ctx else T_CTX // DEC_SEQ
    past = 0 if ctx else PAST_LEN
    qw = HEAD_DIM * (ATT_HEADS // ATT_KV_HEADS)
    in_specs = [pl.BlockSpec((seq, qw), lambda b, h: (row0 + b, Z_AQ // qw + h)),
                pl.BlockSpec((seq, HEAD_DIM), lambda b, h: (row0 + b, Z_AK // HEAD_DIM + h)),
                pl.BlockSpec((seq, HEAD_DIM), lambda b, h: (row0 + b, Z_AV // HEAD_DIM + h))]
    args = [z, z, z]
    if not ctx:
        in_specs += [pl.BlockSpec((1, 1, past, HEAD_DIM), lambda b, h: (b, h, 0, 0)),
                     pl.BlockSpec((1, 1, past, HEAD_DIM), lambda b, h: (b, h, 0, 0)),
                     pl.BlockSpec((seq, HEAD_DIM), lambda b, h: (0, 0)),
                     pl.BlockSpec((seq, HEAD_DIM), lambda b, h: (0, 0))]
        args += [kpast, vpast, cos, sin]
    in_specs += [pl.BlockSpec((1, HEAD_DIM), lambda b, h: (0, 0)),
                 pl.BlockSpec((1, HEAD_DIM), lambda b, h: (0, 0))]
    args += [qg.reshape(1, HEAD_DIM), kg.reshape(1, HEAD_DIM)]
    out_specs = [pl.BlockSpec((seq, qw), lambda b, h: (b, h))]
    out_shape = [jax.ShapeDtypeStruct((nb * seq, MIX_W), BF16)]
    if ctx:
        out_specs += [pl.BlockSpec((1, 1, seq, HEAD_DIM), lambda b, h: (b, h, 0, 0))] * 2
        out_shape += [jax.ShapeDtypeStruct((nb, ATT_KV_HEADS, seq, HEAD_DIM), F32)] * 2
    return pl.pallas_call(
        functools.partial(_attn_kernel, seq=seq, past=past, rope=not ctx),
        grid=(nb, ATT_KV_HEADS),
        in_specs=in_specs, out_specs=out_specs, out_shape=out_shape,
        scratch_shapes=[pltpu.VMEM((seq + past, HEAD_DIM), BF16),
                        pltpu.VMEM((seq + past, HEAD_DIM), BF16)],
        compiler_params=_cparams(("arbitrary", "arbitrary")),
        name="attn_ctx" if ctx else "attn_lat",
    )(*args)


def _s5_prep_kernel(are_ref, aim_ref, ldt_ref, bre_ref, bim_ref, pre_ref, pim_ref, wre_ref, wim_ref):
    a_re = are_ref[...]
    a_im = aim_ref[...]
    dt = jnp.exp(ldt_ref[...])
    pows = []
    for tau in range(S5_LAGS + 1):
        mag = jnp.exp((tau * dt) * a_re)
        ang = (tau * dt) * a_im
        pr, pi = mag * jnp.cos(ang), mag * jnp.sin(ang)
        pre_ref[tau] = pr
        pim_ref[tau] = pi
        pows.append((pr, pi))
    nr, ni = pows[1][0] - 1.0, pows[1][1]
    den = a_re * a_re + a_im * a_im
    cr = (nr * a_re + ni * a_im) / den
    ci = (ni * a_re - nr * a_im) / den
    for d in range(2):
        b_r, b_i = bre_ref[d], bim_ref[d]
        bb_r = cr[d:d + 1] * b_r - ci[d:d + 1] * b_i
        bb_i = cr[d:d + 1] * b_i + ci[d:d + 1] * b_r
        for tau in range(S5_LAGS):
            pr, pi = pows[tau][0][d:d + 1], pows[tau][1][d:d + 1]
            wre_ref[d, tau] = pr * bb_r - pi * bb_i
            wim_ref[d, tau] = pr * bb_i + pi * bb_r


def _s5_prep(a_re, a_im, log_dt, b_re, b_im, c_re, c_im):
    gp = S5_GROUPS * S5_STATE
    ldt = jnp.broadcast_to(log_dt[:, :, None], (2, S5_GROUPS, S5_STATE)).reshape(2, gp)
    bt = lambda b: b.transpose(0, 3, 1, 2).reshape(2, S5_CH, gp)
    pre, pim, wre, wim = pl.pallas_call(
        _s5_prep_kernel,
        out_shape=[jax.ShapeDtypeStruct((S5_LAGS + 1, 2, gp), F32)] * 2
        + [jax.ShapeDtypeStruct((2, S5_LAGS, S5_CH, gp), F32)] * 2,
        name="s5_prep",
    )(a_re.reshape(2, gp), a_im.reshape(2, gp), ldt, bt(b_re), bt(b_im))
    eye = jnp.eye(8, dtype=F32)

    def w_layout(w):
        w = w.reshape(2, S5_LAGS, S5_CH, S5_SG, 8, S5_STATE).transpose(0, 3, 1, 4, 2, 5)
        w = w[:, :, :, :, :, None, :] * eye[None, None, None, :, None, :, None]
        return w.reshape(2, S5_SG, S5_LAGS * LANE, S5_SGW)

    w_in = jnp.concatenate([w_layout(wre), w_layout(wim)], axis=-1).astype(BF16)

    def c_layout(c):
        c = c.reshape(2, S5_SG, 8, S5_CH, S5_STATE).transpose(0, 1, 2, 4, 3)
        c = c[:, :, :, :, None, :] * eye[None, None, :, None, :, None]
        return c.reshape(2, S5_SG, S5_SGW, LANE)

    w_out = jnp.concatenate([c_layout(c_re), -c_layout(c_im)], axis=2).astype(BF16)

    def p_layout(p):
        p = p[1:].reshape(S5_LAGS, 2, S5_SG, S5_SGW).transpose(1, 2, 0, 3)
        return jnp.stack([p[0], p[1, :, ::-1]], axis=0)

    def a8_layout(p):
        p = p[S5_LAGS].reshape(2, S5_SG, 1, S5_SGW)
        return jnp.broadcast_to(p, (2, S5_SG, SUBLANE, S5_SGW))

    return w_in, w_out, a8_layout(pre), a8_layout(pim), p_layout(pre), p_layout(pim)


def _s5_kernel(u_ref, w_ref, c_ref, a8r_ref, a8i_ref, pwr_ref, pwi_ref, h0r_ref, h0i_ref,
               y_ref, hfr_ref, hfi_ref, upad, wbr, wbi, *, seq, tc):
    d = pl.program_id(0)
    zeros = jnp.zeros((SUBLANE, LANE), F32)
    upad[0:SUBLANE, :] = zeros
    upad[SUBLANE:seq + SUBLANE, :] = u_ref[...]
    upad[seq + SUBLANE:seq + 2 * SUBLANE, :] = zeros
    nch = seq // tc
    nt = tc // SUBLANE
    a8r = a8r_ref[0, 0]
    a8i = a8i_ref[0, 0]

    def run(fwd):
        h0r = h0r_ref[0, 0, 0]
        h0i = h0i_ref[0, 0, 0]
        pwr = pwr_ref[0, 0]
        pwi = pwi_ref[0, 0]
        hr = jnp.zeros((SUBLANE, S5_SGW), F32)
        hi = jnp.zeros((SUBLANE, S5_SGW), F32)
        for ci in range(nch):
            c = ci if fwd else nch - 1 - ci
            if fwd:
                win = upad[c * tc:c * tc + tc + SUBLANE, :]
                lags = [pltpu.roll(win, tau, 0)[SUBLANE:SUBLANE + tc] if tau else win[SUBLANE:SUBLANE + tc]
                        for tau in range(S5_LAGS)]
            else:
                win = upad[c * tc + SUBLANE:c * tc + tc + 2 * SUBLANE, :]
                lags = [pltpu.roll(win, tc + SUBLANE - tau, 0)[0:tc] if tau else win[0:tc]
                        for tau in range(S5_LAGS)]
            lhs = jnp.concatenate([x.astype(BF16) for x in lags], axis=1)
            w = jnp.dot(lhs, w_ref[0, 0], preferred_element_type=F32)
            wbr[...] = w[:, :S5_SGW]
            wbi[...] = w[:, S5_SGW:]
            if ci == 0:
                r0 = 0 if fwd else tc - SUBLANE
                wbr[r0:r0 + SUBLANE, :] = wbr[r0:r0 + SUBLANE, :] + (pwr * h0r - pwi * h0i)
                wbi[r0:r0 + SUBLANE, :] = wbi[r0:r0 + SUBLANE, :] + (pwr * h0i + pwi * h0r)

            def step(i, carry):
                cr, ci_ = carry
                t = i if fwd else nt - 1 - i
                rows = pl.ds(pl.multiple_of(t * SUBLANE, SUBLANE), SUBLANE)
                nr = a8r * cr - a8i * ci_ + wbr[rows, :]
                ni = a8r * ci_ + a8i * cr + wbi[rows, :]
                wbr[rows, :] = nr
                wbi[rows, :] = ni
                return nr, ni

            hr, hi = lax.fori_loop(0, nt, step, (hr, hi))
            hcat = jnp.concatenate([wbr[...].astype(BF16), wbi[...].astype(BF16)], axis=1)
            y_ref[0, c * tc:(c + 1) * tc, :] = jnp.dot(hcat, c_ref[0, 0], preferred_element_type=F32)
        last = SUBLANE - 1 if fwd else 0
        hfr_ref[0, 0, 0] = hr[last:last + 1]
        hfi_ref[0, 0, 0] = hi[last:last + 1]

    @pl.when(d == 0)
    def _():
        run(True)

    @pl.when(d == 1)
    def _():
        run(False)


def _s5_scan(z, prep, h0r, h0i, *, ctx):
    w_in, w_out, a8r, a8i, pwr, pwi = prep
    seq = SEQ if ctx else DEC_SEQ
    nb = BATCH if ctx else DEC_BATCH
    row0 = 0 if ctx else T_CTX // DEC_SEQ
    tc = min(seq, 512)
    par = lambda d, s, b: (d, s, 0, 0)
    st = lambda d, s, b: (b, d, s, 0, 0)
    return pl.pallas_call(
        functools.partial(_s5_kernel, seq=seq, tc=tc),
        grid=(2, S5_SG, nb),
        in_specs=[pl.BlockSpec((seq, LANE), lambda d, s, b: (row0 + b, Z_SX // LANE + s)),
                  pl.BlockSpec((1, 1, S5_LAGS * LANE, 2 * S5_SGW), par),
                  pl.BlockSpec((1, 1, 2 * S5_SGW, LANE), par),
                  pl.BlockSpec((1, 1, SUBLANE, S5_SGW), par),
                  pl.BlockSpec((1, 1, SUBLANE, S5_SGW), par),
                  pl.BlockSpec((1, 1, SUBLANE, S5_SGW), par),
                  pl.BlockSpec((1, 1, SUBLANE, S5_SGW), par),
                  pl.BlockSpec((1, 1, 1, 1, S5_SGW), st),
                  pl.BlockSpec((1, 1, 1, 1, S5_SGW), st)],
        out_specs=[pl.BlockSpec((1, seq, LANE), lambda d, s, b: (d, b, s)),
                   pl.BlockSpec((1, 1, 1, 1, S5_SGW), st),
                   pl.BlockSpec((1, 1, 1, 1, S5_SGW), st)],
        out_shape=[jax.ShapeDtypeStruct((2, nb * seq, MIX_W), F32),
                   jax.ShapeDtypeStruct((nb, 2, S5_SG, 1, S5_SGW), F32),
                   jax.ShapeDtypeStruct((nb, 2, S5_SG, 1, S5_SGW), F32)],
        scratch_shapes=[pltpu.VMEM((seq + 2 * SUBLANE, LANE), F32),
                        pltpu.VMEM((tc, S5_SGW), F32),
                        pltpu.VMEM((tc, S5_SGW), F32)],
        compiler_params=_cparams(("arbitrary", "arbitrary", "arbitrary")),
        name="s5_ctx" if ctx else "s5_lat",
    )(z, w_in, w_out, a8r, a8i, pwr, pwi, h0r, h0i)


S5_POST_TM = 512


def _s5_post_kernel(yc_ref, yl_ref, sx_ref, d_ref, w_ref, b_ref, o_ref):
    y2 = _pick(S5_POST_TM, yc_ref, yl_ref)
    y = y2[0] + y2[1] + d_ref[...] * sx_ref[...]
    y = jax.nn.gelu(y)
    gate = jnp.dot(y.astype(BF16), w_ref[...], preferred_element_type=F32) + b_ref[...]
    o_ref[...] = (y * jax.nn.sigmoid(gate)).astype(BF16)


def _s5_post(y_pair, z, d_skip, w_glu, b_glu):
    tm = S5_POST_TM
    return pl.pallas_call(
        _s5_post_kernel,
        grid=(T_ALL // tm,),
        in_specs=_pair_specs(tm, (2, tm, MIX_W), rows_dim=1) + [
                  pl.BlockSpec((tm, MIX_W), lambda i: (i, Z_SX // MIX_W)),
                  pl.BlockSpec((1, MIX_W), lambda i: (0, 0)),
                  pl.BlockSpec((MIX_W, MIX_W), lambda i: (0, 0)),
                  pl.BlockSpec((1, MIX_W), lambda i: (0, 0))],
        out_specs=pl.BlockSpec((tm, MIX_W), lambda i: (i, 0)),
        out_shape=jax.ShapeDtypeStruct((T_ALL, MIX_W), BF16),
        compiler_params=_cparams(("arbitrary",)),
        name="s5_post",
    )(*y_pair, z, d_skip.reshape(1, MIX_W), w_glu.astype(BF16), b_glu.reshape(1, MIX_W))


def _mlstm_kernel(q_ref, k_ref, v_ref, mo_ref, gc_ref, gr_ref, bc_ref, br_ref, c0_ref, n0_ref, m0_ref,
                  ng_ref, h_ref, c_ref, n_ref, m_ref, hs_scr, *, seq, chunk0):
    nc = seq // CHUNK
    ii = lax.broadcasted_iota(jnp.int32, (CHUNK, CHUNK), 0)
    jj = lax.broadcasted_iota(jnp.int32, (CHUNK, CHUNK), 1)
    neg_inf = jnp.float32(-jnp.inf)

    def chunk_step(d, cidx, carry):
        fwd = d == 0
        mask = (jj <= ii) if fwd else (jj >= ii)
        mask_t = (ii <= jj) if fwd else (ii >= jj)
        last = CHUNK - 1 if fwd else 0
        c_st, n_st, m_st = carry
        rows = pl.ds(pl.multiple_of(cidx * CHUNK, CHUNK), CHUNK)
        q = q_ref[rows, :]
        k = k_ref[rows, :] * (MLSTM_DIM ** -0.5)
        v = v_ref[rows, :]
        gcol = gc_ref[0, rows, :] + bc_ref[0]
        grow = gr_ref[0, chunk0 + cidx] + br_ref[0]
        li_c = gcol[:, 2 * d:2 * d + 1]
        lf_c = jax.nn.log_sigmoid(gcol[:, 2 * d + 1:2 * d + 2])
        li_r = grow[2 * d:2 * d + 1, :]
        lf_r = jax.nn.log_sigmoid(grow[2 * d + 1:2 * d + 2, :])
        b_c = jnp.sum(jnp.where(mask, lf_r, 0.0), axis=1, keepdims=True)
        b_r = jnp.sum(jnp.where(mask_t, lf_c, 0.0), axis=0, keepdims=True)
        dmat = jnp.where(mask, b_c - b_r + li_r, neg_inf)
        inter = b_c + m_st
        m_j = jnp.maximum(inter, jnp.max(dmat, axis=1, keepdims=True))
        qb = q.astype(BF16)
        vb = v.astype(BF16)
        s = lax.dot_general(qb, k.astype(BF16), (((1,), (1,)), ((), ())),
                            preferred_element_type=F32) * jnp.exp(dmat - m_j)
        w_inter = jnp.exp(inter - m_j)
        num = (jnp.dot(s.astype(BF16), vb, preferred_element_type=F32)
               + w_inter * jnp.dot(qb, c_st.astype(BF16), preferred_element_type=F32))
        den = jnp.sum(s, axis=1, keepdims=True) + w_inter * jnp.sum(q * n_st, axis=1, keepdims=True)
        hs_scr[d, rows, :] = num / jnp.maximum(jnp.abs(den), jnp.exp(-m_j))
        m_end = m_j[last:last + 1, :]
        b_end = b_c[last:last + 1, :]
        w_c = jnp.exp(b_end - b_c + li_c - m_end)
        decay = jnp.exp(b_end + m_st - m_end)
        kw = k * w_c
        c_new = decay * c_st + lax.dot_general(kw.astype(BF16), vb, (((0,), (0,)), ((), ())),
                                               preferred_element_type=F32)
        n_new = decay * n_st + jnp.sum(kw, axis=0, keepdims=True)
        return c_new, n_new, m_end

    def both(ci, carry):
        return chunk_step(0, ci, carry[0]), chunk_step(1, nc - 1 - ci, carry[1])

    init = tuple((c0_ref[0, d, 0], n0_ref[0, d, 0], m0_ref[0, d, 0]) for d in range(2))
    final = lax.fori_loop(0, nc, both, init)
    for d in range(2):
        c_ref[0, d, 0], n_ref[0, d, 0], m_ref[0, d, 0] = final[d]
    h = _rms(hs_scr[0] + hs_scr[1], ng_ref[0])
    h_ref[...] = (h * jax.nn.sigmoid(mo_ref[...])).astype(BF16)


def _mlstm(z, gcol, grow, bias_c, bias_r, c0, n0, m0, norm_g, *, ctx):
    seq = SEQ if ctx else DEC_SEQ
    nb = BATCH if ctx else DEC_BATCH
    row0 = 0 if ctx else T_CTX // DEC_SEQ
    hd = MLSTM_DIM
    nh = MLSTM_HEADS
    zcol = lambda off: (lambda b, h: (row0 + b, off // hd + h))
    st5 = lambda b, h: (b, 0, h, 0, 0)
    return pl.pallas_call(
        functools.partial(_mlstm_kernel, seq=seq, chunk0=0),
        grid=(nb, nh),
        in_specs=[pl.BlockSpec((seq, hd), zcol(Z_MQ)),
                  pl.BlockSpec((seq, hd), zcol(Z_MK)),
                  pl.BlockSpec((seq, hd), zcol(Z_MV)),
                  pl.BlockSpec((seq, hd), zcol(Z_MO)),
                  pl.BlockSpec((1, seq, 4), lambda b, h: (h, row0 + b, 0)),
                  pl.BlockSpec((1, seq // CHUNK, 4, CHUNK), lambda b, h: (h, row0 + b, 0, 0)),
                  pl.BlockSpec((1, 1, 4), lambda b, h: (h, 0, 0)),
                  pl.BlockSpec((1, 4, 1), lambda b, h: (h, 0, 0)),
                  pl.BlockSpec((1, 2, 1, hd, hd), st5),
                  pl.BlockSpec((1, 2, 1, 1, hd), st5),
                  pl.BlockSpec((1, 2, 1, 1, 1), st5),
                  pl.BlockSpec((1, 1, hd), lambda b, h: (h, 0, 0))],
        out_specs=[pl.BlockSpec((seq, hd), lambda b, h: (b, h)),
                   pl.BlockSpec((1, 2, 1, hd, hd), st5),
                   pl.BlockSpec((1, 2, 1, 1, hd), st5),
                   pl.BlockSpec((1, 2, 1, 1, 1), st5)],
        out_shape=[jax.ShapeDtypeStruct((nb * seq, MIX_W), BF16),
                   jax.ShapeDtypeStruct((nb, 2, nh, hd, hd), F32),
                   jax.ShapeDtypeStruct((nb, 2, nh, 1, hd), F32),
                   jax.ShapeDtypeStruct((nb, 2, nh, 1, 1), F32)],
        scratch_shapes=[pltpu.VMEM((2, seq, hd), F32)],
        compiler_params=_cparams(("arbitrary", "arbitrary")),
        name="mlstm_ctx" if ctx else "mlstm_lat",
    )(z, z, z, z, gcol, grow, bias_c, bias_r, c0, n0, m0, norm_g.reshape(nh, 1, hd))


GMLP_TM = 512


def _gmlp_kernel(gu_ref, gv_ref, ng_ref, ws_ref, bs_ref, o_ref):
    vn = _rms(gv_ref[...], ng_ref[...]).astype(BF16)
    gw = MIX_W // GMLP_GROUPS
    for c in range(GMLP_TM // CHUNK):
        r = slice(c * CHUNK, (c + 1) * CHUNK)
        for g in range(GMLP_GROUPS):
            cs = slice(g * gw, (g + 1) * gw)
            mixed = jnp.dot(ws_ref[g], vn[r, cs], preferred_element_type=F32) + bs_ref[:, g:g + 1]
            o_ref[r, cs] = (gu_ref[r, cs] * mixed).astype(BF16)


def _gmlp(z, norm_g, w_s, b_s):
    tm = GMLP_TM
    return pl.pallas_call(
        _gmlp_kernel,
        grid=(T_ALL // tm,),
        in_specs=[pl.BlockSpec((tm, MIX_W), lambda i: (i, Z_GU // MIX_W)),
                  pl.BlockSpec((tm, MIX_W), lambda i: (i, Z_GV // MIX_W)),
                  pl.BlockSpec((1, MIX_W), lambda i: (0, 0)),
                  pl.BlockSpec((GMLP_GROUPS, CHUNK, CHUNK), lambda i: (0, 0, 0)),
                  pl.BlockSpec((CHUNK, GMLP_GROUPS), lambda i: (0, 0))],
        out_specs=pl.BlockSpec((tm, MIX_W), lambda i: (i, 0)),
        out_shape=jax.ShapeDtypeStruct((T_ALL, MIX_W), BF16),
        compiler_params=_cparams(("arbitrary",)),
        name="gmlp",
    )(z, z, norm_g.reshape(1, MIX_W), w_s.astype(BF16), b_s.T)


OUT_TM = 256


def _out_kernel(*refs, router):
    (ac_ref, al_ref, b_ref, cc_ref, cl_ref, d_ref, xc_ref, xl_ref, mod_ref, g_ref, w_ref), refs = refs[:11], refs[11:]
    if router:
        rh_ref, rl_ref, rb_ref, x1_ref, h2_ref, rt_ref = refs
    else:
        x1_ref, h2_ref = refs
    mix = jnp.dot(_pick(OUT_TM, ac_ref, al_ref), w_ref[0:MIX_W, :], preferred_element_type=F32)
    mix += jnp.dot(b_ref[...], w_ref[MIX_W:2 * MIX_W, :], preferred_element_type=F32)
    mix += jnp.dot(_pick(OUT_TM, cc_ref, cl_ref), w_ref[2 * MIX_W:3 * MIX_W, :], preferred_element_type=F32)
    mix += jnp.dot(d_ref[...], w_ref[3 * MIX_W:4 * MIX_W, :], preferred_element_type=F32)
    x1 = _pick(OUT_TM, xc_ref, xl_ref) + mod_ref[0, 2:3, :] * mix
    x1_ref[...] = x1
    h2 = _rms(x1, g_ref[...]) * (1.0 + mod_ref[0, 4:5, :]) + mod_ref[0, 3:4, :]
    hi = h2.astype(BF16)
    h2_ref[...] = h2 if router else hi
    if router:
        lo = (h2 - hi.astype(F32)).astype(BF16)
        logits = (jnp.dot(hi, rh_ref[...], preferred_element_type=F32)
                  + jnp.dot(lo, rh_ref[...], preferred_element_type=F32)
                  + jnp.dot(hi, rl_ref[...], preferred_element_type=F32)) + rb_ref[...]
        lane = lax.broadcasted_iota(jnp.int32, logits.shape, 1)
        neg_inf = jnp.float32(-jnp.inf)
        lg = jnp.where(lane < N_EXPERTS, logits, neg_inf)
        m1 = jnp.max(lg, axis=-1, keepdims=True)
        i1 = jnp.min(jnp.where(lg == m1, lane, LANE), axis=-1, keepdims=True)
        lg2 = jnp.where(lane == i1, neg_inf, lg)
        m2 = jnp.max(lg2, axis=-1, keepdims=True)
        i2 = jnp.min(jnp.where(lg2 == m2, lane, LANE), axis=-1, keepdims=True)
        e = jnp.exp(m2 - m1)
        w1 = 1.0 / (1.0 + e)
        w2 = e / (1.0 + e)
        rt = jnp.where(lane == 0, i1.astype(F32), 0.0)
        rt = jnp.where(lane == 1, i2.astype(F32), rt)
        rt = jnp.where(lane == 2, w1, rt)
        rt = jnp.where(lane == 3, w2, rt)
        rt_ref[...] = rt


def _out_proj(att_pair, s5_out, ml_pair, gm_out, x_pair, mod_l, g2, w_out, router=None):
    tm = OUT_TM
    row = lambda i: (i, 0)
    fixed = lambda i: (0, 0)
    full = pl.BlockSpec((tm, MIX_W), row)
    pair = _pair_specs(tm, (tm, MIX_W))
    in_specs = pair + [full] + pair + [full] + _pair_specs(tm, (tm, D_MODEL)) + [
        pl.BlockSpec((1, MOD_CHUNKS, D_MODEL), lambda i: (_mod_group(i, tm), 0, 0)),
        pl.BlockSpec((1, D_MODEL), fixed),
        pl.BlockSpec((D_MODEL, D_MODEL), fixed, pipeline_mode=pl.Buffered(1))]
    args = [*att_pair, s5_out, *ml_pair, gm_out, *x_pair, mod_l, g2.reshape(1, D_MODEL), w_out]
    out_specs = [pl.BlockSpec((tm, D_MODEL), row), pl.BlockSpec((tm, D_MODEL), row)]
    out_shape = [jax.ShapeDtypeStruct((T_ALL, D_MODEL), F32),
                 jax.ShapeDtypeStruct((T_ALL, D_MODEL), BF16 if router is None else F32)]
    if router is not None:
        rw, rb = router
        rw = jnp.pad(rw, ((0, 0), (0, LANE - N_EXPERTS)))
        rh = rw.astype(BF16)
        rl = (rw - rh.astype(F32)).astype(BF16)
        in_specs += [pl.BlockSpec((D_MODEL, LANE), fixed)] * 2 + [pl.BlockSpec((1, LANE), fixed)]
        args += [rh, rl, jnp.pad(rb, (0, LANE - N_EXPERTS)).reshape(1, LANE)]
        out_specs.append(pl.BlockSpec((tm, LANE), row))
        out_shape.append(jax.ShapeDtypeStruct((T_ALL, LANE), F32))
    return pl.pallas_call(
        functools.partial(_out_kernel, router=router is not None),
        grid=(T_ALL // tm,),
        in_specs=in_specs, out_specs=out_specs, out_shape=out_shape,
        compiler_params=_cparams(("arbitrary",)),
        name="out_proj_router" if router is not None else "out_proj",
    )(*args)


def _tile_cols(w):
    w = w.reshape(w.shape[:-1] + (D_FF // FFN_TF, FFN_TF))
    return jnp.swapaxes(w, -3, -2).astype(BF16)


def _ffn_kernel(x_ref, wg_ref, wu_ref, wd_ref, o_ref):
    @pl.when(pl.program_id(1) == 0)
    def _():
        o_ref[...] = jnp.zeros_like(o_ref)

    def sub(s, carry):
        rows = pl.ds(pl.multiple_of(s * FFN_DENSE_SUB, FFN_DENSE_SUB), FFN_DENSE_SUB)
        xs = x_ref[rows, :]
        g = jnp.dot(xs, wg_ref[0], preferred_element_type=F32)
        u = jnp.dot(xs, wu_ref[0], preferred_element_type=F32)
        a = (g * jax.nn.sigmoid(g) * u).astype(BF16)
        o_ref[rows, :] += jnp.dot(a, wd_ref[...], preferred_element_type=F32)
        return carry

    lax.fori_loop(0, FFN_TM // FFN_DENSE_SUB, sub, 0)


def _ffn(x, w_gate, w_up, w_down):
    return pl.pallas_call(
        _ffn_kernel,
        grid=(T_ALL // FFN_TM, D_FF // FFN_TF),
        in_specs=[pl.BlockSpec((FFN_TM, D_MODEL), lambda i, f: (i, 0)),
                  pl.BlockSpec((1, D_MODEL, FFN_TF), lambda i, f: (f, 0, 0)),
                  pl.BlockSpec((1, D_MODEL, FFN_TF), lambda i, f: (f, 0, 0)),
                  pl.BlockSpec((FFN_TF, D_MODEL), lambda i, f: (f, 0))],
        out_specs=pl.BlockSpec((FFN_TM, D_MODEL), lambda i, f: (i, 0)),
        out_shape=jax.ShapeDtypeStruct((T_ALL, D_MODEL), F32),
        compiler_params=_cparams(("arbitrary", "arbitrary")),
        name="ffn",
    )(x, w_gate, w_up, w_down)


def _moe_ffn_kernel(te_ref, ns_ref, src_ref, x_hbm, wg_ref, wu_ref, wd_ref, o_ref, xf, xb, sems):
    i = pl.program_id(0)
    f = pl.program_id(1)
    ntiles = pl.num_programs(0)

    def row_copy(src_row, slot, r):
        return pltpu.make_async_copy(x_hbm.at[pl.ds(src_row, 1)], xf.at[slot, pl.ds(r, 1)], sems.at[slot])

    def issue(tile, slot):
        def body(r, carry):
            row_copy(src_ref[tile * FFN_TM + r], slot, r).start()
            return carry

        lax.fori_loop(0, ns_ref[tile] * FFN_SUB, body, 0)

    def wait_rows(tile, slot):
        def body(s, carry):
            rows = pl.ds(pl.multiple_of(s * FFN_SUB, FFN_SUB), FFN_SUB)
            pltpu.make_async_copy(x_hbm.at[pl.ds(0, FFN_SUB)], xf.at[slot, rows], sems.at[slot]).wait()
            return carry

        lax.fori_loop(0, ns_ref[tile], body, 0)

    @pl.when(f == 0)
    def _():
        slot = i % 2

        @pl.when(i == 0)
        def _():
            issue(0, 0)

        wait_rows(i, slot)

        @pl.when(i + 1 < ntiles)
        def _():
            issue(i + 1, 1 - slot)

        def cast(s, carry):
            rows = pl.ds(pl.multiple_of(s * FFN_SUB, FFN_SUB), FFN_SUB)
            xb[rows, :] = xf[slot, rows, :].astype(BF16)
            return carry

        lax.fori_loop(0, ns_ref[i], cast, 0)
        o_ref[...] = jnp.zeros_like(o_ref)

    def sub(s, carry):
        rows = pl.ds(pl.multiple_of(s * FFN_SUB, FFN_SUB), FFN_SUB)
        xs = xb[rows, :]
        g = jnp.dot(xs, wg_ref[0, 0], preferred_element_type=F32)
        u = jnp.dot(xs, wu_ref[0, 0], preferred_element_type=F32)
        a = (g * jax.nn.sigmoid(g) * u).astype(BF16)
        o_ref[rows, :] += jnp.dot(a, wd_ref[0], preferred_element_type=F32)
        return carry

    lax.fori_loop(0, ns_ref[i], sub, 0)


def _moe_ffn(h2, tile_e, tile_ns, src, w_gate, w_up, w_down):
    nf = D_FF // FFN_TF

    def fcol(i, f, ns):
        return jnp.where(ns[i] > 0, f, nf - 1)

    return pl.pallas_call(
        _moe_ffn_kernel,
        grid_spec=pltpu.PrefetchScalarGridSpec(
            num_scalar_prefetch=3,
            grid=(MOE_TILES, nf),
            in_specs=[pl.BlockSpec(memory_space=pl.ANY),
                      pl.BlockSpec((1, 1, D_MODEL, FFN_TF), lambda i, f, te, ns, sr: (te[i], fcol(i, f, ns), 0, 0)),
                      pl.BlockSpec((1, 1, D_MODEL, FFN_TF), lambda i, f, te, ns, sr: (te[i], fcol(i, f, ns), 0, 0)),
                      pl.BlockSpec((1, FFN_TF, D_MODEL), lambda i, f, te, ns, sr: (te[i], fcol(i, f, ns), 0))],
            out_specs=pl.BlockSpec((FFN_TM, D_MODEL), lambda i, f, te, ns, sr: (i, 0)),
            scratch_shapes=[pltpu.VMEM((2, FFN_TM, D_MODEL), F32),
                            pltpu.VMEM((FFN_TM, D_MODEL), BF16),
                            pltpu.SemaphoreType.DMA((2,))]),
        out_shape=jax.ShapeDtypeStruct((MOE_TILES * FFN_TM, D_MODEL), F32),
        compiler_params=_cparams(("arbitrary", "arbitrary")),
        name="moe_ffn",
    )(tile_e, tile_ns, src, h2, w_gate, w_up, w_down)


RES_TM = 256


def _store_pair(val, oc_ref, ol_ref):
    i = pl.program_id(0)

    @pl.when(i < T_CTX // RES_TM)
    def _():
        oc_ref[...] = val

    @pl.when(i >= T_CTX // RES_TM)
    def _():
        ol_ref[...] = val


def _res_kernel(x_ref, y_ref, mod_ref, g_ref, oc_ref, ol_ref, *, final):
    x2 = x_ref[...] + mod_ref[0, 5:6, :] * y_ref[...]
    _store_pair(_rms(x2, g_ref[...]) if final else x2, oc_ref, ol_ref)


def _moe_res_kernel(slot_ref, x_ref, rt_ref, mod_ref, g_ref, ys_hbm, oc_ref, ol_ref, ybuf, sems, *, final):
    i = pl.program_id(0)
    nt = pl.num_programs(0)

    def row_copy(src_row, buf, k, r):
        return pltpu.make_async_copy(ys_hbm.at[pl.ds(src_row, 1)], ybuf.at[buf, k, pl.ds(r, 1)], sems.at[buf])

    def issue(tile, buf):
        def body(r, carry):
            for k in range(2):
                row_copy(slot_ref[k * T_ALL + tile * RES_TM + r], buf, k, r).start()
            return carry

        lax.fori_loop(0, RES_TM, body, 0)

    buf = i % 2

    @pl.when(i == 0)
    def _():
        issue(0, 0)

    for k in range(2):
        pltpu.make_async_copy(ys_hbm.at[pl.ds(0, RES_TM)], ybuf.at[buf, k], sems.at[buf]).wait()

    @pl.when(i + 1 < nt)
    def _():
        issue(i + 1, 1 - buf)

    rt = rt_ref[...]
    ffn = rt[:, 2:3] * ybuf[buf, 0] + rt[:, 3:4] * ybuf[buf, 1]
    x2 = x_ref[...] + mod_ref[0, 5:6, :] * ffn
    _store_pair(_rms(x2, g_ref[...]) if final else x2, oc_ref, ol_ref)


def _residual(x1, y, mod_l, final_g, *, route=None, slot=None, final):
    tm = RES_TM
    nt = T_ALL // tm
    out_shape = [jax.ShapeDtypeStruct((T_CTX, D_MODEL), F32), jax.ShapeDtypeStruct((T_LAT, D_MODEL), F32)]
    out_specs = _pair_specs(tm, (tm, D_MODEL))
    if route is None:
        row = lambda i: (i, 0)
        return pl.pallas_call(
            functools.partial(_res_kernel, final=final),
            grid=(nt,),
            in_specs=[pl.BlockSpec((tm, D_MODEL), row), pl.BlockSpec((tm, D_MODEL), row),
                      pl.BlockSpec((1, MOD_CHUNKS, D_MODEL), lambda i: (_mod_group(i, tm), 0, 0)),
                      pl.BlockSpec((1, D_MODEL), lambda i: (0, 0))],
            out_specs=out_specs,
            out_shape=out_shape,
            compiler_params=_cparams(("arbitrary",)),
            name="ffn_residual",
        )(x1, y, mod_l, final_g.reshape(1, D_MODEL))
    row = lambda i, sl: (i, 0)
    return pl.pallas_call(
        functools.partial(_moe_res_kernel, final=final),
        grid_spec=pltpu.PrefetchScalarGridSpec(
            num_scalar_prefetch=1,
            grid=(nt,),
            in_specs=[pl.BlockSpec((tm, D_MODEL), row), pl.BlockSpec((tm, LANE), row),
                      pl.BlockSpec((1, MOD_CHUNKS, D_MODEL), lambda i, sl: (_mod_group(i, tm), 0, 0)),
                      pl.BlockSpec((1, D_MODEL), lambda i, sl: (0, 0)),
                      pl.BlockSpec(memory_space=pl.ANY)],
            out_specs=out_specs,
            scratch_shapes=[pltpu.VMEM((2, 2, tm, D_MODEL), F32), pltpu.SemaphoreType.DMA((2,))]),
        out_shape=out_shape,
        compiler_params=_cparams(("arbitrary",)),
        name="moe_residual",
    )(slot, x1, route, mod_l, final_g.reshape(1, D_MODEL), y)


def _route_plan(route):
    e_flat = jnp.concatenate([route[:, 0], route[:, 1]]).astype(jnp.int32)
    onehot = (e_flat[:, None] == jnp.arange(N_EXPERTS, dtype=jnp.int32)[None, :]).astype(jnp.int32)
    ranks = jnp.cumsum(onehot, axis=0) - onehot
    rank = jnp.sum(ranks * onehot, axis=1)
    counts = jnp.sum(onehot, axis=0)
    tiles = (counts + FFN_TM - 1) // FFN_TM
    tile_start = jnp.cumsum(tiles) - tiles
    slot = tile_start[e_flat] * FFN_TM + rank
    n_slots = MOE_TILES * FFN_TM
    tok = jnp.concatenate([jnp.arange(T_ALL, dtype=jnp.int32)] * 2)
    src = jnp.zeros((n_slots,), jnp.int32).at[slot].set(tok)
    tile_ids = jnp.arange(MOE_TILES, dtype=jnp.int32)
    ends = jnp.cumsum(tiles)
    tile_e = jnp.sum((tile_ids[:, None] >= ends[None, :]).astype(jnp.int32), axis=1)
    used = tile_e < N_EXPERTS
    last_e = jnp.max(jnp.where(counts > 0, jnp.arange(N_EXPERTS, dtype=jnp.int32), 0))
    tile_e = jnp.where(used, tile_e, last_e)
    rows_in_tile = jnp.clip(counts[tile_e] - (tile_ids - tile_start[tile_e]) * FFN_TM, 0, FFN_TM)
    tile_ns = jnp.where(used, (rows_in_tile + FFN_SUB - 1) // FFN_SUB, 0).astype(jnp.int32)
    return src, slot, tile_e.astype(jnp.int32), tile_ns


def _rope_tables():
    length = DEC_SEQ
    r = jnp.repeat(jnp.arange(length // GRID_W, dtype=F32), GRID_W)
    col = (jnp.arange(length) % GRID_W).astype(F32)
    half = HEAD_DIM // 2
    inv = ROPE_THETA ** (-jnp.arange(0, half, 2, dtype=F32) / half)
    ar, ac = r[:, None] * inv, col[:, None] * inv
    cos = jnp.concatenate([jnp.cos(ar), jnp.cos(ar), jnp.cos(ac), jnp.cos(ac)], axis=-1)
    sin = jnp.concatenate([-jnp.sin(ar), jnp.sin(ar), -jnp.sin(ac), jnp.sin(ac)], axis=-1)
    return cos, sin


def kernel(x_prompt, x_sample, c, c_ctx, cache_attn_k, cache_attn_v, state_s5_re, state_s5_im, state_mlstm_c, state_mlstm_n, state_mlstm_m, norm1_g, norm2_g, w_mod, b_mod, w_in, w_out, q_norm_g, k_norm_g, s5_a_re, s5_a_im, s5_log_dt, s5_b_re, s5_b_im, s5_c_re, s5_c_im, s5_d, s5_w_glu, s5_b_glu, mlstm_i_bias, mlstm_f_bias, mlstm_norm_g, gmlp_norm_g, gmlp_w_s, gmlp_b_s, ffn_w_gate, ffn_w_up, ffn_w_down, moe_router, moe_router_bias, moe_w_gate, moe_w_up, moe_w_down, final_norm_g):
    x = (x_prompt.reshape(T_CTX, D_MODEL), x_sample.reshape(T_LAT, D_MODEL))
    cond = jnp.concatenate([c_ctx[None, :], c], axis=0)
    mod = _modulation(cond, w_mod, b_mod)
    cos, sin = _rope_tables()
    nh = MLSTM_HEADS
    zeros_s5 = jnp.zeros((BATCH, 2, S5_SG, 1, S5_SGW), F32)
    zeros_c = jnp.zeros((BATCH, 2, nh, MLSTM_DIM, MLSTM_DIM), F32)
    zeros_n = jnp.zeros((BATCH, 2, nh, 1, MLSTM_DIM), F32)
    zeros_m = jnp.zeros((BATCH, 2, nh, 1, 1), F32)

    ctx_states = []
    for l in range(DEPTH):
        use_moe = l % 2 == 1
        j = l // 2
        wl = w_in[l]
        w_p = jnp.concatenate([wl[:, :3584], wl[:, 3600:], wl[:, 3584:3600],
                               jnp.zeros((D_MODEL, Z_COLS - 4624), F32)], axis=1).astype(BF16)
        z = _in_proj(x, mod[l], norm1_g[l], w_p)

        att_c, k_new, v_new = _attention(z, None, q_norm_g[l], k_norm_g[l], ctx=True)
        (att_l,) = _attention(z, None, q_norm_g[l], k_norm_g[l], ctx=False,
                              kpast=cache_attn_k[:, l], vpast=cache_attn_v[:, l], cos=cos, sin=sin)

        prep = _s5_prep(s5_a_re[l], s5_a_im[l], s5_log_dt[l], s5_b_re[l], s5_b_im[l], s5_c_re[l], s5_c_im[l])
        y_c, hf_re, hf_im = _s5_scan(z, prep, zeros_s5, zeros_s5, ctx=True)
        st = lambda s: s[:, l].reshape(DEC_BATCH, 2, S5_SG, 1, S5_SGW)
        y_l, _, _ = _s5_scan(z, prep, st(state_s5_re), st(state_s5_im), ctx=False)
        s5_out = _s5_post((y_c, y_l), z, s5_d[l], s5_w_glu[l], s5_b_glu[l])

        mg = z[:, Z_MG:Z_MG + 16].reshape(T_ALL, 2, 2, nh)
        gcol = mg.transpose(3, 0, 1, 2).reshape(nh, T_ALL, 4)
        grow = mg.reshape(T_ALL // CHUNK, CHUNK, 4, nh).transpose(3, 0, 2, 1)
        bias = jnp.stack([mlstm_i_bias[l], mlstm_f_bias[l]], axis=1)
        bias_c = bias.transpose(2, 0, 1).reshape(nh, 1, 4)
        bias_r = bias_c.reshape(nh, 4, 1)
        ml_c_out, c_new, n_new, m_new = _mlstm(z, gcol, grow, bias_c, bias_r, zeros_c, zeros_n, zeros_m,
                                               mlstm_norm_g[l], ctx=True)
        ml_l_out, _, _, _ = _mlstm(z, gcol, grow, bias_c, bias_r, state_mlstm_c[:, l],
                                   state_mlstm_n[:, l].reshape(DEC_BATCH, 2, nh, 1, MLSTM_DIM),
                                   state_mlstm_m[:, l].reshape(DEC_BATCH, 2, nh, 1, 1),
                                   mlstm_norm_g[l], ctx=False)

        gm_out = _gmlp(z, gmlp_norm_g[l], gmlp_w_s[l], gmlp_b_s[l])

        ctx_states.append((k_new, v_new,
                           hf_re.reshape(BATCH, 2, S5_GROUPS, S5_STATE), hf_im.reshape(BATCH, 2, S5_GROUPS, S5_STATE),
                           c_new, n_new.reshape(BATCH, 2, nh, MLSTM_DIM), m_new.reshape(BATCH, 2, nh)))

        w_o = w_out[l].astype(BF16)
        final = l == DEPTH - 1
        mixed = ((att_c, att_l), s5_out, (ml_c_out, ml_l_out), gm_out)
        if use_moe:
            x1, h2, route = _out_proj(*mixed, x, mod[l], norm2_g[l], w_o,
                                      router=(moe_router[j], moe_router_bias[j]))
            src, slot, tile_e, tile_ns = _route_plan(route)
            ys = _moe_ffn(h2, tile_e, tile_ns, src, _tile_cols(moe_w_gate[j]), _tile_cols(moe_w_up[j]),
                          moe_w_down[j].astype(BF16))
            x = _residual(x1, ys, mod[l], final_norm_g, route=route, slot=slot, final=final)
        else:
            x1, h2 = _out_proj(*mixed, x, mod[l], norm2_g[l], w_o)
            y = _ffn(h2, _tile_cols(ffn_w_gate[j]), _tile_cols(ffn_w_up[j]), ffn_w_down[j].astype(BF16))
            x = _residual(x1, y, mod[l], final_norm_g, final=final)

    y_prompt = x[0].reshape(BATCH, SEQ, D_MODEL)
    y_sample = x[1].reshape(DEC_BATCH, DEC_SEQ, D_MODEL)
    stack = lambda i: jnp.stack([s[i] for s in ctx_states], axis=1)
    return (y_prompt, y_sample, stack(0), stack(1), stack(2), stack(3), stack(4), stack(5), stack(6))
```

```python
import functools
import math

import numpy as np
import jax
import jax.numpy as jnp
from jax import lax
from jax.experimental import pallas as pl
from jax.experimental.pallas import tpu as pltpu

F32 = jnp.float32
BF16 = jnp.bfloat16

D_MODEL = 2048
BATCH = 16
SEQ = 256
DEPTH = 2
DEC_BATCH = 2
DEC_SEQ = 2048
PAST_LEN = 256
GRID_W = 64
MIX_W = 512
ATT_HEADS = 4
ATT_KV_HEADS = 2
HEAD_DIM = 128
ROPE_THETA = 10000.0
S5_CH = 16
S5_GROUPS = 32
S5_STATE = 64
MLSTM_HEADS = 4
MLSTM_DIM = 128
CHUNK = 128
GMLP_GROUPS = 4
D_FF = 7168
N_EXPERTS = 8
MOD_CHUNKS = 6
EPS = 1e-6

T_CTX = BATCH * SEQ
T_LAT = DEC_BATCH * DEC_SEQ
T_ALL = T_CTX + T_LAT
N_GROUPS_MOD = 1 + DEC_BATCH

Z_AQ, Z_AK, Z_AV, Z_SX = 0, 512, 768, 1024
Z_MQ, Z_MK, Z_MV, Z_MO = 1536, 2048, 2560, 3072
Z_GU, Z_GV, Z_MG = 3584, 4096, 4608
Z_COLS = 4736
LANE = 128
SUBLANE = 8

VMEM_LIMIT = 56 * 1024 * 1024

S5_SG = 4
S5_SGW = 8 * S5_STATE
S5_LAGS = 8

FFN_TM = 1024
FFN_SUB = 256
FFN_DENSE_SUB = 512
FFN_TF = 512
MOE_TILES = 2 * T_ALL // FFN_TM + N_EXPERTS
ISSUE_UNROLL = 8


def _cparams(sem=None):
    return pltpu.CompilerParams(dimension_semantics=sem, vmem_limit_bytes=VMEM_LIMIT)


def _mod_group(i, tm):
    return jnp.maximum(i * tm // DEC_SEQ - (T_CTX // DEC_SEQ - 1), 0)


def _rms(x, g):
    return x * lax.rsqrt(jnp.mean(x * x, axis=-1, keepdims=True) + EPS) * g


MOD_TN = 512


def _mod_kernel(cb_ref, w_ref, b_ref, o_ref):
    w = w_ref[0]
    rows = []
    for r in range(N_GROUPS_MOD):
        c = cb_ref[r]
        s = c * jax.nn.sigmoid(c)
        parts = [jnp.sum(w[:, j * LANE:(j + 1) * LANE] * s, axis=0, keepdims=True)
                 for j in range(MOD_TN // LANE)]
        rows.append(jnp.concatenate(parts, axis=1))
    rows.append(jnp.zeros((SUBLANE - N_GROUPS_MOD, MOD_TN), F32))
    o_ref[0] = jnp.concatenate(rows, axis=0) + b_ref[0]


def _modulation(cond, w_mod, b_mod):
    cb = jnp.broadcast_to(cond[:, :, None], (N_GROUPS_MOD, D_MODEL, LANE))
    n = MOD_CHUNKS * D_MODEL
    out = pl.pallas_call(
        _mod_kernel,
        grid=(DEPTH, n // MOD_TN),
        in_specs=[pl.BlockSpec((N_GROUPS_MOD, D_MODEL, LANE), lambda l, j: (0, 0, 0)),
                  pl.BlockSpec((1, D_MODEL, MOD_TN), lambda l, j: (l, 0, j)),
                  pl.BlockSpec((1, 1, MOD_TN), lambda l, j: (l, 0, j))],
        out_specs=pl.BlockSpec((1, SUBLANE, MOD_TN), lambda l, j: (l, 0, j)),
        out_shape=jax.ShapeDtypeStruct((DEPTH, SUBLANE, n), F32),
        compiler_params=_cparams(("arbitrary", "arbitrary")),
        name="adaln_mod",
    )(cb, w_mod, b_mod.reshape(DEPTH, 1, n))
    return out[:, :N_GROUPS_MOD].reshape(DEPTH, N_GROUPS_MOD, MOD_CHUNKS, D_MODEL)


IN_TM = 256
IN_CHUNK = 512


def _pair_specs(tm, block, rows_dim=0):
    nc = T_CTX // tm

    def index(row):
        return tuple(row if d == rows_dim else 0 for d in range(len(block)))

    return [pl.BlockSpec(block, lambda i, *_: index(jnp.minimum(i, nc - 1))),
            pl.BlockSpec(block, lambda i, *_: index(jnp.maximum(i - nc, 0)))]


def _pick(tm, ctx_ref, lat_ref):
    return jnp.where(pl.program_id(0) < T_CTX // tm, ctx_ref[...], lat_ref[...])


def _in_kernel(xc_ref, xl_ref, mod_ref, g_ref, w_ref, z_ref):
    x = _pick(IN_TM, xc_ref, xl_ref)
    shift = mod_ref[0, 0:1, :]
    scale = mod_ref[0, 1:2, :]
    h = (_rms(x, g_ref[...]) * (1.0 + scale) + shift).astype(BF16)
    for c0 in range(0, Z_COLS, IN_CHUNK):
        cw = min(IN_CHUNK, Z_COLS - c0)
        z_ref[:, c0:c0 + cw] = jnp.dot(h, w_ref[:, c0:c0 + cw], preferred_element_type=F32)


def _in_proj(x_pair, mod_l, g, w_p):
    return pl.pallas_call(
        _in_kernel,
        grid=(T_ALL // IN_TM,),
        in_specs=_pair_specs(IN_TM, (IN_TM, D_MODEL)) + [
                  pl.BlockSpec((1, MOD_CHUNKS, D_MODEL), lambda i: (_mod_group(i, IN_TM), 0, 0)),
                  pl.BlockSpec((1, D_MODEL), lambda i: (0, 0)),
                  pl.BlockSpec((D_MODEL, Z_COLS), lambda i: (0, 0), pipeline_mode=pl.Buffered(1))],
        out_specs=pl.BlockSpec((IN_TM, Z_COLS), lambda i: (i, 0)),
        out_shape=jax.ShapeDtypeStruct((T_ALL, Z_COLS), F32),
        compiler_params=_cparams(("arbitrary",)),
        name="in_proj",
    )(*x_pair, mod_l, g.reshape(1, D_MODEL), w_p)


def _rope(t, c, s):
    lane = lax.broadcasted_iota(jnp.int32, t.shape, 1)
    first = (lane % (HEAD_DIM // 2)) < (HEAD_DIM // 4)
    swapped = jnp.where(first, pltpu.roll(t, HEAD_DIM - HEAD_DIM // 4, 1), pltpu.roll(t, HEAD_DIM // 4, 1))
    return t * c + swapped * s


def _attn_kernel(*refs, seq, past, rope):
    if rope:
        (aq_ref, ak_ref, av_ref, kp_ref, vp_ref, cos_ref, sin_ref, qg_ref, kg_ref,
         att_ref, kb_scr, vb_scr) = refs
    else:
        aq_ref, ak_ref, av_ref, qg_ref, kg_ref, att_ref, knew_ref, vnew_ref, kb_scr, vb_scr = refs
    kn = _rms(ak_ref[...], kg_ref[...])
    v = av_ref[...]
    if rope:
        kn = _rope(kn, cos_ref[...], sin_ref[...])
        kb_scr[seq:seq + past, :] = kp_ref[0, 0].astype(BF16)
        vb_scr[seq:seq + past, :] = vp_ref[0, 0].astype(BF16)
    else:
        knew_ref[0, 0] = kn
        vnew_ref[0, 0] = v
    kb_scr[0:seq, :] = kn.astype(BF16)
    vb_scr[0:seq, :] = v.astype(BF16)
    grp = ATT_HEADS // ATT_KV_HEADS

    def q_block(qb, carry):
        rows = pl.ds(pl.multiple_of(qb * CHUNK, CHUNK), CHUNK)
        qs = []
        for g in range(grp):
            q = _rms(aq_ref[rows, g * HEAD_DIM:(g + 1) * HEAD_DIM], qg_ref[...])
            if rope:
                q = _rope(q, cos_ref[rows, :], sin_ref[rows, :])
            qs.append(q)
        q2 = jnp.concatenate(qs, axis=0).astype(BF16)
        s = lax.dot_general(q2, kb_scr[...], (((1,), (1,)), ((), ())),
                            preferred_element_type=F32) * (HEAD_DIM ** -0.5)
        m = jnp.max(s, axis=-1, keepdims=True)
        p = jnp.exp(s - m)
        den = jnp.sum(p, axis=-1, keepdims=True)
        o = jnp.dot(p.astype(BF16), vb_scr[...], preferred_element_type=F32) / den
        for g in range(grp):
            att_ref[rows, g * HEAD_DIM:(g + 1) * HEAD_DIM] = o[g * CHUNK:(g + 1) * CHUNK].astype(BF16)
        return carry

    lax.fori_loop(0, seq // CHUNK, q_block, 0)


def _attention(z, att_out_shape, qg, kg, *, ctx, kpast=None, vpast=None, cos=None, sin=None):
    seq = SEQ if ctx else DEC_SEQ
    nb = BATCH if ctx else DEC_BATCH
    row0 = 0 if ctx else T_CTX // DEC_SEQ
    past = 0 if ctx else PAST_LEN
    qw = HEAD_DIM * (ATT_HEADS // ATT_KV_HEADS)
    in_specs = [pl.BlockSpec((seq, qw), lambda b, h: (row0 + b, Z_AQ // qw + h)),
                pl.BlockSpec((seq, HEAD_DIM), lambda b, h: (row0 + b, Z_AK // HEAD_DIM + h)),
                pl.BlockSpec((seq, HEAD_DIM), lambda b, h: (row0 + b, Z_AV // HEAD_DIM + h))]
    args = [z, z, z]
    if not ctx:
        in_specs += [pl.BlockSpec((1, 1, past, HEAD_DIM), lambda b, h: (b, h, 0, 0)),
                     pl.BlockSpec((1, 1, past, HEAD_DIM), lambda b, h: (b, h, 0, 0)),
                     pl.BlockSpec((seq, HEAD_DIM), lambda b, h: (0, 0)),
                     pl.BlockSpec((seq, HEAD_DIM), lambda b, h: (0, 0))]
        args += [kpast, vpast, cos, sin]
    in_specs += [pl.BlockSpec((1, HEAD_DIM), lambda b, h: (0, 0)),
                 pl.BlockSpec((1, HEAD_DIM), lambda b, h: (0, 0))]
    args += [qg.reshape(1, HEAD_DIM), kg.reshape(1, HEAD_DIM)]
    out_specs = [pl.BlockSpec((seq, qw), lambda b, h: (b, h))]
    out_shape = [jax.ShapeDtypeStruct((nb * seq, MIX_W), BF16)]
    if ctx:
        out_specs += [pl.BlockSpec((1, 1, seq, HEAD_DIM), lambda b, h: (b, h, 0, 0))] * 2
        out_shape += [jax.ShapeDtypeStruct((nb, ATT_KV_HEADS, seq, HEAD_DIM), F32)] * 2
    return pl.pallas_call(
        functools.partial(_attn_kernel, seq=seq, past=past, rope=not ctx),
        grid=(nb, ATT_KV_HEADS),
        in_specs=in_specs, out_specs=out_specs, out_shape=out_shape,
        scratch_shapes=[pltpu.VMEM((seq + past, HEAD_DIM), BF16),
                        pltpu.VMEM((seq + past, HEAD_DIM), BF16)],
        compiler_params=_cparams(("arbitrary", "arbitrary")),
        name="attn_ctx" if ctx else "attn_lat",
    )(*args)


def _s5_prep_kernel(are_ref, aim_ref, ldt_ref, bre_ref, bim_ref, pre_ref, pim_ref, wre_ref, wim_ref):
    a_re = are_ref[...]
    a_im = aim_ref[...]
    dt = jnp.exp(ldt_ref[...])
    pows = []
    for tau in range(S5_LAGS + 1):
        mag = jnp.exp((tau * dt) * a_re)
        ang = (tau * dt) * a_im
        pr, pi = mag * jnp.cos(ang), mag * jnp.sin(ang)
        pre_ref[tau] = pr
        pim_ref[tau] = pi
        pows.append((pr, pi))
    nr, ni = pows[1][0] - 1.0, pows[1][1]
    den = a_re * a_re + a_im * a_im
    cr = (nr * a_re + ni * a_im) / den
    ci = (ni * a_re - nr * a_im) / den
    for d in range(2):
        b_r, b_i = bre_ref[d], bim_ref[d]
        bb_r = cr[d:d + 1] * b_r - ci[d:d + 1] * b_i
        bb_i = cr[d:d + 1] * b_i + ci[d:d + 1] * b_r
        for tau in range(S5_LAGS):
            pr, pi = pows[tau][0][d:d + 1], pows[tau][1][d:d + 1]
            wre_ref[d, tau] = pr * bb_r - pi * bb_i
            wim_ref[d, tau] = pr * bb_i + pi * bb_r


def _s5_prep(a_re, a_im, log_dt, b_re, b_im, c_re, c_im):
    gp = S5_GROUPS * S5_STATE
    ldt = jnp.broadcast_to(log_dt[:, :, None], (2, S5_GROUPS, S5_STATE)).reshape(2, gp)
    bt = lambda b: b.transpose(0, 3, 1, 2).reshape(2, S5_CH, gp)
    pre, pim, wre, wim = pl.pallas_call(
        _s5_prep_kernel,
        out_shape=[jax.ShapeDtypeStruct((S5_LAGS + 1, 2, gp), F32)] * 2
        + [jax.ShapeDtypeStruct((2, S5_LAGS, S5_CH, gp), F32)] * 2,
        name="s5_prep",
    )(a_re.reshape(2, gp), a_im.reshape(2, gp), ldt, bt(b_re), bt(b_im))
    eye = jnp.eye(8, dtype=F32)

    hw = S5_SGW // 2
    half_mask = (jnp.arange(hw)[None, :] // S5_STATE == jnp.arange(4)[:, None]).astype(F32)

    def w_layout(w):
        w = w.reshape(2, S5_LAGS // 2, 2, S5_CH, S5_SG, 2, hw).transpose(0, 4, 5, 1, 2, 3, 6)
        w = w[:, :, :, :, :, None, :, :] * half_mask[:, None, :]
        return w.reshape(2, S5_SG, 2, S5_LAGS * LANE // 2, hw)

    w_in = jnp.concatenate([w_layout(wre), w_layout(wim)], axis=-1).astype(BF16)

    def c_layout(c):
        c = c.reshape(2, S5_SG, 8, S5_CH, S5_STATE).transpose(0, 1, 2, 4, 3)
        c = c[:, :, :, :, None, :] * eye[None, None, :, None, :, None]
        return c.reshape(2, S5_SG, S5_SGW, LANE)

    w_out = jnp.concatenate([c_layout(c_re), -c_layout(c_im)], axis=2).astype(BF16)

    def p_layout(p):
        p = p[1:].reshape(S5_LAGS, 2, S5_SG, S5_SGW).transpose(1, 2, 0, 3)
        return jnp.stack([p[0], p[1, :, ::-1]], axis=0)

    def a8_layout(p):
        p = p[S5_LAGS].reshape(2, S5_SG, 1, S5_SGW)
        return jnp.broadcast_to(p, (2, S5_SG, SUBLANE, S5_SGW))

    return w_in, w_out, a8_layout(pre), a8_layout(pim), p_layout(pre), p_layout(pim)


def _s5_kernel(u_ref, w_ref, c_ref, a8r_ref, a8i_ref, pwr_ref, pwi_ref, h0r_ref, h0i_ref,
               y_ref, hfr_ref, hfi_ref, upad, wbr, wbi, *, seq, tc):
    d = pl.program_id(0)
    zeros = jnp.zeros((SUBLANE, LANE), F32)
    upad[0:SUBLANE, :] = zeros
    upad[SUBLANE:seq + SUBLANE, :] = u_ref[...]
    upad[seq + SUBLANE:seq + 2 * SUBLANE, :] = zeros
    nch = seq // tc
    nt = tc // SUBLANE
    a8r = a8r_ref[0, 0]
    a8i = a8i_ref[0, 0]

    def run(fwd):
        h0r = h0r_ref[0, 0, 0]
        h0i = h0i_ref[0, 0, 0]
        pwr = pwr_ref[0, 0]
        pwi = pwi_ref[0, 0]
        hr = jnp.zeros((SUBLANE, S5_SGW), F32)
        hi = jnp.zeros((SUBLANE, S5_SGW), F32)
        for ci in range(nch):
            c = ci if fwd else nch - 1 - ci
            if fwd:
                win = upad[c * tc:c * tc + tc + SUBLANE, :]
                lags = [pltpu.roll(win, tau, 0)[SUBLANE:SUBLANE + tc] if tau else win[SUBLANE:SUBLANE + tc]
                        for tau in range(S5_LAGS)]
            else:
                win = upad[c * tc + SUBLANE:c * tc + tc + 2 * SUBLANE, :]
                lags = [pltpu.roll(win, tc + SUBLANE - tau, 0)[0:tc] if tau else win[0:tc]
                        for tau in range(S5_LAGS)]
            low = lax.broadcasted_iota(jnp.int32, (tc, LANE), 1) < LANE // 2
            slabs = ([], [])
            for k in range(S5_LAGS // 2):
                a, b = lags[2 * k], lags[2 * k + 1]
                slabs[0].append(jnp.where(low, a, pltpu.roll(b, LANE // 2, 1)).astype(BF16))
                slabs[1].append(jnp.where(low, pltpu.roll(a, LANE // 2, 1), b).astype(BF16))
            hw = S5_SGW // 2
            for half in range(2):
                w = jnp.dot(jnp.concatenate(slabs[half], axis=1), w_ref[0, 0, half],
                            preferred_element_type=F32)
                wbr[:, half * hw:(half + 1) * hw] = w[:, :hw]
                wbi[:, half * hw:(half + 1) * hw] = w[:, hw:]
            if ci == 0:
                r0 = 0 if fwd else tc - SUBLANE
                wbr[r0:r0 + SUBLANE, :] = wbr[r0:r0 + SUBLANE, :] + (pwr * h0r - pwi * h0i)
                wbi[r0:r0 + SUBLANE, :] = wbi[r0:r0 + SUBLANE, :] + (pwr * h0i + pwi * h0r)

            def step(i, carry):
                cr, ci_ = carry
                t = i if fwd else nt - 1 - i
                rows = pl.ds(pl.multiple_of(t * SUBLANE, SUBLANE), SUBLANE)
                nr = a8r * cr - a8i * ci_ + wbr[rows, :]
                ni = a8r * ci_ + a8i * cr + wbi[rows, :]
                wbr[rows, :] = nr
                wbi[rows, :] = ni
                return nr, ni

            hr, hi = lax.fori_loop(0, nt, step, (hr, hi))
            hcat = jnp.concatenate([wbr[...].astype(BF16), wbi[...].astype(BF16)], axis=1)
            y_ref[0, c * tc:(c + 1) * tc, :] = jnp.dot(hcat, c_ref[0, 0], preferred_element_type=F32)
        last = SUBLANE - 1 if fwd else 0
        hfr_ref[0, 0, 0] = hr[last:last + 1]
        hfi_ref[0, 0, 0] = hi[last:last + 1]

    @pl.when(d == 0)
    def _():
        run(True)

    @pl.when(d == 1)
    def _():
        run(False)


def _s5_scan(z, prep, h0r, h0i, *, ctx):
    w_in, w_out, a8r, a8i, pwr, pwi = prep
    seq = SEQ if ctx else DEC_SEQ
    nb = BATCH if ctx else DEC_BATCH
    row0 = 0 if ctx else T_CTX // DEC_SEQ
    tc = min(seq, 512)
    par = lambda d, s, b: (d, s, 0, 0)
    st = lambda d, s, b: (b, d, s, 0, 0)
    return pl.pallas_call(
        functools.partial(_s5_kernel, seq=seq, tc=tc),
        grid=(2, S5_SG, nb),
        in_specs=[pl.BlockSpec((seq, LANE), lambda d, s, b: (row0 + b, Z_SX // LANE + s)),
                  pl.BlockSpec((1, 1, 2, S5_LAGS * LANE // 2, S5_SGW), lambda d, s, b: (d, s, 0, 0, 0)),
                  pl.BlockSpec((1, 1, 2 * S5_SGW, LANE), par),
                  pl.BlockSpec((1, 1, SUBLANE, S5_SGW), par),
                  pl.BlockSpec((1, 1, SUBLANE, S5_SGW), par),
                  pl.BlockSpec((1, 1, SUBLANE, S5_SGW), par),
                  pl.BlockSpec((1, 1, SUBLANE, S5_SGW), par),
                  pl.BlockSpec((1, 1, 1, 1, S5_SGW), st),
                  pl.BlockSpec((1, 1, 1, 1, S5_SGW), st)],
        out_specs=[pl.BlockSpec((1, seq, LANE), lambda d, s, b: (d, b, s)),
                   pl.BlockSpec((1, 1, 1, 1, S5_SGW), st),
                   pl.BlockSpec((1, 1, 1, 1, S5_SGW), st)],
        out_shape=[jax.ShapeDtypeStruct((2, nb * seq, MIX_W), F32),
                   jax.ShapeDtypeStruct((nb, 2, S5_SG, 1, S5_SGW), F32),
                   jax.ShapeDtypeStruct((nb, 2, S5_SG, 1, S5_SGW), F32)],
        scratch_shapes=[pltpu.VMEM((seq + 2 * SUBLANE, LANE), F32),
                        pltpu.VMEM((tc, S5_SGW), F32),
                        pltpu.VMEM((tc, S5_SGW), F32)],
        compiler_params=_cparams(("arbitrary", "arbitrary", "arbitrary")),
        name="s5_ctx" if ctx else "s5_lat",
    )(z, w_in, w_out, a8r, a8i, pwr, pwi, h0r, h0i)


S5_POST_TM = 512


def _s5_post_kernel(yc_ref, yl_ref, sx_ref, d_ref, w_ref, b_ref, o_ref):
    y2 = _pick(S5_POST_TM, yc_ref, yl_ref)
    y = y2[0] + y2[1] + d_ref[...] * sx_ref[...]
    y = jax.nn.gelu(y)
    gate = jnp.dot(y.astype(BF16), w_ref[...], preferred_element_type=F32) + b_ref[...]
    o_ref[...] = (y * jax.nn.sigmoid(gate)).astype(BF16)


def _s5_post(y_pair, z, d_skip, w_glu, b_glu):
    tm = S5_POST_TM
    return pl.pallas_call(
        _s5_post_kernel,
        grid=(T_ALL // tm,),
        in_specs=_pair_specs(tm, (2, tm, MIX_W), rows_dim=1) + [
                  pl.BlockSpec((tm, MIX_W), lambda i: (i, Z_SX // MIX_W)),
                  pl.BlockSpec((1, MIX_W), lambda i: (0, 0)),
                  pl.BlockSpec((MIX_W, MIX_W), lambda i: (0, 0)),
                  pl.BlockSpec((1, MIX_W), lambda i: (0, 0))],
        out_specs=pl.BlockSpec((tm, MIX_W), lambda i: (i, 0)),
        out_shape=jax.ShapeDtypeStruct((T_ALL, MIX_W), BF16),
        compiler_params=_cparams(("arbitrary",)),
        name="s5_post",
    )(*y_pair, z, d_skip.reshape(1, MIX_W), w_glu.astype(BF16), b_glu.reshape(1, MIX_W))


def _mlstm_kernel(q_ref, k_ref, v_ref, mo_ref, gc_ref, gr_ref, bc_ref, br_ref, c0_ref, n0_ref, m0_ref,
                  ng_ref, h_ref, c_ref, n_ref, m_ref, hs_scr, *, seq, chunk0):
    nc = seq // CHUNK
    ii = lax.broadcasted_iota(jnp.int32, (CHUNK, CHUNK), 0)
    jj = lax.broadcasted_iota(jnp.int32, (CHUNK, CHUNK), 1)
    neg_inf = jnp.float32(-jnp.inf)

    def chunk_step(d, cidx, carry):
        fwd = d == 0
        mask = (jj <= ii) if fwd else (jj >= ii)
        mask_t = (ii <= jj) if fwd else (ii >= jj)
        last = CHUNK - 1 if fwd else 0
        c_st, n_st, m_st = carry
        rows = pl.ds(pl.multiple_of(cidx * CHUNK, CHUNK), CHUNK)
        q = q_ref[rows, :]
        k = k_ref[rows, :] * (MLSTM_DIM ** -0.5)
        v = v_ref[rows, :]
        gcol = gc_ref[0, rows, :] + bc_ref[0]
        grow = gr_ref[0, chunk0 + cidx] + br_ref[0]
        li_c = gcol[:, 2 * d:2 * d + 1]
        lf_c = jax.nn.log_sigmoid(gcol[:, 2 * d + 1:2 * d + 2])
        li_r = grow[2 * d:2 * d + 1, :]
        lf_r = jax.nn.log_sigmoid(grow[2 * d + 1:2 * d + 2, :])
        b_c = jnp.sum(jnp.where(mask, lf_r, 0.0), axis=1, keepdims=True)
        b_r = jnp.sum(jnp.where(mask_t, lf_c, 0.0), axis=0, keepdims=True)
        dmat = jnp.where(mask, b_c - b_r + li_r, neg_inf)
        inter = b_c + m_st
        m_j = jnp.maximum(inter, jnp.max(dmat, axis=1, keepdims=True))
        qb = q.astype(BF16)
        vb = v.astype(BF16)
        s = lax.dot_general(qb, k.astype(BF16), (((1,), (1,)), ((), ())),
                            preferred_element_type=F32) * jnp.exp(dmat - m_j)
        w_inter = jnp.exp(inter - m_j)
        num = (jnp.dot(s.astype(BF16), vb, preferred_element_type=F32)
               + w_inter * jnp.dot(qb, c_st.astype(BF16), preferred_element_type=F32))
        den = jnp.sum(s, axis=1, keepdims=True) + w_inter * jnp.sum(q * n_st, axis=1, keepdims=True)
        hs_scr[d, rows, :] = num / jnp.maximum(jnp.abs(den), jnp.exp(-m_j))
        m_end = m_j[last:last + 1, :]
        b_end = b_c[last:last + 1, :]
        w_c = jnp.exp(b_end - b_c + li_c - m_end)
        decay = jnp.exp(b_end + m_st - m_end)
        kw = k * w_c
        c_new = decay * c_st + lax.dot_general(kw.astype(BF16), vb, (((0,), (0,)), ((), ())),
                                               preferred_element_type=F32)
        n_new = decay * n_st + jnp.sum(kw, axis=0, keepdims=True)
        return c_new, n_new, m_end

    def both(ci, carry):
        return chunk_step(0, ci, carry[0]), chunk_step(1, nc - 1 - ci, carry[1])

    init = tuple((c0_ref[0, d, 0], n0_ref[0, d, 0], m0_ref[0, d, 0]) for d in range(2))
    final = lax.fori_loop(0, nc, both, init)
    for d in range(2):
        c_ref[0, d, 0], n_ref[0, d, 0], m_ref[0, d, 0] = final[d]
    h = _rms(hs_scr[0] + hs_scr[1], ng_ref[0])
    h_ref[...] = (h * jax.nn.sigmoid(mo_ref[...])).astype(BF16)


def _mlstm(z, gcol, grow, bias_c, bias_r, c0, n0, m0, norm_g, *, ctx):
    seq = SEQ if ctx else DEC_SEQ
    nb = BATCH if ctx else DEC_BATCH
    row0 = 0 if ctx else T_CTX // DEC_SEQ
    hd = MLSTM_DIM
    nh = MLSTM_HEADS
    zcol = lambda off: (lambda b, h: (row0 + b, off // hd + h))
    st5 = lambda b, h: (b, 0, h, 0, 0)
    return pl.pallas_call(
        functools.partial(_mlstm_kernel, seq=seq, chunk0=0),
        grid=(nb, nh),
        in_specs=[pl.BlockSpec((seq, hd), zcol(Z_MQ)),
                  pl.BlockSpec((seq, hd), zcol(Z_MK)),
                  pl.BlockSpec((seq, hd), zcol(Z_MV)),
                  pl.BlockSpec((seq, hd), zcol(Z_MO)),
                  pl.BlockSpec((1, seq, 4), lambda b, h: (h, row0 + b, 0)),
                  pl.BlockSpec((1, seq // CHUNK, 4, CHUNK), lambda b, h: (h, row0 + b, 0, 0)),
                  pl.BlockSpec((1, 1, 4), lambda b, h: (h, 0, 0)),
                  pl.BlockSpec((1, 4, 1), lambda b, h: (h, 0, 0)),
                  pl.BlockSpec((1, 2, 1, hd, hd), st5),
                  pl.BlockSpec((1, 2, 1, 1, hd), st5),
                  pl.BlockSpec((1, 2, 1, 1, 1), st5),
                  pl.BlockSpec((1, 1, hd), lambda b, h: (h, 0, 0))],
        out_specs=[pl.BlockSpec((seq, hd), lambda b, h: (b, h)),
                   pl.BlockSpec((1, 2, 1, hd, hd), st5),
                   pl.BlockSpec((1, 2, 1, 1, hd), st5),
                   pl.BlockSpec((1, 2, 1, 1, 1), st5)],
        out_shape=[jax.ShapeDtypeStruct((nb * seq, MIX_W), BF16),
                   jax.ShapeDtypeStruct((nb, 2, nh, hd, hd), F32),
                   jax.ShapeDtypeStruct((nb, 2, nh, 1, hd), F32),
                   jax.ShapeDtypeStruct((nb, 2, nh, 1, 1), F32)],
        scratch_shapes=[pltpu.VMEM((2, seq, hd), F32)],
        compiler_params=_cparams(("arbitrary", "arbitrary")),
        name="mlstm_ctx" if ctx else "mlstm_lat",
    )(z, z, z, z, gcol, grow, bias_c, bias_r, c0, n0, m0, norm_g.reshape(nh, 1, hd))


GMLP_TM = 512


def _gmlp_kernel(gu_ref, gv_ref, ng_ref, ws_ref, bs_ref, o_ref):
    vn = _rms(gv_ref[...], ng_ref[...]).astype(BF16)
    gw = MIX_W // GMLP_GROUPS
    for c in range(GMLP_TM // CHUNK):
        r = slice(c * CHUNK, (c + 1) * CHUNK)
        for g in range(GMLP_GROUPS):
            cs = slice(g * gw, (g + 1) * gw)
            mixed = jnp.dot(ws_ref[g], vn[r, cs], preferred_element_type=F32) + bs_ref[:, g:g + 1]
            o_ref[r, cs] = (gu_ref[r, cs] * mixed).astype(BF16)


def _gmlp(z, norm_g, w_s, b_s):
    tm = GMLP_TM
    return pl.pallas_call(
        _gmlp_kernel,
        grid=(T_ALL // tm,),
        in_specs=[pl.BlockSpec((tm, MIX_W), lambda i: (i, Z_GU // MIX_W)),
                  pl.BlockSpec((tm, MIX_W), lambda i: (i, Z_GV // MIX_W)),
                  pl.BlockSpec((1, MIX_W), lambda i: (0, 0)),
                  pl.BlockSpec((GMLP_GROUPS, CHUNK, CHUNK), lambda i: (0, 0, 0)),
                  pl.BlockSpec((CHUNK, GMLP_GROUPS), lambda i: (0, 0))],
        out_specs=pl.BlockSpec((tm, MIX_W), lambda i: (i, 0)),
        out_shape=jax.ShapeDtypeStruct((T_ALL, MIX_W), BF16),
        compiler_params=_cparams(("arbitrary",)),
        name="gmlp",
    )(z, z, norm_g.reshape(1, MIX_W), w_s.astype(BF16), b_s.T)


OUT_TM = 256


def _out_kernel(*refs, router):
    (ac_ref, al_ref, b_ref, cc_ref, cl_ref, d_ref, xc_ref, xl_ref, mod_ref, g_ref, w_ref), refs = refs[:11], refs[11:]
    if router:
        rh_ref, rl_ref, rb_ref, x1_ref, h2_ref, rt_ref = refs
    else:
        x1_ref, h2_ref = refs
    mix = jnp.dot(_pick(OUT_TM, ac_ref, al_ref), w_ref[0:MIX_W, :], preferred_element_type=F32)
    mix += jnp.dot(b_ref[...], w_ref[MIX_W:2 * MIX_W, :], preferred_element_type=F32)
    mix += jnp.dot(_pick(OUT_TM, cc_ref, cl_ref), w_ref[2 * MIX_W:3 * MIX_W, :], preferred_element_type=F32)
    mix += jnp.dot(d_ref[...], w_ref[3 * MIX_W:4 * MIX_W, :], preferred_element_type=F32)
    x1 = _pick(OUT_TM, xc_ref, xl_ref) + mod_ref[0, 2:3, :] * mix
    x1_ref[...] = x1
    h2 = _rms(x1, g_ref[...]) * (1.0 + mod_ref[0, 4:5, :]) + mod_ref[0, 3:4, :]
    hi = h2.astype(BF16)
    h2_ref[...] = h2 if router else hi
    if router:
        lo = (h2 - hi.astype(F32)).astype(BF16)
        logits = (jnp.dot(hi, rh_ref[...], preferred_element_type=F32)
                  + jnp.dot(lo, rh_ref[...], preferred_element_type=F32)
                  + jnp.dot(hi, rl_ref[...], preferred_element_type=F32)) + rb_ref[...]
        lane = lax.broadcasted_iota(jnp.int32, logits.shape, 1)
        neg_inf = jnp.float32(-jnp.inf)
        lg = jnp.where(lane < N_EXPERTS, logits, neg_inf)
        m1 = jnp.max(lg, axis=-1, keepdims=True)
        i1 = jnp.min(jnp.where(lg == m1, lane, LANE), axis=-1, keepdims=True)
        lg2 = jnp.where(lane == i1, neg_inf, lg)
        m2 = jnp.max(lg2, axis=-1, keepdims=True)
        i2 = jnp.min(jnp.where(lg2 == m2, lane, LANE), axis=-1, keepdims=True)
        e = jnp.exp(m2 - m1)
        w1 = 1.0 / (1.0 + e)
        w2 = e / (1.0 + e)
        rt = jnp.where(lane == 0, i1.astype(F32), 0.0)
        rt = jnp.where(lane == 1, i2.astype(F32), rt)
        rt = jnp.where(lane == 2, w1, rt)
        rt = jnp.where(lane == 3, w2, rt)
        rt_ref[...] = rt


def _out_proj(att_pair, s5_out, ml_pair, gm_out, x_pair, mod_l, g2, w_out, router=None):
    tm = OUT_TM
    row = lambda i: (i, 0)
    fixed = lambda i: (0, 0)
    full = pl.BlockSpec((tm, MIX_W), row)
    pair = _pair_specs(tm, (tm, MIX_W))
    in_specs = pair + [full] + pair + [full] + _pair_specs(tm, (tm, D_MODEL)) + [
        pl.BlockSpec((1, MOD_CHUNKS, D_MODEL), lambda i: (_mod_group(i, tm), 0, 0)),
        pl.BlockSpec((1, D_MODEL), fixed),
        pl.BlockSpec((D_MODEL, D_MODEL), fixed, pipeline_mode=pl.Buffered(1))]
    args = [*att_pair, s5_out, *ml_pair, gm_out, *x_pair, mod_l, g2.reshape(1, D_MODEL), w_out]
    out_specs = [pl.BlockSpec((tm, D_MODEL), row), pl.BlockSpec((tm, D_MODEL), row)]
    out_shape = [jax.ShapeDtypeStruct((T_ALL, D_MODEL), F32),
                 jax.ShapeDtypeStruct((T_ALL, D_MODEL), BF16 if router is None else F32)]
    if router is not None:
        rw, rb = router
        rw = jnp.pad(rw, ((0, 0), (0, LANE - N_EXPERTS)))
        rh = rw.astype(BF16)
        rl = (rw - rh.astype(F32)).astype(BF16)
        in_specs += [pl.BlockSpec((D_MODEL, LANE), fixed)] * 2 + [pl.BlockSpec((1, LANE), fixed)]
        args += [rh, rl, jnp.pad(rb, (0, LANE - N_EXPERTS)).reshape(1, LANE)]
        out_specs.append(pl.BlockSpec((tm, LANE), row))
        out_shape.append(jax.ShapeDtypeStruct((T_ALL, LANE), F32))
    return pl.pallas_call(
        functools.partial(_out_kernel, router=router is not None),
        grid=(T_ALL // tm,),
        in_specs=in_specs, out_specs=out_specs, out_shape=out_shape,
        compiler_params=_cparams(("arbitrary",)),
        name="out_proj_router" if router is not None else "out_proj",
    )(*args)


def _ffn_kernel(x_ref, wg_ref, wu_ref, wd_ref, o_ref):
    @pl.when(pl.program_id(1) == 0)
    def _():
        o_ref[...] = jnp.zeros_like(o_ref)

    def sub(s, carry):
        rows = pl.ds(pl.multiple_of(s * FFN_DENSE_SUB, FFN_DENSE_SUB), FFN_DENSE_SUB)
        xs = x_ref[rows, :]
        g = jnp.dot(xs, wg_ref[...], preferred_element_type=F32)
        u = jnp.dot(xs, wu_ref[...], preferred_element_type=F32)
        a = (g * jax.nn.sigmoid(g) * u).astype(BF16)
        o_ref[rows, :] += jnp.dot(a, wd_ref[...], preferred_element_type=F32)
        return carry

    lax.fori_loop(0, FFN_TM // FFN_DENSE_SUB, sub, 0)


def _ffn(x, w_gate, w_up, w_down):
    return pl.pallas_call(
        _ffn_kernel,
        grid=(T_ALL // FFN_TM, D_FF // FFN_TF),
        in_specs=[pl.BlockSpec((FFN_TM, D_MODEL), lambda i, f: (i, 0)),
                  pl.BlockSpec((D_MODEL, FFN_TF), lambda i, f: (0, f)),
                  pl.BlockSpec((D_MODEL, FFN_TF), lambda i, f: (0, f)),
                  pl.BlockSpec((FFN_TF, D_MODEL), lambda i, f: (f, 0))],
        out_specs=pl.BlockSpec((FFN_TM, D_MODEL), lambda i, f: (i, 0)),
        out_shape=jax.ShapeDtypeStruct((T_ALL, D_MODEL), F32),
        compiler_params=_cparams(("arbitrary", "arbitrary")),
        name="ffn",
    )(x, w_gate, w_up, w_down)


def _moe_ffn_kernel(te_ref, ns_ref, src_ref, x_hbm, wg_ref, wu_ref, wd_ref, o_ref, xf, xb, sems):
    i = pl.program_id(0)
    f = pl.program_id(1)
    ntiles = pl.num_programs(0)

    def row_copy(src_row, slot, r):
        return pltpu.make_async_copy(x_hbm.at[pl.ds(src_row, 1)], xf.at[slot, pl.ds(r, 1)], sems.at[slot])

    def issue(tile, slot):
        def body(r8, carry):
            for u in range(ISSUE_UNROLL):
                r = r8 * ISSUE_UNROLL + u
                row_copy(src_ref[tile * FFN_TM + r], slot, r).start()
            return carry

        lax.fori_loop(0, ns_ref[tile] * (FFN_SUB // ISSUE_UNROLL), body, 0)

    def wait_rows(tile, slot):
        def body(s, carry):
            rows = pl.ds(pl.multiple_of(s * FFN_SUB, FFN_SUB), FFN_SUB)
            pltpu.make_async_copy(x_hbm.at[pl.ds(0, FFN_SUB)], xf.at[slot, rows], sems.at[slot]).wait()
            return carry

        lax.fori_loop(0, ns_ref[tile], body, 0)

    @pl.when(f == 0)
    def _():
        slot = i % 2

        @pl.when(i == 0)
        def _():
            issue(0, 0)

        wait_rows(i, slot)

        @pl.when(i + 1 < ntiles)
        def _():
            issue(i + 1, 1 - slot)

        def cast(s, carry):
            rows = pl.ds(pl.multiple_of(s * FFN_SUB, FFN_SUB), FFN_SUB)
            xb[rows, :] = xf[slot, rows, :].astype(BF16)
            return carry

        lax.fori_loop(0, ns_ref[i], cast, 0)
        o_ref[...] = jnp.zeros_like(o_ref)

    def sub(s, carry):
        rows = pl.ds(pl.multiple_of(s * FFN_SUB, FFN_SUB), FFN_SUB)
        xs = xb[rows, :]
        g = jnp.dot(xs, wg_ref[0], preferred_element_type=F32)
        u = jnp.dot(xs, wu_ref[0], preferred_element_type=F32)
        a = (g * jax.nn.sigmoid(g) * u).astype(BF16)
        o_ref[rows, :] += jnp.dot(a, wd_ref[0], preferred_element_type=F32)
        return carry

    lax.fori_loop(0, ns_ref[i], sub, 0)


def _moe_ffn(h2, tile_e, tile_ns, src, w_gate, w_up, w_down):
    nf = D_FF // FFN_TF

    def fcol(i, f, ns):
        return jnp.where(ns[i] > 0, f, nf - 1)

    return pl.pallas_call(
        _moe_ffn_kernel,
        grid_spec=pltpu.PrefetchScalarGridSpec(
            num_scalar_prefetch=3,
            grid=(MOE_TILES, nf),
            in_specs=[pl.BlockSpec(memory_space=pl.ANY),
                      pl.BlockSpec((1, D_MODEL, FFN_TF), lambda i, f, te, ns, sr: (te[i], 0, fcol(i, f, ns))),
                      pl.BlockSpec((1, D_MODEL, FFN_TF), lambda i, f, te, ns, sr: (te[i], 0, fcol(i, f, ns))),
                      pl.BlockSpec((1, FFN_TF, D_MODEL), lambda i, f, te, ns, sr: (te[i], fcol(i, f, ns), 0))],
            out_specs=pl.BlockSpec((FFN_TM, D_MODEL), lambda i, f, te, ns, sr: (i, 0)),
            scratch_shapes=[pltpu.VMEM((2, FFN_TM, D_MODEL), F32),
                            pltpu.VMEM((FFN_TM, D_MODEL), BF16),
                            pltpu.SemaphoreType.DMA((2,))]),
        out_shape=jax.ShapeDtypeStruct((MOE_TILES * FFN_TM, D_MODEL), F32),
        compiler_params=_cparams(("arbitrary", "arbitrary")),
        name="moe_ffn",
    )(tile_e, tile_ns, src, h2, w_gate, w_up, w_down)


RES_TM = 256


def _store_pair(val, oc_ref, ol_ref):
    i = pl.program_id(0)

    @pl.when(i < T_CTX // RES_TM)
    def _():
        oc_ref[...] = val

    @pl.when(i >= T_CTX // RES_TM)
    def _():
        ol_ref[...] = val


def _res_kernel(x_ref, y_ref, mod_ref, g_ref, oc_ref, ol_ref, *, final):
    x2 = x_ref[...] + mod_ref[0, 5:6, :] * y_ref[...]
    _store_pair(_rms(x2, g_ref[...]) if final else x2, oc_ref, ol_ref)


def _moe_res_kernel(slot_ref, x_ref, rt_ref, mod_ref, g_ref, ys_hbm, oc_ref, ol_ref, ybuf, sems, *, final):
    i = pl.program_id(0)
    nt = pl.num_programs(0)

    def row_copy(src_row, buf, k, r):
        return pltpu.make_async_copy(ys_hbm.at[pl.ds(src_row, 1)], ybuf.at[buf, k, pl.ds(r, 1)], sems.at[buf])

    def issue(tile, buf):
        def body(r8, carry):
            for u in range(ISSUE_UNROLL):
                r = r8 * ISSUE_UNROLL + u
                for k in range(2):
                    row_copy(slot_ref[k * T_ALL + tile * RES_TM + r], buf, k, r).start()
            return carry

        lax.fori_loop(0, RES_TM // ISSUE_UNROLL, body, 0)

    buf = i % 2

    @pl.when(i == 0)
    def _():
        issue(0, 0)

    for k in range(2):
        pltpu.make_async_copy(ys_hbm.at[pl.ds(0, RES_TM)], ybuf.at[buf, k], sems.at[buf]).wait()

    @pl.when(i + 1 < nt)
    def _():
        issue(i + 1, 1 - buf)

    rt = rt_ref[...]
    ffn = rt[:, 2:3] * ybuf[buf, 0] + rt[:, 3:4] * ybuf[buf, 1]
    x2 = x_ref[...] + mod_ref[0, 5:6, :] * ffn
    _store_pair(_rms(x2, g_ref[...]) if final else x2, oc_ref, ol_ref)


def _residual(x1, y, mod_l, final_g, *, route=None, slot=None, final):
    tm = RES_TM
    nt = T_ALL // tm
    out_shape = [jax.ShapeDtypeStruct((T_CTX, D_MODEL), F32), jax.ShapeDtypeStruct((T_LAT, D_MODEL), F32)]
    out_specs = _pair_specs(tm, (tm, D_MODEL))
    if route is None:
        row = lambda i: (i, 0)
        return pl.pallas_call(
            functools.partial(_res_kernel, final=final),
            grid=(nt,),
            in_specs=[pl.BlockSpec((tm, D_MODEL), row), pl.BlockSpec((tm, D_MODEL), row),
                      pl.BlockSpec((1, MOD_CHUNKS, D_MODEL), lambda i: (_mod_group(i, tm), 0, 0)),
                      pl.BlockSpec((1, D_MODEL), lambda i: (0, 0))],
            out_specs=out_specs,
            out_shape=out_shape,
            compiler_params=_cparams(("arbitrary",)),
            name="ffn_residual",
        )(x1, y, mod_l, final_g.reshape(1, D_MODEL))
    row = lambda i, sl: (i, 0)
    return pl.pallas_call(
        functools.partial(_moe_res_kernel, final=final),
        grid_spec=pltpu.PrefetchScalarGridSpec(
            num_scalar_prefetch=1,
            grid=(nt,),
            in_specs=[pl.BlockSpec((tm, D_MODEL), row), pl.BlockSpec((tm, LANE), row),
                      pl.BlockSpec((1, MOD_CHUNKS, D_MODEL), lambda i, sl: (_mod_group(i, tm), 0, 0)),
                      pl.BlockSpec((1, D_MODEL), lambda i, sl: (0, 0)),
                      pl.BlockSpec(memory_space=pl.ANY)],
            out_specs=out_specs,
            scratch_shapes=[pltpu.VMEM((2, 2, tm, D_MODEL), F32), pltpu.SemaphoreType.DMA((2,))]),
        out_shape=out_shape,
        compiler_params=_cparams(("arbitrary",)),
        name="moe_residual",
    )(slot, x1, route, mod_l, final_g.reshape(1, D_MODEL), y)


def _route_plan(route):
    e_flat = jnp.concatenate([route[:, 0], route[:, 1]]).astype(jnp.int32)
    onehot = (e_flat[:, None] == jnp.arange(N_EXPERTS, dtype=jnp.int32)[None, :]).astype(jnp.int32)
    ranks = jnp.cumsum(onehot, axis=0) - onehot
    rank = jnp.sum(ranks * onehot, axis=1)
    counts = jnp.sum(onehot, axis=0)
    tiles = (counts + FFN_TM - 1) // FFN_TM
    tile_start = jnp.cumsum(tiles) - tiles
    slot = tile_start[e_flat] * FFN_TM + rank
    n_slots = MOE_TILES * FFN_TM
    tok = jnp.concatenate([jnp.arange(T_ALL, dtype=jnp.int32)] * 2)
    src = jnp.zeros((n_slots,), jnp.int32).at[slot].set(tok)
    tile_ids = jnp.arange(MOE_TILES, dtype=jnp.int32)
    ends = jnp.cumsum(tiles)
    tile_e = jnp.sum((tile_ids[:, None] >= ends[None, :]).astype(jnp.int32), axis=1)
    used = tile_e < N_EXPERTS
    last_e = jnp.max(jnp.where(counts > 0, jnp.arange(N_EXPERTS, dtype=jnp.int32), 0))
    tile_e = jnp.where(used, tile_e, last_e)
    rows_in_tile = jnp.clip(counts[tile_e] - (tile_ids - tile_start[tile_e]) * FFN_TM, 0, FFN_TM)
    tile_ns = jnp.where(used, (rows_in_tile + FFN_SUB - 1) // FFN_SUB, 0).astype(jnp.int32)
    return src, slot, tile_e.astype(jnp.int32), tile_ns


def _rope_tables():
    length = DEC_SEQ
    r = jnp.repeat(jnp.arange(length // GRID_W, dtype=F32), GRID_W)
    col = (jnp.arange(length) % GRID_W).astype(F32)
    half = HEAD_DIM // 2
    inv = ROPE_THETA ** (-jnp.arange(0, half, 2, dtype=F32) / half)
    ar, ac = r[:, None] * inv, col[:, None] * inv
    cos = jnp.concatenate([jnp.cos(ar), jnp.cos(ar), jnp.cos(ac), jnp.cos(ac)], axis=-1)
    sin = jnp.concatenate([-jnp.sin(ar), jnp.sin(ar), -jnp.sin(ac), jnp.sin(ac)], axis=-1)
    return cos, sin


def kernel(x_prompt, x_sample, c, c_ctx, cache_attn_k, cache_attn_v, state_s5_re, state_s5_im, state_mlstm_c, state_mlstm_n, state_mlstm_m, norm1_g, norm2_g, w_mod, b_mod, w_in, w_out, q_norm_g, k_norm_g, s5_a_re, s5_a_im, s5_log_dt, s5_b_re, s5_b_im, s5_c_re, s5_c_im, s5_d, s5_w_glu, s5_b_glu, mlstm_i_bias, mlstm_f_bias, mlstm_norm_g, gmlp_norm_g, gmlp_w_s, gmlp_b_s, ffn_w_gate, ffn_w_up, ffn_w_down, moe_router, moe_router_bias, moe_w_gate, moe_w_up, moe_w_down, final_norm_g):
    x = (x_prompt.reshape(T_CTX, D_MODEL), x_sample.reshape(T_LAT, D_MODEL))
    cond = jnp.concatenate([c_ctx[None, :], c], axis=0)
    mod = _modulation(cond, w_mod, b_mod)
    cos, sin = _rope_tables()
    nh = MLSTM_HEADS
    zeros_s5 = jnp.zeros((BATCH, 2, S5_SG, 1, S5_SGW), F32)
    zeros_c = jnp.zeros((BATCH, 2, nh, MLSTM_DIM, MLSTM_DIM), F32)
    zeros_n = jnp.zeros((BATCH, 2, nh, 1, MLSTM_DIM), F32)
    zeros_m = jnp.zeros((BATCH, 2, nh, 1, 1), F32)

    ctx_states = []
    for l in range(DEPTH):
        use_moe = l % 2 == 1
        j = l // 2
        wl = w_in[l]
        w_p = jnp.concatenate([wl[:, :3584], wl[:, 3600:], wl[:, 3584:3600],
                               jnp.zeros((D_MODEL, Z_COLS - 4624), F32)], axis=1).astype(BF16)
        z = _in_proj(x, mod[l], norm1_g[l], w_p)

        att_c, k_new, v_new = _attention(z, None, q_norm_g[l], k_norm_g[l], ctx=True)
        (att_l,) = _attention(z, None, q_norm_g[l], k_norm_g[l], ctx=False,
                              kpast=cache_attn_k[:, l], vpast=cache_attn_v[:, l], cos=cos, sin=sin)

        prep = _s5_prep(s5_a_re[l], s5_a_im[l], s5_log_dt[l], s5_b_re[l], s5_b_im[l], s5_c_re[l], s5_c_im[l])
        y_c, hf_re, hf_im = _s5_scan(z, prep, zeros_s5, zeros_s5, ctx=True)
        st = lambda s: s[:, l].reshape(DEC_BATCH, 2, S5_SG, 1, S5_SGW)
        y_l, _, _ = _s5_scan(z, prep, st(state_s5_re), st(state_s5_im), ctx=False)
        s5_out = _s5_post((y_c, y_l), z, s5_d[l], s5_w_glu[l], s5_b_glu[l])

        mg = z[:, Z_MG:Z_MG + 16].reshape(T_ALL, 2, 2, nh)
        gcol = mg.transpose(3, 0, 1, 2).reshape(nh, T_ALL, 4)
        grow = mg.reshape(T_ALL // CHUNK, CHUNK, 4, nh).transpose(3, 0, 2, 1)
        bias = jnp.stack([mlstm_i_bias[l], mlstm_f_bias[l]], axis=1)
        bias_c = bias.transpose(2, 0, 1).reshape(nh, 1, 4)
        bias_r = bias_c.reshape(nh, 4, 1)
        ml_c_out, c_new, n_new, m_new = _mlstm(z, gcol, grow, bias_c, bias_r, zeros_c, zeros_n, zeros_m,
                                               mlstm_norm_g[l], ctx=True)
        ml_l_out, _, _, _ = _mlstm(z, gcol, grow, bias_c, bias_r, state_mlstm_c[:, l],
                                   state_mlstm_n[:, l].reshape(DEC_BATCH, 2, nh, 1, MLSTM_DIM),
                                   state_mlstm_m[:, l].reshape(DEC_BATCH, 2, nh, 1, 1),
                                   mlstm_norm_g[l], ctx=False)

        gm_out = _gmlp(z, gmlp_norm_g[l], gmlp_w_s[l], gmlp_b_s[l])

        ctx_states.append((k_new, v_new,
                           hf_re.reshape(BATCH, 2, S5_GROUPS, S5_STATE), hf_im.reshape(BATCH, 2, S5_GROUPS, S5_STATE),
                           c_new, n_new.reshape(BATCH, 2, nh, MLSTM_DIM), m_new.reshape(BATCH, 2, nh)))

        w_o = w_out[l].astype(BF16)
        final = l == DEPTH - 1
        mixed = ((att_c, att_l), s5_out, (ml_c_out, ml_l_out), gm_out)
        if use_moe:
            x1, h2, route = _out_proj(*mixed, x, mod[l], norm2_g[l], w_o,
                                      router=(moe_router[j], moe_router_bias[j]))
            src, slot, tile_e, tile_ns = _route_plan(route)
            ys = _moe_ffn(h2, tile_e, tile_ns, src, moe_w_gate[j].astype(BF16), moe_w_up[j].astype(BF16),
                          moe_w_down[j].astype(BF16))
            x = _residual(x1, ys, mod[l], final_norm_g, route=route, slot=slot, final=final)
        else:
            x1, h2 = _out_proj(*mixed, x, mod[l], norm2_g[l], w_o)
            y = _ffn(h2, ffn_w_gate[j].astype(BF16), ffn_w_up[j].astype(BF16), ffn_w_down[j].astype(BF16))
            x = _residual(x1, y, mod[l], final_norm_g, final=final)

    y_prompt = x[0].reshape(BATCH, SEQ, D_MODEL)
    y_sample = x[1].reshape(DEC_BATCH, DEC_SEQ, D_MODEL)
    stack = lambda i: jnp.stack([s[i] for s in ctx_states], axis=1)
    return (y_prompt, y_sample, stack(0), stack(1), stack(2), stack(3), stack(4), stack(5), stack(6))
```

```python
import functools
import math

import numpy as np
import jax
import jax.numpy as jnp
from jax import lax
from jax.experimental import pallas as pl
from jax.experimental.pallas import tpu as pltpu

F32 = jnp.float32
BF16 = jnp.bfloat16

D_MODEL = 2048
BATCH = 16
SEQ = 256
DEPTH = 2
DEC_BATCH = 2
DEC_SEQ = 2048
PAST_LEN = 256
GRID_W = 64
MIX_W = 512
ATT_HEADS = 4
ATT_KV_HEADS = 2
HEAD_DIM = 128
ROPE_THETA = 10000.0
S5_CH = 16
S5_GROUPS = 32
S5_STATE = 64
MLSTM_HEADS = 4
MLSTM_DIM = 128
MLSTM_HG = 1
CHUNK = 128
GMLP_GROUPS = 4
D_FF = 7168
N_EXPERTS = 8
MOD_CHUNKS = 6
EPS = 1e-6

T_CTX = BATCH * SEQ
T_LAT = DEC_BATCH * DEC_SEQ
T_ALL = T_CTX + T_LAT
N_GROUPS_MOD = 1 + DEC_BATCH

Z_AQ, Z_AK, Z_AV, Z_SX = 0, 512, 768, 1024
Z_MQ, Z_MK, Z_MV, Z_MO = 1536, 2048, 2560, 3072
Z_GU, Z_GV, Z_MG = 3584, 4096, 4608
Z_COLS = 4736
LANE = 128
SUBLANE = 8

VMEM_LIMIT = 56 * 1024 * 1024

S5_SG = 4
S5_SGW = 8 * S5_STATE
S5_LAGS = 8

FFN_TM = 1024
FFN_SUB = 256
FFN_DENSE_SUB = 512
FFN_TF = 512
MOE_TILES = 2 * T_ALL // FFN_TM + N_EXPERTS
ISSUE_UNROLL = 8


def _cparams(sem=None):
    return pltpu.CompilerParams(dimension_semantics=sem, vmem_limit_bytes=VMEM_LIMIT)


def _mod_group(i, tm):
    return jnp.maximum(i * tm // DEC_SEQ - (T_CTX // DEC_SEQ - 1), 0)


def _rms(x, g):
    return x * lax.rsqrt(jnp.mean(x * x, axis=-1, keepdims=True) + EPS) * g


MOD_TN = 1024


def _mod_kernel(cb_ref, w_ref, b_ref, o_ref, silu_scr):
    @pl.when((pl.program_id(0) == 0) & (pl.program_id(1) == 0))
    def _():
        c = cb_ref[...]
        silu_scr[...] = c * jax.nn.sigmoid(c)

    w = w_ref[0]
    rows = []
    for r in range(N_GROUPS_MOD):
        s = silu_scr[r]
        parts = [jnp.sum(w[:, j * LANE:(j + 1) * LANE] * s, axis=0, keepdims=True)
                 for j in range(MOD_TN // LANE)]
        rows.append(jnp.concatenate(parts, axis=1))
    rows.append(jnp.zeros((SUBLANE - N_GROUPS_MOD, MOD_TN), F32))
    o_ref[0] = jnp.concatenate(rows, axis=0) + b_ref[0]


def _modulation(cond, w_mod, b_mod):
    cb = jnp.broadcast_to(cond[:, :, None], (N_GROUPS_MOD, D_MODEL, LANE))
    n = MOD_CHUNKS * D_MODEL
    out = pl.pallas_call(
        _mod_kernel,
        grid=(DEPTH, n // MOD_TN),
        in_specs=[pl.BlockSpec((N_GROUPS_MOD, D_MODEL, LANE), lambda l, j: (0, 0, 0)),
                  pl.BlockSpec((1, D_MODEL, MOD_TN), lambda l, j: (l, 0, j)),
                  pl.BlockSpec((1, 1, MOD_TN), lambda l, j: (l, 0, j))],
        out_specs=pl.BlockSpec((1, SUBLANE, MOD_TN), lambda l, j: (l, 0, j)),
        out_shape=jax.ShapeDtypeStruct((DEPTH, SUBLANE, n), F32),
        scratch_shapes=[pltpu.VMEM((N_GROUPS_MOD, D_MODEL, LANE), F32)],
        compiler_params=_cparams(("arbitrary", "arbitrary")),
        name="adaln_mod",
    )(cb, w_mod, b_mod.reshape(DEPTH, 1, n))
    return out[:, :N_GROUPS_MOD].reshape(DEPTH, N_GROUPS_MOD, MOD_CHUNKS, D_MODEL)


IN_TM = 256
IN_CHUNK = 512


def _pair_specs(tm, block, rows_dim=0):
    nc = T_CTX // tm

    def index(row):
        return tuple(row if d == rows_dim else 0 for d in range(len(block)))

    return [pl.BlockSpec(block, lambda i, *_: index(jnp.minimum(i, nc - 1))),
            pl.BlockSpec(block, lambda i, *_: index(jnp.maximum(i - nc, 0)))]


def _pick(tm, ctx_ref, lat_ref):
    return jnp.where(pl.program_id(0) < T_CTX // tm, ctx_ref[...], lat_ref[...])


def _store_pair(tm, val, oc_ref, ol_ref):
    i = pl.program_id(0)

    @pl.when(i < T_CTX // tm)
    def _():
        oc_ref[...] = val

    @pl.when(i >= T_CTX // tm)
    def _():
        ol_ref[...] = val


def _in_kernel(*refs, pending):
    if pending:
        x1_ref, y_ref, modp_ref, mod_ref, g_ref, w_ref, z_ref, oc_ref, ol_ref = refs
        x = x1_ref[...] + modp_ref[0, 5:6, :] * y_ref[...]
        _store_pair(IN_TM, x, oc_ref, ol_ref)
    else:
        xc_ref, xl_ref, mod_ref, g_ref, w_ref, z_ref = refs
        x = _pick(IN_TM, xc_ref, xl_ref)
    shift = mod_ref[0, 0:1, :]
    scale = mod_ref[0, 1:2, :]
    h = (_rms(x, g_ref[...]) * (1.0 + scale) + shift).astype(BF16)
    for c0 in range(0, Z_COLS, IN_CHUNK):
        cw = min(IN_CHUNK, Z_COLS - c0)
        z_ref[:, c0:c0 + cw] = jnp.dot(h, w_ref[:, c0:c0 + cw], preferred_element_type=F32)


def _in_proj(x, mod_l, g, w_p):
    tm = IN_TM
    pending = len(x) == 3
    row = lambda i: (i, 0)
    mod_spec = pl.BlockSpec((1, MOD_CHUNKS, D_MODEL), lambda i: (_mod_group(i, tm), 0, 0))
    if pending:
        x_specs = [pl.BlockSpec((tm, D_MODEL), row), pl.BlockSpec((tm, D_MODEL), row), mod_spec]
    else:
        x_specs = _pair_specs(tm, (tm, D_MODEL))
    out_specs = [pl.BlockSpec((tm, Z_COLS), row)]
    out_shape = [jax.ShapeDtypeStruct((T_ALL, Z_COLS), F32)]
    if pending:
        out_specs += _pair_specs(tm, (tm, D_MODEL))
        out_shape += [jax.ShapeDtypeStruct((T_CTX, D_MODEL), F32), jax.ShapeDtypeStruct((T_LAT, D_MODEL), F32)]
    outs = pl.pallas_call(
        functools.partial(_in_kernel, pending=pending),
        grid=(T_ALL // tm,),
        in_specs=x_specs + [
                  mod_spec,
                  pl.BlockSpec((1, D_MODEL), lambda i: (0, 0)),
                  pl.BlockSpec((D_MODEL, Z_COLS), lambda i: (0, 0), pipeline_mode=pl.Buffered(1))],
        out_specs=out_specs,
        out_shape=out_shape,
        compiler_params=_cparams(("arbitrary",)),
        name="in_proj_res" if pending else "in_proj",
    )(*x, mod_l, g.reshape(1, D_MODEL), w_p)
    return (outs[0], (outs[1], outs[2])) if pending else (outs[0], x)


def _rope(t, c, s):
    lane = lax.broadcasted_iota(jnp.int32, t.shape, 1)
    first = (lane % (HEAD_DIM // 2)) < (HEAD_DIM // 4)
    swapped = jnp.where(first, pltpu.roll(t, HEAD_DIM - HEAD_DIM // 4, 1), pltpu.roll(t, HEAD_DIM // 4, 1))
    return t * c + swapped * s


def _attn_kernel(*refs, seq, past, rope):
    if rope:
        (aq_ref, ak_ref, av_ref, kp_ref, vp_ref, cos_ref, sin_ref, qg_ref, kg_ref,
         att_ref, kb_scr, vb_scr) = refs
    else:
        aq_ref, ak_ref, av_ref, qg_ref, kg_ref, att_ref, knew_ref, vnew_ref, kb_scr, vb_scr = refs
    kn = _rms(ak_ref[...], kg_ref[...])
    v = av_ref[...]
    if rope:
        kn = _rope(kn, cos_ref[...], sin_ref[...])
        kb_scr[seq:seq + past, :] = kp_ref[0, 0].astype(BF16)
        vb_scr[seq:seq + past, :] = vp_ref[0, 0].astype(BF16)
    else:
        knew_ref[0, 0] = kn
        vnew_ref[0, 0] = v
    kb_scr[0:seq, :] = kn.astype(BF16)
    vb_scr[0:seq, :] = v.astype(BF16)
    grp = ATT_HEADS // ATT_KV_HEADS

    def q_block(qb, carry):
        rows = pl.ds(pl.multiple_of(qb * CHUNK, CHUNK), CHUNK)
        qs = []
        for g in range(grp):
            q = _rms(aq_ref[rows, g * HEAD_DIM:(g + 1) * HEAD_DIM], qg_ref[...])
            if rope:
                q = _rope(q, cos_ref[rows, :], sin_ref[rows, :])
            qs.append(q)
        q2 = jnp.concatenate(qs, axis=0).astype(BF16)
        s = lax.dot_general(q2, kb_scr[...], (((1,), (1,)), ((), ())),
                            preferred_element_type=F32) * (HEAD_DIM ** -0.5)
        m = jnp.max(s, axis=-1, keepdims=True)
        p = jnp.exp(s - m)
        den = jnp.sum(p, axis=-1, keepdims=True)
        o = jnp.dot(p.astype(BF16), vb_scr[...], preferred_element_type=F32) / den
        for g in range(grp):
            att_ref[rows, g * HEAD_DIM:(g + 1) * HEAD_DIM] = o[g * CHUNK:(g + 1) * CHUNK].astype(BF16)
        return carry

    lax.fori_loop(0, seq // CHUNK, q_block, 0)


def _attention(z, att_out_shape, qg, kg, *, ctx, kpast=None, vpast=None, cos=None, sin=None):
    seq = SEQ if ctx else DEC_SEQ
    nb = BATCH if ctx else DEC_BATCH
    row0 = 0 if ctx else T_CTX // DEC_SEQ
    past = 0 if ctx else PAST_LEN
    qw = HEAD_DIM * (ATT_HEADS // ATT_KV_HEADS)
    in_specs = [pl.BlockSpec((seq, qw), lambda b, h: (row0 + b, Z_AQ // qw + h)),
                pl.BlockSpec((seq, HEAD_DIM), lambda b, h: (row0 + b, Z_AK // HEAD_DIM + h)),
                pl.BlockSpec((seq, HEAD_DIM), lambda b, h: (row0 + b, Z_AV // HEAD_DIM + h))]
    args = [z, z, z]
    if not ctx:
        in_specs += [pl.BlockSpec((1, 1, past, HEAD_DIM), lambda b, h: (b, h, 0, 0)),
                     pl.BlockSpec((1, 1, past, HEAD_DIM), lambda b, h: (b, h, 0, 0)),
                     pl.BlockSpec((seq, HEAD_DIM), lambda b, h: (0, 0)),
                     pl.BlockSpec((seq, HEAD_DIM), lambda b, h: (0, 0))]
        args += [kpast, vpast, cos, sin]
    in_specs += [pl.BlockSpec((1, HEAD_DIM), lambda b, h: (0, 0)),
                 pl.BlockSpec((1, HEAD_DIM), lambda b, h: (0, 0))]
    args += [qg.reshape(1, HEAD_DIM), kg.reshape(1, HEAD_DIM)]
    out_specs = [pl.BlockSpec((seq, qw), lambda b, h: (b, h))]
    out_shape = [jax.ShapeDtypeStruct((nb * seq, MIX_W), BF16)]
    if ctx:
        out_specs += [pl.BlockSpec((1, 1, seq, HEAD_DIM), lambda b, h: (b, h, 0, 0))] * 2
        out_shape += [jax.ShapeDtypeStruct((nb, ATT_KV_HEADS, seq, HEAD_DIM), F32)] * 2
    return pl.pallas_call(
        functools.partial(_attn_kernel, seq=seq, past=past, rope=not ctx),
        grid=(nb, ATT_KV_HEADS),
        in_specs=in_specs, out_specs=out_specs, out_shape=out_shape,
        scratch_shapes=[pltpu.VMEM((seq + past, HEAD_DIM), BF16),
                        pltpu.VMEM((seq + past, HEAD_DIM), BF16)],
        compiler_params=_cparams(("arbitrary", "arbitrary")),
        name="attn_ctx" if ctx else "attn_lat",
    )(*args)


def _s5_prep_kernel(are_ref, aim_ref, ldt_ref, bre_ref, bim_ref, pre_ref, pim_ref, wre_ref, wim_ref):
    a_re = are_ref[...]
    a_im = aim_ref[...]
    dt = jnp.exp(ldt_ref[...])
    pows = []
    for tau in range(S5_LAGS + 1):
        mag = jnp.exp((tau * dt) * a_re)
        ang = (tau * dt) * a_im
        pr, pi = mag * jnp.cos(ang), mag * jnp.sin(ang)
        pre_ref[tau] = pr
        pim_ref[tau] = pi
        pows.append((pr, pi))
    nr, ni = pows[1][0] - 1.0, pows[1][1]
    den = a_re * a_re + a_im * a_im
    cr = (nr * a_re + ni * a_im) / den
    ci = (ni * a_re - nr * a_im) / den
    for d in range(2):
        b_r, b_i = bre_ref[d], bim_ref[d]
        bb_r = cr[d:d + 1] * b_r - ci[d:d + 1] * b_i
        bb_i = cr[d:d + 1] * b_i + ci[d:d + 1] * b_r
        for tau in range(S5_LAGS):
            pr, pi = pows[tau][0][d:d + 1], pows[tau][1][d:d + 1]
            wre_ref[d, tau] = pr * bb_r - pi * bb_i
            wim_ref[d, tau] = pr * bb_i + pi * bb_r


def _s5_prep(a_re, a_im, log_dt, b_re, b_im, c_re, c_im):
    gp = S5_GROUPS * S5_STATE
    ldt = jnp.broadcast_to(log_dt[:, :, None], (2, S5_GROUPS, S5_STATE)).reshape(2, gp)
    bt = lambda b: b.transpose(0, 3, 1, 2).reshape(2, S5_CH, gp)
    pre, pim, wre, wim = pl.pallas_call(
        _s5_prep_kernel,
        out_shape=[jax.ShapeDtypeStruct((S5_LAGS + 1, 2, gp), F32)] * 2
        + [jax.ShapeDtypeStruct((2, S5_LAGS, S5_CH, gp), F32)] * 2,
        name="s5_prep",
    )(a_re.reshape(2, gp), a_im.reshape(2, gp), ldt, bt(b_re), bt(b_im))
    eye = jnp.eye(8, dtype=F32)

    hw = S5_SGW // 2
    half_mask = (jnp.arange(hw)[None, :] // S5_STATE == jnp.arange(4)[:, None]).astype(F32)

    def w_layout(w):
        w = w.reshape(2, S5_LAGS // 2, 2, S5_CH, S5_SG, 2, hw).transpose(0, 4, 5, 1, 2, 3, 6)
        w = w[:, :, :, :, :, None, :, :] * half_mask[:, None, :]
        return w.reshape(2, S5_SG, 2, S5_LAGS * LANE // 2, hw)

    w_in = jnp.concatenate([w_layout(wre), w_layout(wim)], axis=-1).astype(BF16)

    def c_layout(c):
        c = c.reshape(2, S5_SG, 8, S5_CH, S5_STATE).transpose(0, 1, 2, 4, 3)
        c = c[:, :, :, :, None, :] * eye[None, None, :, None, :, None]
        return c.reshape(2, S5_SG, S5_SGW, LANE)

    w_out = jnp.concatenate([c_layout(c_re), -c_layout(c_im)], axis=2).astype(BF16)

    def p_layout(p):
        p = p[1:].reshape(S5_LAGS, 2, S5_SG, S5_SGW).transpose(1, 2, 0, 3)
        return jnp.stack([p[0], p[1, :, ::-1]], axis=0)

    def a8_layout(p):
        p = p[S5_LAGS].reshape(2, S5_SG, 1, S5_SGW)
        return jnp.broadcast_to(p, (2, S5_SG, SUBLANE, S5_SGW))

    return w_in, w_out, a8_layout(pre), a8_layout(pim), p_layout(pre), p_layout(pim)


def _s5_kernel(u_ref, w_ref, c_ref, a8r_ref, a8i_ref, pwr_ref, pwi_ref, h0r_ref, h0i_ref,
               y_ref, hfr_ref, hfi_ref, upad, wbr, wbi, *, seq, tc):
    d = pl.program_id(0)
    zeros = jnp.zeros((SUBLANE, LANE), F32)
    upad[0:SUBLANE, :] = zeros
    upad[SUBLANE:seq + SUBLANE, :] = u_ref[...]
    upad[seq + SUBLANE:seq + 2 * SUBLANE, :] = zeros
    nch = seq // tc
    nt = tc // SUBLANE
    a8r = a8r_ref[0, 0]
    a8i = a8i_ref[0, 0]

    def run(fwd):
        h0r = h0r_ref[0, 0, 0]
        h0i = h0i_ref[0, 0, 0]
        pwr = pwr_ref[0, 0]
        pwi = pwi_ref[0, 0]
        hr = jnp.zeros((SUBLANE, S5_SGW), F32)
        hi = jnp.zeros((SUBLANE, S5_SGW), F32)
        for ci in range(nch):
            c = ci if fwd else nch - 1 - ci
            if fwd:
                win = upad[c * tc:c * tc + tc + SUBLANE, :]
                lags = [pltpu.roll(win, tau, 0)[SUBLANE:SUBLANE + tc] if tau else win[SUBLANE:SUBLANE + tc]
                        for tau in range(S5_LAGS)]
            else:
                win = upad[c * tc + SUBLANE:c * tc + tc + 2 * SUBLANE, :]
                lags = [pltpu.roll(win, tc + SUBLANE - tau, 0)[0:tc] if tau else win[0:tc]
                        for tau in range(S5_LAGS)]
            low = lax.broadcasted_iota(jnp.int32, (tc, LANE), 1) < LANE // 2
            slabs = ([], [])
            for k in range(S5_LAGS // 2):
                a, b = lags[2 * k], lags[2 * k + 1]
                slabs[0].append(jnp.where(low, a, pltpu.roll(b, LANE // 2, 1)).astype(BF16))
                slabs[1].append(jnp.where(low, pltpu.roll(a, LANE // 2, 1), b).astype(BF16))
            hw = S5_SGW // 2
            for half in range(2):
                w = jnp.dot(jnp.concatenate(slabs[half], axis=1), w_ref[0, 0, half],
                            preferred_element_type=F32)
                wbr[:, half * hw:(half + 1) * hw] = w[:, :hw]
                wbi[:, half * hw:(half + 1) * hw] = w[:, hw:]
            if ci == 0:
                r0 = 0 if fwd else tc - SUBLANE
                wbr[r0:r0 + SUBLANE, :] = wbr[r0:r0 + SUBLANE, :] + (pwr * h0r - pwi * h0i)
                wbi[r0:r0 + SUBLANE, :] = wbi[r0:r0 + SUBLANE, :] + (pwr * h0i + pwi * h0r)

            def step(i, carry):
                cr, ci_ = carry
                t = i if fwd else nt - 1 - i
                rows = pl.ds(pl.multiple_of(t * SUBLANE, SUBLANE), SUBLANE)
                nr = a8r * cr - a8i * ci_ + wbr[rows, :]
                ni = a8r * ci_ + a8i * cr + wbi[rows, :]
                wbr[rows, :] = nr
                wbi[rows, :] = ni
                return nr, ni

            hr, hi = lax.fori_loop(0, nt, step, (hr, hi))
            hcat = jnp.concatenate([wbr[...].astype(BF16), wbi[...].astype(BF16)], axis=1)
            y_ref[0, c * tc:(c + 1) * tc, :] = jnp.dot(hcat, c_ref[0, 0], preferred_element_type=F32)
        last = SUBLANE - 1 if fwd else 0
        hfr_ref[0, 0, 0] = hr[last:last + 1]
        hfi_ref[0, 0, 0] = hi[last:last + 1]

    @pl.when(d == 0)
    def _():
        run(True)

    @pl.when(d == 1)
    def _():
        run(False)


def _s5_scan(z, prep, h0r, h0i, *, ctx):
    w_in, w_out, a8r, a8i, pwr, pwi = prep
    seq = SEQ if ctx else DEC_SEQ
    nb = BATCH if ctx else DEC_BATCH
    row0 = 0 if ctx else T_CTX // DEC_SEQ
    tc = min(seq, 512)
    par = lambda d, s, b: (d, s, 0, 0)
    st = lambda d, s, b: (b, d, s, 0, 0)
    return pl.pallas_call(
        functools.partial(_s5_kernel, seq=seq, tc=tc),
        grid=(2, S5_SG, nb),
        in_specs=[pl.BlockSpec((seq, LANE), lambda d, s, b: (row0 + b, Z_SX // LANE + s)),
                  pl.BlockSpec((1, 1, 2, S5_LAGS * LANE // 2, S5_SGW), lambda d, s, b: (d, s, 0, 0, 0)),
                  pl.BlockSpec((1, 1, 2 * S5_SGW, LANE), par),
                  pl.BlockSpec((1, 1, SUBLANE, S5_SGW), par),
                  pl.BlockSpec((1, 1, SUBLANE, S5_SGW), par),
                  pl.BlockSpec((1, 1, SUBLANE, S5_SGW), par),
                  pl.BlockSpec((1, 1, SUBLANE, S5_SGW), par),
                  pl.BlockSpec((1, 1, 1, 1, S5_SGW), st),
                  pl.BlockSpec((1, 1, 1, 1, S5_SGW), st)],
        out_specs=[pl.BlockSpec((1, seq, LANE), lambda d, s, b: (d, b, s)),
                   pl.BlockSpec((1, 1, 1, 1, S5_SGW), st),
                   pl.BlockSpec((1, 1, 1, 1, S5_SGW), st)],
        out_shape=[jax.ShapeDtypeStruct((2, nb * seq, MIX_W), F32),
                   jax.ShapeDtypeStruct((nb, 2, S5_SG, 1, S5_SGW), F32),
                   jax.ShapeDtypeStruct((nb, 2, S5_SG, 1, S5_SGW), F32)],
        scratch_shapes=[pltpu.VMEM((seq + 2 * SUBLANE, LANE), F32),
                        pltpu.VMEM((tc, S5_SGW), F32),
                        pltpu.VMEM((tc, S5_SGW), F32)],
        compiler_params=_cparams(("arbitrary", "arbitrary", "arbitrary")),
        name="s5_ctx" if ctx else "s5_lat",
    )(z, w_in, w_out, a8r, a8i, pwr, pwi, h0r, h0i)


S5_POST_TM = 512


def _s5_post_kernel(yc_ref, yl_ref, sx_ref, d_ref, w_ref, b_ref, o_ref):
    y2 = _pick(S5_POST_TM, yc_ref, yl_ref)
    y = y2[0] + y2[1] + d_ref[...] * sx_ref[...]
    y = jax.nn.gelu(y)
    gate = jnp.dot(y.astype(BF16), w_ref[...], preferred_element_type=F32) + b_ref[...]
    o_ref[...] = (y * jax.nn.sigmoid(gate)).astype(BF16)


def _s5_post(y_pair, z, d_skip, w_glu, b_glu):
    tm = S5_POST_TM
    return pl.pallas_call(
        _s5_post_kernel,
        grid=(T_ALL // tm,),
        in_specs=_pair_specs(tm, (2, tm, MIX_W), rows_dim=1) + [
                  pl.BlockSpec((tm, MIX_W), lambda i: (i, Z_SX // MIX_W)),
                  pl.BlockSpec((1, MIX_W), lambda i: (0, 0)),
                  pl.BlockSpec((MIX_W, MIX_W), lambda i: (0, 0)),
                  pl.BlockSpec((1, MIX_W), lambda i: (0, 0))],
        out_specs=pl.BlockSpec((tm, MIX_W), lambda i: (i, 0)),
        out_shape=jax.ShapeDtypeStruct((T_ALL, MIX_W), BF16),
        compiler_params=_cparams(("arbitrary",)),
        name="s5_post",
    )(*y_pair, z, d_skip.reshape(1, MIX_W), w_glu.astype(BF16), b_glu.reshape(1, MIX_W))


def _mlstm_kernel(q_ref, k_ref, v_ref, mo_ref, gc_ref, gr_ref, bc_ref, br_ref, c0_ref, n0_ref, m0_ref,
                  ng_ref, h_ref, c_ref, n_ref, m_ref, hs_scr, *, seq, chunk0):
    nc = seq // CHUNK
    ii = lax.broadcasted_iota(jnp.int32, (CHUNK, CHUNK), 0)
    jj = lax.broadcasted_iota(jnp.int32, (CHUNK, CHUNK), 1)
    neg_inf = jnp.float32(-jnp.inf)

    def chunk_step(hh, d, cidx, carry):
        cols = slice(hh * MLSTM_DIM, (hh + 1) * MLSTM_DIM)
        fwd = d == 0
        mask = (jj <= ii) if fwd else (jj >= ii)
        mask_t = (ii <= jj) if fwd else (ii >= jj)
        last = CHUNK - 1 if fwd else 0
        c_st, n_st, m_st = carry
        rows = pl.ds(pl.multiple_of(cidx * CHUNK, CHUNK), CHUNK)
        q = q_ref[rows, cols]
        k = k_ref[rows, cols] * (MLSTM_DIM ** -0.5)
        v = v_ref[rows, cols]
        gcol = gc_ref[hh, rows, :] + bc_ref[hh]
        grow = gr_ref[hh, chunk0 + cidx] + br_ref[hh]
        li_c = gcol[:, 2 * d:2 * d + 1]
        lf_c = jax.nn.log_sigmoid(gcol[:, 2 * d + 1:2 * d + 2])
        li_r = grow[2 * d:2 * d + 1, :]
        lf_r = jax.nn.log_sigmoid(grow[2 * d + 1:2 * d + 2, :])
        b_c = jnp.sum(jnp.where(mask, lf_r, 0.0), axis=1, keepdims=True)
        b_r = jnp.sum(jnp.where(mask_t, lf_c, 0.0), axis=0, keepdims=True)
        dmat = jnp.where(mask, b_c - b_r + li_r, neg_inf)
        inter = b_c + m_st
        m_j = jnp.maximum(inter, jnp.max(dmat, axis=1, keepdims=True))
        qb = q.astype(BF16)
        vb = v.astype(BF16)
        s = lax.dot_general(qb, k.astype(BF16), (((1,), (1,)), ((), ())),
                            preferred_element_type=F32) * jnp.exp(dmat - m_j)
        w_inter = jnp.exp(inter - m_j)
        num = (jnp.dot(s.astype(BF16), vb, preferred_element_type=F32)
               + w_inter * jnp.dot(qb, c_st.astype(BF16), preferred_element_type=F32))
        den = jnp.sum(s, axis=1, keepdims=True) + w_inter * jnp.sum(q * n_st, axis=1, keepdims=True)
        hs_scr[d, rows, cols] = num / jnp.maximum(jnp.abs(den), jnp.exp(-m_j))
        m_end = m_j[last:last + 1, :]
        b_end = b_c[last:last + 1, :]
        w_c = jnp.exp(b_end - b_c + li_c - m_end)
        decay = jnp.exp(b_end + m_st - m_end)
        kw = k * w_c
        c_new = decay * c_st + lax.dot_general(kw.astype(BF16), vb, (((0,), (0,)), ((), ())),
                                               preferred_element_type=F32)
        n_new = decay * n_st + jnp.sum(kw, axis=0, keepdims=True)
        return c_new, n_new, m_end

    chains = [(hh, d) for hh in range(MLSTM_HG) for d in range(2)]

    def all_chains(ci, carry):
        return tuple(chunk_step(hh, d, ci if d == 0 else nc - 1 - ci, carry[n])
                     for n, (hh, d) in enumerate(chains))

    init = tuple((c0_ref[0, d, hh], n0_ref[0, d, hh], m0_ref[0, d, hh]) for hh, d in chains)
    final = lax.fori_loop(0, nc, all_chains, init)
    for n, (hh, d) in enumerate(chains):
        c_ref[0, d, hh], n_ref[0, d, hh], m_ref[0, d, hh] = final[n]
    for hh in range(MLSTM_HG):
        cols = slice(hh * MLSTM_DIM, (hh + 1) * MLSTM_DIM)
        h = _rms(hs_scr[0, :, cols] + hs_scr[1, :, cols], ng_ref[hh])
        h_ref[:, cols] = (h * jax.nn.sigmoid(mo_ref[:, cols])).astype(BF16)


def _mlstm(z, gcol, grow, bias_c, bias_r, c0, n0, m0, norm_g, *, ctx):
    seq = SEQ if ctx else DEC_SEQ
    nb = BATCH if ctx else DEC_BATCH
    row0 = 0 if ctx else T_CTX // DEC_SEQ
    hd = MLSTM_DIM
    nh = MLSTM_HEADS
    hg = MLSTM_HG
    hw = hg * hd
    zcol = lambda off: (lambda b, h: (row0 + b, off // hw + h))
    st5 = lambda b, h: (b, 0, h, 0, 0)
    return pl.pallas_call(
        functools.partial(_mlstm_kernel, seq=seq, chunk0=0),
        grid=(nb, nh // hg),
        in_specs=[pl.BlockSpec((seq, hw), zcol(Z_MQ)),
                  pl.BlockSpec((seq, hw), zcol(Z_MK)),
                  pl.BlockSpec((seq, hw), zcol(Z_MV)),
                  pl.BlockSpec((seq, hw), zcol(Z_MO)),
                  pl.BlockSpec((hg, seq, 4), lambda b, h: (h, row0 + b, 0)),
                  pl.BlockSpec((hg, seq // CHUNK, 4, CHUNK), lambda b, h: (h, row0 + b, 0, 0)),
                  pl.BlockSpec((hg, 1, 4), lambda b, h: (h, 0, 0)),
                  pl.BlockSpec((hg, 4, 1), lambda b, h: (h, 0, 0)),
                  pl.BlockSpec((1, 2, hg, hd, hd), st5),
                  pl.BlockSpec((1, 2, hg, 1, hd), st5),
                  pl.BlockSpec((1, 2, hg, 1, 1), st5),
                  pl.BlockSpec((hg, 1, hd), lambda b, h: (h, 0, 0))],
        out_specs=[pl.BlockSpec((seq, hw), lambda b, h: (b, h)),
                   pl.BlockSpec((1, 2, hg, hd, hd), st5),
                   pl.BlockSpec((1, 2, hg, 1, hd), st5),
                   pl.BlockSpec((1, 2, hg, 1, 1), st5)],
        out_shape=[jax.ShapeDtypeStruct((nb * seq, MIX_W), BF16),
                   jax.ShapeDtypeStruct((nb, 2, nh, hd, hd), F32),
                   jax.ShapeDtypeStruct((nb, 2, nh, 1, hd), F32),
                   jax.ShapeDtypeStruct((nb, 2, nh, 1, 1), F32)],
        scratch_shapes=[pltpu.VMEM((2, seq, hw), F32)],
        compiler_params=_cparams(("arbitrary", "arbitrary")),
        name="mlstm_ctx" if ctx else "mlstm_lat",
    )(z, z, z, z, gcol, grow, bias_c, bias_r, c0, n0, m0, norm_g.reshape(nh, 1, hd))


GMLP_TM = 512


def _gmlp_kernel(gu_ref, gv_ref, ng_ref, ws_ref, bs_ref, o_ref):
    vn = _rms(gv_ref[...], ng_ref[...]).astype(BF16)
    gw = MIX_W // GMLP_GROUPS
    for c in range(GMLP_TM // CHUNK):
        r = slice(c * CHUNK, (c + 1) * CHUNK)
        for g in range(GMLP_GROUPS):
            cs = slice(g * gw, (g + 1) * gw)
            mixed = jnp.dot(ws_ref[g], vn[r, cs], preferred_element_type=F32) + bs_ref[:, g:g + 1]
            o_ref[r, cs] = (gu_ref[r, cs] * mixed).astype(BF16)


def _gmlp(z, norm_g, w_s, b_s):
    tm = GMLP_TM
    return pl.pallas_call(
        _gmlp_kernel,
        grid=(T_ALL // tm,),
        in_specs=[pl.BlockSpec((tm, MIX_W), lambda i: (i, Z_GU // MIX_W)),
                  pl.BlockSpec((tm, MIX_W), lambda i: (i, Z_GV // MIX_W)),
                  pl.BlockSpec((1, MIX_W), lambda i: (0, 0)),
                  pl.BlockSpec((GMLP_GROUPS, CHUNK, CHUNK), lambda i: (0, 0, 0)),
                  pl.BlockSpec((CHUNK, GMLP_GROUPS), lambda i: (0, 0))],
        out_specs=pl.BlockSpec((tm, MIX_W), lambda i: (i, 0)),
        out_shape=jax.ShapeDtypeStruct((T_ALL, MIX_W), BF16),
        compiler_params=_cparams(("arbitrary",)),
        name="gmlp",
    )(z, z, norm_g.reshape(1, MIX_W), w_s.astype(BF16), b_s.T)


OUT_TM = 256


def _out_kernel(*refs, router):
    (ac_ref, al_ref, b_ref, cc_ref, cl_ref, d_ref, xc_ref, xl_ref, mod_ref, g_ref, w_ref), refs = refs[:11], refs[11:]
    if router:
        rh_ref, rl_ref, rb_ref, x1_ref, h2_ref, rt_ref = refs
    else:
        x1_ref, h2_ref = refs
    mix = jnp.dot(_pick(OUT_TM, ac_ref, al_ref), w_ref[0:MIX_W, :], preferred_element_type=F32)
    mix += jnp.dot(b_ref[...], w_ref[MIX_W:2 * MIX_W, :], preferred_element_type=F32)
    mix += jnp.dot(_pick(OUT_TM, cc_ref, cl_ref), w_ref[2 * MIX_W:3 * MIX_W, :], preferred_element_type=F32)
    mix += jnp.dot(d_ref[...], w_ref[3 * MIX_W:4 * MIX_W, :], preferred_element_type=F32)
    x1 = _pick(OUT_TM, xc_ref, xl_ref) + mod_ref[0, 2:3, :] * mix
    x1_ref[...] = x1
    h2 = _rms(x1, g_ref[...]) * (1.0 + mod_ref[0, 4:5, :]) + mod_ref[0, 3:4, :]
    hi = h2.astype(BF16)
    h2_ref[...] = h2 if router else hi
    if router:
        lo = (h2 - hi.astype(F32)).astype(BF16)
        logits = (jnp.dot(hi, rh_ref[...], preferred_element_type=F32)
                  + jnp.dot(lo, rh_ref[...], preferred_element_type=F32)
                  + jnp.dot(hi, rl_ref[...], preferred_element_type=F32)) + rb_ref[...]
        lane = lax.broadcasted_iota(jnp.int32, logits.shape, 1)
        neg_inf = jnp.float32(-jnp.inf)
        lg = jnp.where(lane < N_EXPERTS, logits, neg_inf)
        m1 = jnp.max(lg, axis=-1, keepdims=True)
        i1 = jnp.min(jnp.where(lg == m1, lane, LANE), axis=-1, keepdims=True)
        lg2 = jnp.where(lane == i1, neg_inf, lg)
        m2 = jnp.max(lg2, axis=-1, keepdims=True)
        i2 = jnp.min(jnp.where(lg2 == m2, lane, LANE), axis=-1, keepdims=True)
        e = jnp.exp(m2 - m1)
        w1 = 1.0 / (1.0 + e)
        w2 = e / (1.0 + e)
        rt = jnp.where(lane == 0, i1.astype(F32), 0.0)
        rt = jnp.where(lane == 1, i2.astype(F32), rt)
        rt = jnp.where(lane == 2, w1, rt)
        rt = jnp.where(lane == 3, w2, rt)
        rt_ref[...] = rt


def _out_proj(att_pair, s5_out, ml_pair, gm_out, x_pair, mod_l, g2, w_out, router=None):
    tm = OUT_TM
    row = lambda i: (i, 0)
    fixed = lambda i: (0, 0)
    full = pl.BlockSpec((tm, MIX_W), row)
    pair = _pair_specs(tm, (tm, MIX_W))
    in_specs = pair + [full] + pair + [full] + _pair_specs(tm, (tm, D_MODEL)) + [
        pl.BlockSpec((1, MOD_CHUNKS, D_MODEL), lambda i: (_mod_group(i, tm), 0, 0)),
        pl.BlockSpec((1, D_MODEL), fixed),
        pl.BlockSpec((D_MODEL, D_MODEL), fixed, pipeline_mode=pl.Buffered(1))]
    args = [*att_pair, s5_out, *ml_pair, gm_out, *x_pair, mod_l, g2.reshape(1, D_MODEL), w_out]
    out_specs = [pl.BlockSpec((tm, D_MODEL), row), pl.BlockSpec((tm, D_MODEL), row)]
    out_shape = [jax.ShapeDtypeStruct((T_ALL, D_MODEL), F32),
                 jax.ShapeDtypeStruct((T_ALL, D_MODEL), BF16 if router is None else F32)]
    if router is not None:
        rw, rb = router
        rw = jnp.pad(rw, ((0, 0), (0, LANE - N_EXPERTS)))
        rh = rw.astype(BF16)
        rl = (rw - rh.astype(F32)).astype(BF16)
        in_specs += [pl.BlockSpec((D_MODEL, LANE), fixed)] * 2 + [pl.BlockSpec((1, LANE), fixed)]
        args += [rh, rl, jnp.pad(rb, (0, LANE - N_EXPERTS)).reshape(1, LANE)]
        out_specs.append(pl.BlockSpec((tm, LANE), row))
        out_shape.append(jax.ShapeDtypeStruct((T_ALL, LANE), F32))
    return pl.pallas_call(
        functools.partial(_out_kernel, router=router is not None),
        grid=(T_ALL // tm,),
        in_specs=in_specs, out_specs=out_specs, out_shape=out_shape,
        compiler_params=_cparams(("arbitrary",)),
        name="out_proj_router" if router is not None else "out_proj",
    )(*args)


def _ffn_kernel(x_ref, wg_ref, wu_ref, wd_ref, o_ref):
    @pl.when(pl.program_id(1) == 0)
    def _():
        o_ref[...] = jnp.zeros_like(o_ref)

    def sub(s, carry):
        rows = pl.ds(pl.multiple_of(s * FFN_DENSE_SUB, FFN_DENSE_SUB), FFN_DENSE_SUB)
        xs = x_ref[rows, :]
        g = jnp.dot(xs, wg_ref[...], preferred_element_type=F32)
        u = jnp.dot(xs, wu_ref[...], preferred_element_type=F32)
        a = (g * jax.nn.sigmoid(g) * u).astype(BF16)
        o_ref[rows, :] += jnp.dot(a, wd_ref[...], preferred_element_type=F32)
        return carry

    lax.fori_loop(0, FFN_TM // FFN_DENSE_SUB, sub, 0)


def _ffn(x, w_gate, w_up, w_down):
    return pl.pallas_call(
        _ffn_kernel,
        grid=(T_ALL // FFN_TM, D_FF // FFN_TF),
        in_specs=[pl.BlockSpec((FFN_TM, D_MODEL), lambda i, f: (i, 0)),
                  pl.BlockSpec((D_MODEL, FFN_TF), lambda i, f: (0, f)),
                  pl.BlockSpec((D_MODEL, FFN_TF), lambda i, f: (0, f)),
                  pl.BlockSpec((FFN_TF, D_MODEL), lambda i, f: (f, 0))],
        out_specs=pl.BlockSpec((FFN_TM, D_MODEL), lambda i, f: (i, 0)),
        out_shape=jax.ShapeDtypeStruct((T_ALL, D_MODEL), F32),
        compiler_params=_cparams(("arbitrary", "arbitrary")),
        name="ffn",
    )(x, w_gate, w_up, w_down)


def _moe_ffn_kernel(te_ref, ns_ref, src_ref, x_hbm, wg_ref, wu_ref, wd_ref, o_ref, xf, xb, sems):
    i = pl.program_id(0)
    f = pl.program_id(1)
    ntiles = pl.num_programs(0)

    def row_copy(src_row, slot, r):
        return pltpu.make_async_copy(x_hbm.at[pl.ds(src_row, 1)], xf.at[slot, pl.ds(r, 1)], sems.at[slot])

    def issue(tile, slot):
        def body(r8, carry):
            for u in range(ISSUE_UNROLL):
                r = r8 * ISSUE_UNROLL + u
                row_copy(src_ref[tile * FFN_TM + r], slot, r).start()
            return carry

        lax.fori_loop(0, ns_ref[tile] * (FFN_SUB // ISSUE_UNROLL), body, 0)

    def wait_rows(tile, slot):
        def body(s, carry):
            rows = pl.ds(pl.multiple_of(s * FFN_SUB, FFN_SUB), FFN_SUB)
            pltpu.make_async_copy(x_hbm.at[pl.ds(0, FFN_SUB)], xf.at[slot, rows], sems.at[slot]).wait()
            return carry

        lax.fori_loop(0, ns_ref[tile], body, 0)

    @pl.when(f == 0)
    def _():
        slot = i % 2

        @pl.when(i == 0)
        def _():
            issue(0, 0)

        wait_rows(i, slot)

        @pl.when(i + 1 < ntiles)
        def _():
            issue(i + 1, 1 - slot)

        def cast(s, carry):
            rows = pl.ds(pl.multiple_of(s * FFN_SUB, FFN_SUB), FFN_SUB)
            xb[rows, :] = xf[slot, rows, :].astype(BF16)
            return carry

        lax.fori_loop(0, ns_ref[i], cast, 0)
        o_ref[...] = jnp.zeros_like(o_ref)

    def block(row0, size):
        rows = pl.ds(pl.multiple_of(row0, size), size)
        xs = xb[rows, :]
        g = jnp.dot(xs, wg_ref[0], preferred_element_type=F32)
        u = jnp.dot(xs, wu_ref[0], preferred_element_type=F32)
        a = (g * jax.nn.sigmoid(g) * u).astype(BF16)
        o_ref[rows, :] += jnp.dot(a, wd_ref[0], preferred_element_type=F32)

    ns = ns_ref[i]
    pairs = ns // 2

    def pair(s, carry):
        block(s * (2 * FFN_SUB), 2 * FFN_SUB)
        return carry

    lax.fori_loop(0, pairs, pair, 0)

    @pl.when(ns % 2 == 1)
    def _():
        block(pairs * (2 * FFN_SUB), FFN_SUB)


def _moe_ffn(h2, tile_e, tile_ns, src, w_gate, w_up, w_down):
    tf = FFN_TF
    nf = D_FF // tf

    def fcol(i, f, ns):
        return jnp.where(ns[i] > 0, f, nf - 1)

    return pl.pallas_call(
        _moe_ffn_kernel,
        grid_spec=pltpu.PrefetchScalarGridSpec(
            num_scalar_prefetch=3,
            grid=(MOE_TILES, nf),
            in_specs=[pl.BlockSpec(memory_space=pl.ANY),
                      pl.BlockSpec((1, D_MODEL, tf), lambda i, f, te, ns, sr: (te[i], 0, fcol(i, f, ns))),
                      pl.BlockSpec((1, D_MODEL, tf), lambda i, f, te, ns, sr: (te[i], 0, fcol(i, f, ns))),
                      pl.BlockSpec((1, tf, D_MODEL), lambda i, f, te, ns, sr: (te[i], fcol(i, f, ns), 0))],
            out_specs=pl.BlockSpec((FFN_TM, D_MODEL), lambda i, f, te, ns, sr: (i, 0)),
            scratch_shapes=[pltpu.VMEM((2, FFN_TM, D_MODEL), F32),
                            pltpu.VMEM((FFN_TM, D_MODEL), BF16),
                            pltpu.SemaphoreType.DMA((2,))]),
        out_shape=jax.ShapeDtypeStruct((MOE_TILES * FFN_TM, D_MODEL), F32),
        compiler_params=_cparams(("arbitrary", "arbitrary")),
        name="moe_ffn",
    )(tile_e, tile_ns, src, h2, w_gate, w_up, w_down)


RES_TM = 256


def _res_kernel(x_ref, y_ref, mod_ref, g_ref, oc_ref, ol_ref, *, final):
    x2 = x_ref[...] + mod_ref[0, 5:6, :] * y_ref[...]
    _store_pair(RES_TM, _rms(x2, g_ref[...]) if final else x2, oc_ref, ol_ref)


def _moe_res_kernel(slot_ref, x_ref, rt_ref, mod_ref, g_ref, ys_hbm, oc_ref, ol_ref, ybuf, sems, *, final):
    i = pl.program_id(0)
    nt = pl.num_programs(0)

    def row_copy(src_row, buf, k, r):
        return pltpu.make_async_copy(ys_hbm.at[pl.ds(src_row, 1)], ybuf.at[buf, k, pl.ds(r, 1)], sems.at[buf])

    def issue(tile, buf):
        def body(r8, carry):
            for u in range(ISSUE_UNROLL):
                r = r8 * ISSUE_UNROLL + u
                for k in range(2):
                    row_copy(slot_ref[k * T_ALL + tile * RES_TM + r], buf, k, r).start()
            return carry

        lax.fori_loop(0, RES_TM // ISSUE_UNROLL, body, 0)

    buf = i % 2

    @pl.when(i == 0)
    def _():
        issue(0, 0)

    for k in range(2):
        pltpu.make_async_copy(ys_hbm.at[pl.ds(0, RES_TM)], ybuf.at[buf, k], sems.at[buf]).wait()

    @pl.when(i + 1 < nt)
    def _():
        issue(i + 1, 1 - buf)

    rt = rt_ref[...]
    ffn = rt[:, 2:3] * ybuf[buf, 0] + rt[:, 3:4] * ybuf[buf, 1]
    x2 = x_ref[...] + mod_ref[0, 5:6, :] * ffn
    _store_pair(RES_TM, _rms(x2, g_ref[...]) if final else x2, oc_ref, ol_ref)


def _residual(x1, y, mod_l, final_g, *, route=None, slot=None, final):
    tm = RES_TM
    nt = T_ALL // tm
    out_shape = [jax.ShapeDtypeStruct((T_CTX, D_MODEL), F32), jax.ShapeDtypeStruct((T_LAT, D_MODEL), F32)]
    out_specs = _pair_specs(tm, (tm, D_MODEL))
    if route is None:
        row = lambda i: (i, 0)
        return pl.pallas_call(
            functools.partial(_res_kernel, final=final),
            grid=(nt,),
            in_specs=[pl.BlockSpec((tm, D_MODEL), row), pl.BlockSpec((tm, D_MODEL), row),
                      pl.BlockSpec((1, MOD_CHUNKS, D_MODEL), lambda i: (_mod_group(i, tm), 0, 0)),
                      pl.BlockSpec((1, D_MODEL), lambda i: (0, 0))],
            out_specs=out_specs,
            out_shape=out_shape,
            compiler_params=_cparams(("arbitrary",)),
            name="ffn_residual",
        )(x1, y, mod_l, final_g.reshape(1, D_MODEL))
    row = lambda i, sl: (i, 0)
    return pl.pallas_call(
        functools.partial(_moe_res_kernel, final=final),
        grid_spec=pltpu.PrefetchScalarGridSpec(
            num_scalar_prefetch=1,
            grid=(nt,),
            in_specs=[pl.BlockSpec((tm, D_MODEL), row), pl.BlockSpec((tm, LANE), row),
                      pl.BlockSpec((1, MOD_CHUNKS, D_MODEL), lambda i, sl: (_mod_group(i, tm), 0, 0)),
                      pl.BlockSpec((1, D_MODEL), lambda i, sl: (0, 0)),
                      pl.BlockSpec(memory_space=pl.ANY)],
            out_specs=out_specs,
            scratch_shapes=[pltpu.VMEM((2, 2, tm, D_MODEL), F32), pltpu.SemaphoreType.DMA((2,))]),
        out_shape=out_shape,
        compiler_params=_cparams(("arbitrary",)),
        name="moe_residual",
    )(slot, x1, route, mod_l, final_g.reshape(1, D_MODEL), y)


def _route_plan(route):
    e_flat = jnp.concatenate([route[:, 0], route[:, 1]]).astype(jnp.int32)
    onehot = (e_flat[:, None] == jnp.arange(N_EXPERTS, dtype=jnp.int32)[None, :]).astype(jnp.int32)
    ranks = jnp.cumsum(onehot, axis=0) - onehot
    rank = jnp.sum(ranks * onehot, axis=1)
    counts = jnp.sum(onehot, axis=0)
    tiles = (counts + FFN_TM - 1) // FFN_TM
    tile_start = jnp.cumsum(tiles) - tiles
    slot = tile_start[e_flat] * FFN_TM + rank
    n_slots = MOE_TILES * FFN_TM
    tok = jnp.concatenate([jnp.arange(T_ALL, dtype=jnp.int32)] * 2)
    src = jnp.zeros((n_slots,), jnp.int32).at[slot].set(tok)
    tile_ids = jnp.arange(MOE_TILES, dtype=jnp.int32)
    ends = jnp.cumsum(tiles)
    tile_e = jnp.sum((tile_ids[:, None] >= ends[None, :]).astype(jnp.int32), axis=1)
    used = tile_e < N_EXPERTS
    last_e = jnp.max(jnp.where(counts > 0, jnp.arange(N_EXPERTS, dtype=jnp.int32), 0))
    tile_e = jnp.where(used, tile_e, last_e)
    rows_in_tile = jnp.clip(counts[tile_e] - (tile_ids - tile_start[tile_e]) * FFN_TM, 0, FFN_TM)
    tile_ns = jnp.where(used, (rows_in_tile + FFN_SUB - 1) // FFN_SUB, 0).astype(jnp.int32)
    return src, slot, tile_e.astype(jnp.int32), tile_ns


def _rope_tables():
    length = DEC_SEQ
    r = jnp.repeat(jnp.arange(length // GRID_W, dtype=F32), GRID_W)
    col = (jnp.arange(length) % GRID_W).astype(F32)
    half = HEAD_DIM // 2
    inv = ROPE_THETA ** (-jnp.arange(0, half, 2, dtype=F32) / half)
    ar, ac = r[:, None] * inv, col[:, None] * inv
    cos = jnp.concatenate([jnp.cos(ar), jnp.cos(ar), jnp.cos(ac), jnp.cos(ac)], axis=-1)
    sin = jnp.concatenate([-jnp.sin(ar), jnp.sin(ar), -jnp.sin(ac), jnp.sin(ac)], axis=-1)
    return cos, sin


def kernel(x_prompt, x_sample, c, c_ctx, cache_attn_k, cache_attn_v, state_s5_re, state_s5_im, state_mlstm_c, state_mlstm_n, state_mlstm_m, norm1_g, norm2_g, w_mod, b_mod, w_in, w_out, q_norm_g, k_norm_g, s5_a_re, s5_a_im, s5_log_dt, s5_b_re, s5_b_im, s5_c_re, s5_c_im, s5_d, s5_w_glu, s5_b_glu, mlstm_i_bias, mlstm_f_bias, mlstm_norm_g, gmlp_norm_g, gmlp_w_s, gmlp_b_s, ffn_w_gate, ffn_w_up, ffn_w_down, moe_router, moe_router_bias, moe_w_gate, moe_w_up, moe_w_down, final_norm_g):
    x = (x_prompt.reshape(T_CTX, D_MODEL), x_sample.reshape(T_LAT, D_MODEL))
    cond = jnp.concatenate([c_ctx[None, :], c], axis=0)
    mod = _modulation(cond, w_mod, b_mod)
    cos, sin = _rope_tables()
    nh = MLSTM_HEADS
    zeros_s5 = jnp.zeros((BATCH, 2, S5_SG, 1, S5_SGW), F32)
    zeros_c = jnp.zeros((BATCH, 2, nh, MLSTM_DIM, MLSTM_DIM), F32)
    zeros_n = jnp.zeros((BATCH, 2, nh, 1, MLSTM_DIM), F32)
    zeros_m = jnp.zeros((BATCH, 2, nh, 1, 1), F32)

    ctx_states = []
    for l in range(DEPTH):
        use_moe = l % 2 == 1
        j = l // 2
        wl = w_in[l]
        w_p = jnp.concatenate([wl[:, :3584], wl[:, 3600:], wl[:, 3584:3600],
                               jnp.zeros((D_MODEL, Z_COLS - 4624), F32)], axis=1).astype(BF16)
        z, x = _in_proj(x, mod[l], norm1_g[l], w_p)

        att_c, k_new, v_new = _attention(z, None, q_norm_g[l], k_norm_g[l], ctx=True)
        (att_l,) = _attention(z, None, q_norm_g[l], k_norm_g[l], ctx=False,
                              kpast=cache_attn_k[:, l], vpast=cache_attn_v[:, l], cos=cos, sin=sin)

        prep = _s5_prep(s5_a_re[l], s5_a_im[l], s5_log_dt[l], s5_b_re[l], s5_b_im[l], s5_c_re[l], s5_c_im[l])
        y_c, hf_re, hf_im = _s5_scan(z, prep, zeros_s5, zeros_s5, ctx=True)
        st = lambda s: s[:, l].reshape(DEC_BATCH, 2, S5_SG, 1, S5_SGW)
        y_l, _, _ = _s5_scan(z, prep, st(state_s5_re), st(state_s5_im), ctx=False)
        s5_out = _s5_post((y_c, y_l), z, s5_d[l], s5_w_glu[l], s5_b_glu[l])

        mg = z[:, Z_MG:Z_MG + 16].reshape(T_ALL, 2, 2, nh)
        gcol = mg.transpose(3, 0, 1, 2).reshape(nh, T_ALL, 4)
        grow = mg.reshape(T_ALL // CHUNK, CHUNK, 4, nh).transpose(3, 0, 2, 1)
        bias = jnp.stack([mlstm_i_bias[l], mlstm_f_bias[l]], axis=1)
        bias_c = bias.transpose(2, 0, 1).reshape(nh, 1, 4)
        bias_r = bias_c.reshape(nh, 4, 1)
        ml_c_out, c_new, n_new, m_new = _mlstm(z, gcol, grow, bias_c, bias_r, zeros_c, zeros_n, zeros_m,
                                               mlstm_norm_g[l], ctx=True)
        ml_l_out, _, _, _ = _mlstm(z, gcol, grow, bias_c, bias_r, state_mlstm_c[:, l],
                                   state_mlstm_n[:, l].reshape(DEC_BATCH, 2, nh, 1, MLSTM_DIM),
                                   state_mlstm_m[:, l].reshape(DEC_BATCH, 2, nh, 1, 1),
                                   mlstm_norm_g[l], ctx=False)

        gm_out = _gmlp(z, gmlp_norm_g[l], gmlp_w_s[l], gmlp_b_s[l])

        ctx_states.append((k_new, v_new,
                           hf_re.reshape(BATCH, 2, S5_GROUPS, S5_STATE), hf_im.reshape(BATCH, 2, S5_GROUPS, S5_STATE),
                           c_new, n_new.reshape(BATCH, 2, nh, MLSTM_DIM), m_new.reshape(BATCH, 2, nh)))

        w_o = w_out[l].astype(BF16)
        final = l == DEPTH - 1
        mixed = ((att_c, att_l), s5_out, (ml_c_out, ml_l_out), gm_out)
        if use_moe:
            x1, h2, route = _out_proj(*mixed, x, mod[l], norm2_g[l], w_o,
                                      router=(moe_router[j], moe_router_bias[j]))
            src, slot, tile_e, tile_ns = _route_plan(route)
            ys = _moe_ffn(h2, tile_e, tile_ns, src, moe_w_gate[j].astype(BF16), moe_w_up[j].astype(BF16),
                          moe_w_down[j].astype(BF16))
            x = _residual(x1, ys, mod[l], final_norm_g, route=route, slot=slot, final=final)
        else:
            x1, h2 = _out_proj(*mixed, x, mod[l], norm2_g[l], w_o)
            y = _ffn(h2, ffn_w_gate[j].astype(BF16), ffn_w_up[j].astype(BF16), ffn_w_down[j].astype(BF16))
            if final:
                x = _residual(x1, y, mod[l], final_norm_g, final=True)
            else:
                x = (x1, y, mod[l])

    y_prompt = x[0].reshape(BATCH, SEQ, D_MODEL)
    y_sample = x[1].reshape(DEC_BATCH, DEC_SEQ, D_MODEL)
    stack = lambda i: jnp.stack([s[i] for s in ctx_states], axis=1)
    return (y_prompt, y_sample, stack(0), stack(1), stack(2), stack(3), stack(4), stack(5), stack(6))
```

```python
import functools
import math

import numpy as np
import jax
import jax.numpy as jnp
from jax import lax
from jax.experimental import pallas as pl
from jax.experimental.pallas import tpu as pltpu

F32 = jnp.float32
BF16 = jnp.bfloat16

D_MODEL = 2048
BATCH = 16
SEQ = 256
DEPTH = 2
DEC_BATCH = 2
DEC_SEQ = 2048
PAST_LEN = 256
GRID_W = 64
MIX_W = 512
ATT_HEADS = 4
ATT_KV_HEADS = 2
HEAD_DIM = 128
ROPE_THETA = 10000.0
S5_CH = 16
S5_GROUPS = 32
S5_STATE = 64
MLSTM_HEADS = 4
MLSTM_DIM = 128
MLSTM_HG = 1
CHUNK = 128
GMLP_GROUPS = 4
D_FF = 7168
N_EXPERTS = 8
MOD_CHUNKS = 6
EPS = 1e-6

T_CTX = BATCH * SEQ
T_LAT = DEC_BATCH * DEC_SEQ
T_ALL = T_CTX + T_LAT
N_GROUPS_MOD = 1 + DEC_BATCH

Z_AQ, Z_AK, Z_AV, Z_SX = 0, 512, 768, 1024
Z_MQ, Z_MK, Z_MV, Z_MO = 1536, 2048, 2560, 3072
Z_GU, Z_GV, Z_MG = 3584, 4096, 4608
Z_COLS = 4736
LANE = 128
SUBLANE = 8

VMEM_LIMIT = 56 * 1024 * 1024

S5_SG = 4
S5_SGW = 8 * S5_STATE
S5_LAGS = 8
S5_CTX_NSEQ = 1

FFN_TM = 1024
FFN_SUB = 256
FFN_DENSE_SUB = 512
FFN_TF = 512
MOE_TILES = 2 * T_ALL // FFN_TM + N_EXPERTS
ISSUE_UNROLL = 8


def _cparams(sem=None):
    return pltpu.CompilerParams(dimension_semantics=sem, vmem_limit_bytes=VMEM_LIMIT)


def _mod_group(i, tm):
    return jnp.maximum(i * tm // DEC_SEQ - (T_CTX // DEC_SEQ - 1), 0)


def _rms(x, g):
    return x * lax.rsqrt(jnp.mean(x * x, axis=-1, keepdims=True) + EPS) * g


MOD_TN = 1024


def _mod_kernel(cb_ref, w_ref, b_ref, o_ref, silu_scr):
    @pl.when((pl.program_id(0) == 0) & (pl.program_id(1) == 0))
    def _():
        c = cb_ref[...]
        silu_scr[...] = c * jax.nn.sigmoid(c)

    w = w_ref[0]
    rows = []
    for r in range(N_GROUPS_MOD):
        s = silu_scr[r]
        parts = [jnp.sum(w[:, j * LANE:(j + 1) * LANE] * s, axis=0, keepdims=True)
                 for j in range(MOD_TN // LANE)]
        rows.append(jnp.concatenate(parts, axis=1))
    rows.append(jnp.zeros((SUBLANE - N_GROUPS_MOD, MOD_TN), F32))
    o_ref[0] = jnp.concatenate(rows, axis=0) + b_ref[0]


def _modulation(cond, w_mod, b_mod):
    cb = jnp.broadcast_to(cond[:, :, None], (N_GROUPS_MOD, D_MODEL, LANE))
    n = MOD_CHUNKS * D_MODEL
    out = pl.pallas_call(
        _mod_kernel,
        grid=(DEPTH, n // MOD_TN),
        in_specs=[pl.BlockSpec((N_GROUPS_MOD, D_MODEL, LANE), lambda l, j: (0, 0, 0)),
                  pl.BlockSpec((1, D_MODEL, MOD_TN), lambda l, j: (l, 0, j)),
                  pl.BlockSpec((1, 1, MOD_TN), lambda l, j: (l, 0, j))],
        out_specs=pl.BlockSpec((1, SUBLANE, MOD_TN), lambda l, j: (l, 0, j)),
        out_shape=jax.ShapeDtypeStruct((DEPTH, SUBLANE, n), F32),
        scratch_shapes=[pltpu.VMEM((N_GROUPS_MOD, D_MODEL, LANE), F32)],
        compiler_params=_cparams(("arbitrary", "arbitrary")),
        name="adaln_mod",
    )(cb, w_mod, b_mod.reshape(DEPTH, 1, n))
    return out[:, :N_GROUPS_MOD].reshape(DEPTH, N_GROUPS_MOD, MOD_CHUNKS, D_MODEL)


IN_TM = 256
IN_CHUNK = 512


def _pair_specs(tm, block, rows_dim=0):
    nc = T_CTX // tm

    def index(row):
        return tuple(row if d == rows_dim else 0 for d in range(len(block)))

    return [pl.BlockSpec(block, lambda i, *_: index(jnp.minimum(i, nc - 1))),
            pl.BlockSpec(block, lambda i, *_: index(jnp.maximum(i - nc, 0)))]


def _pick(tm, ctx_ref, lat_ref):
    return jnp.where(pl.program_id(0) < T_CTX // tm, ctx_ref[...], lat_ref[...])


def _store_pair(tm, val, oc_ref, ol_ref):
    i = pl.program_id(0)

    @pl.when(i < T_CTX // tm)
    def _():
        oc_ref[...] = val

    @pl.when(i >= T_CTX // tm)
    def _():
        ol_ref[...] = val


def _in_kernel(*refs, pending):
    if pending:
        x1_ref, y_ref, modp_ref, mod_ref, g_ref, w_ref, z_ref, oc_ref, ol_ref = refs
        x = x1_ref[...] + modp_ref[0, 5:6, :] * y_ref[...]
        _store_pair(IN_TM, x, oc_ref, ol_ref)
    else:
        xc_ref, xl_ref, mod_ref, g_ref, w_ref, z_ref = refs
        x = _pick(IN_TM, xc_ref, xl_ref)
    shift = mod_ref[0, 0:1, :]
    scale = mod_ref[0, 1:2, :]
    h = (_rms(x, g_ref[...]) * (1.0 + scale) + shift).astype(BF16)
    for c0 in range(0, Z_COLS, IN_CHUNK):
        cw = min(IN_CHUNK, Z_COLS - c0)
        z_ref[:, c0:c0 + cw] = jnp.dot(h, w_ref[:, c0:c0 + cw], preferred_element_type=F32)


def _in_proj(x, mod_l, g, w_p):
    tm = IN_TM
    pending = len(x) == 3
    row = lambda i: (i, 0)
    mod_spec = pl.BlockSpec((1, MOD_CHUNKS, D_MODEL), lambda i: (_mod_group(i, tm), 0, 0))
    if pending:
        x_specs = [pl.BlockSpec((tm, D_MODEL), row), pl.BlockSpec((tm, D_MODEL), row), mod_spec]
    else:
        x_specs = _pair_specs(tm, (tm, D_MODEL))
    out_specs = [pl.BlockSpec((tm, Z_COLS), row)]
    out_shape = [jax.ShapeDtypeStruct((T_ALL, Z_COLS), F32)]
    if pending:
        out_specs += _pair_specs(tm, (tm, D_MODEL))
        out_shape += [jax.ShapeDtypeStruct((T_CTX, D_MODEL), F32), jax.ShapeDtypeStruct((T_LAT, D_MODEL), F32)]
    outs = pl.pallas_call(
        functools.partial(_in_kernel, pending=pending),
        grid=(T_ALL // tm,),
        in_specs=x_specs + [
                  mod_spec,
                  pl.BlockSpec((1, D_MODEL), lambda i: (0, 0)),
                  pl.BlockSpec((D_MODEL, Z_COLS), lambda i: (0, 0), pipeline_mode=pl.Buffered(1))],
        out_specs=out_specs,
        out_shape=out_shape,
        compiler_params=_cparams(("arbitrary",)),
        name="in_proj_res" if pending else "in_proj",
    )(*x, mod_l, g.reshape(1, D_MODEL), w_p)
    return (outs[0], (outs[1], outs[2])) if pending else (outs[0], x)


def _rope(t, c, s):
    lane = lax.broadcasted_iota(jnp.int32, t.shape, 1)
    first = (lane % (HEAD_DIM // 2)) < (HEAD_DIM // 4)
    swapped = jnp.where(first, pltpu.roll(t, HEAD_DIM - HEAD_DIM // 4, 1), pltpu.roll(t, HEAD_DIM // 4, 1))
    return t * c + swapped * s


def _attn_kernel(*refs, seq, past, rope):
    if rope:
        (aq_ref, ak_ref, av_ref, kp_ref, vp_ref, cos_ref, sin_ref, qg_ref, kg_ref,
         att_ref, kb_scr, vb_scr) = refs
    else:
        aq_ref, ak_ref, av_ref, qg_ref, kg_ref, att_ref, knew_ref, vnew_ref, kb_scr, vb_scr = refs
    kn = _rms(ak_ref[...], kg_ref[...])
    v = av_ref[...]
    if rope:
        kn = _rope(kn, cos_ref[...], sin_ref[...])
        kb_scr[seq:seq + past, :] = kp_ref[0, 0].astype(BF16)
        vb_scr[seq:seq + past, :] = vp_ref[0, 0].astype(BF16)
    else:
        knew_ref[0, 0] = kn
        vnew_ref[0, 0] = v
    kb_scr[0:seq, :] = kn.astype(BF16)
    vb_scr[0:seq, :] = v.astype(BF16)
    grp = ATT_HEADS // ATT_KV_HEADS

    def q_block(qb, carry):
        rows = pl.ds(pl.multiple_of(qb * CHUNK, CHUNK), CHUNK)
        qs = []
        for g in range(grp):
            q = _rms(aq_ref[rows, g * HEAD_DIM:(g + 1) * HEAD_DIM], qg_ref[...])
            if rope:
                q = _rope(q, cos_ref[rows, :], sin_ref[rows, :])
            qs.append(q)
        q2 = jnp.concatenate(qs, axis=0).astype(BF16)
        s = lax.dot_general(q2, kb_scr[...], (((1,), (1,)), ((), ())),
                            preferred_element_type=F32) * (HEAD_DIM ** -0.5)
        m = jnp.max(s, axis=-1, keepdims=True)
        p = jnp.exp(s - m)
        den = jnp.sum(p, axis=-1, keepdims=True)
        o = jnp.dot(p.astype(BF16), vb_scr[...], preferred_element_type=F32) / den
        for g in range(grp):
            att_ref[rows, g * HEAD_DIM:(g + 1) * HEAD_DIM] = o[g * CHUNK:(g + 1) * CHUNK].astype(BF16)
        return carry

    lax.fori_loop(0, seq // CHUNK, q_block, 0)


def _attention(z, att_out_shape, qg, kg, *, ctx, kpast=None, vpast=None, cos=None, sin=None):
    seq = SEQ if ctx else DEC_SEQ
    nb = BATCH if ctx else DEC_BATCH
    row0 = 0 if ctx else T_CTX // DEC_SEQ
    past = 0 if ctx else PAST_LEN
    qw = HEAD_DIM * (ATT_HEADS // ATT_KV_HEADS)
    in_specs = [pl.BlockSpec((seq, qw), lambda b, h: (row0 + b, Z_AQ // qw + h)),
                pl.BlockSpec((seq, HEAD_DIM), lambda b, h: (row0 + b, Z_AK // HEAD_DIM + h)),
                pl.BlockSpec((seq, HEAD_DIM), lambda b, h: (row0 + b, Z_AV // HEAD_DIM + h))]
    args = [z, z, z]
    if not ctx:
        in_specs += [pl.BlockSpec((1, 1, past, HEAD_DIM), lambda b, h: (b, h, 0, 0)),
                     pl.BlockSpec((1, 1, past, HEAD_DIM), lambda b, h: (b, h, 0, 0)),
                     pl.BlockSpec((seq, HEAD_DIM), lambda b, h: (0, 0)),
                     pl.BlockSpec((seq, HEAD_DIM), lambda b, h: (0, 0))]
        args += [kpast, vpast, cos, sin]
    in_specs += [pl.BlockSpec((1, HEAD_DIM), lambda b, h: (0, 0)),
                 pl.BlockSpec((1, HEAD_DIM), lambda b, h: (0, 0))]
    args += [qg.reshape(1, HEAD_DIM), kg.reshape(1, HEAD_DIM)]
    out_specs = [pl.BlockSpec((seq, qw), lambda b, h: (b, h))]
    out_shape = [jax.ShapeDtypeStruct((nb * seq, MIX_W), BF16)]
    if ctx:
        out_specs += [pl.BlockSpec((1, 1, seq, HEAD_DIM), lambda b, h: (b, h, 0, 0))] * 2
        out_shape += [jax.ShapeDtypeStruct((nb, ATT_KV_HEADS, seq, HEAD_DIM), F32)] * 2
    return pl.pallas_call(
        functools.partial(_attn_kernel, seq=seq, past=past, rope=not ctx),
        grid=(nb, ATT_KV_HEADS),
        in_specs=in_specs, out_specs=out_specs, out_shape=out_shape,
        scratch_shapes=[pltpu.VMEM((seq + past, HEAD_DIM), BF16),
                        pltpu.VMEM((seq + past, HEAD_DIM), BF16)],
        compiler_params=_cparams(("arbitrary", "arbitrary")),
        name="attn_ctx" if ctx else "attn_lat",
    )(*args)


def _s5_prep_kernel(are_ref, aim_ref, ldt_ref, bre_ref, bim_ref, pre_ref, pim_ref, wre_ref, wim_ref):
    a_re = are_ref[...]
    a_im = aim_ref[...]
    dt = jnp.exp(ldt_ref[...])
    pows = []
    for tau in range(S5_LAGS + 1):
        mag = jnp.exp((tau * dt) * a_re)
        ang = (tau * dt) * a_im
        pr, pi = mag * jnp.cos(ang), mag * jnp.sin(ang)
        pre_ref[tau] = pr
        pim_ref[tau] = pi
        pows.append((pr, pi))
    nr, ni = pows[1][0] - 1.0, pows[1][1]
    den = a_re * a_re + a_im * a_im
    cr = (nr * a_re + ni * a_im) / den
    ci = (ni * a_re - nr * a_im) / den
    for d in range(2):
        b_r, b_i = bre_ref[d], bim_ref[d]
        bb_r = cr[d:d + 1] * b_r - ci[d:d + 1] * b_i
        bb_i = cr[d:d + 1] * b_i + ci[d:d + 1] * b_r
        for tau in range(S5_LAGS):
            pr, pi = pows[tau][0][d:d + 1], pows[tau][1][d:d + 1]
            wre_ref[d, tau] = pr * bb_r - pi * bb_i
            wim_ref[d, tau] = pr * bb_i + pi * bb_r


def _s5_prep(a_re, a_im, log_dt, b_re, b_im, c_re, c_im):
    gp = S5_GROUPS * S5_STATE
    ldt = jnp.broadcast_to(log_dt[:, :, None], (2, S5_GROUPS, S5_STATE)).reshape(2, gp)
    bt = lambda b: b.transpose(0, 3, 1, 2).reshape(2, S5_CH, gp)
    pre, pim, wre, wim = pl.pallas_call(
        _s5_prep_kernel,
        out_shape=[jax.ShapeDtypeStruct((S5_LAGS + 1, 2, gp), F32)] * 2
        + [jax.ShapeDtypeStruct((2, S5_LAGS, S5_CH, gp), F32)] * 2,
        name="s5_prep",
    )(a_re.reshape(2, gp), a_im.reshape(2, gp), ldt, bt(b_re), bt(b_im))
    eye = jnp.eye(8, dtype=F32)

    hw = S5_SGW // 2
    half_mask = (jnp.arange(hw)[None, :] // S5_STATE == jnp.arange(4)[:, None]).astype(F32)

    def w_layout(w):
        w = w.reshape(2, S5_LAGS // 2, 2, S5_CH, S5_SG, 2, hw).transpose(0, 4, 5, 1, 2, 3, 6)
        w = w[:, :, :, :, :, None, :, :] * half_mask[:, None, :]
        return w.reshape(2, S5_SG, 2, S5_LAGS * LANE // 2, hw)

    w_in = jnp.concatenate([w_layout(wre), w_layout(wim)], axis=-1).astype(BF16)

    def c_layout(c):
        c = c.reshape(2, S5_SG, 8, S5_CH, S5_STATE).transpose(0, 1, 2, 4, 3)
        c = c[:, :, :, :, None, :] * eye[None, None, :, None, :, None]
        return c.reshape(2, S5_SG, S5_SGW, LANE)

    w_out = jnp.concatenate([c_layout(c_re), -c_layout(c_im)], axis=2).astype(BF16)

    def p_layout(p):
        p = p[1:].reshape(S5_LAGS, 2, S5_SG, S5_SGW).transpose(1, 2, 0, 3)
        return jnp.stack([p[0], p[1, :, ::-1]], axis=0)

    def a8_layout(p):
        p = p[S5_LAGS].reshape(2, S5_SG, 1, S5_SGW)
        return jnp.broadcast_to(p, (2, S5_SG, SUBLANE, S5_SGW))

    return w_in, w_out, a8_layout(pre), a8_layout(pim), p_layout(pre), p_layout(pim)


def _s5_kernel(u_ref, w_ref, c_ref, a8r_ref, a8i_ref, pwr_ref, pwi_ref, h0r_ref, h0i_ref,
               y_ref, hfr_ref, hfi_ref, upad, wbr, wbi, *, seq, tc, nseq):
    d = pl.program_id(0)
    zeros = jnp.zeros((SUBLANE, LANE), F32)
    for q in range(nseq):
        upad[q, 0:SUBLANE, :] = zeros
        upad[q, SUBLANE:seq + SUBLANE, :] = u_ref[q * seq:(q + 1) * seq, :]
        upad[q, seq + SUBLANE:seq + 2 * SUBLANE, :] = zeros
    nch = seq // tc
    nt = tc // SUBLANE
    a8r = a8r_ref[0, 0]
    a8i = a8i_ref[0, 0]

    def run(fwd):
        pwr = pwr_ref[0, 0]
        pwi = pwi_ref[0, 0]
        state = tuple((jnp.zeros((SUBLANE, S5_SGW), F32), jnp.zeros((SUBLANE, S5_SGW), F32))
                      for _ in range(nseq))
        for ci in range(nch):
            c = ci if fwd else nch - 1 - ci
            per_seq = []
            for q in range(nseq):
                if fwd:
                    win = upad[q, c * tc:c * tc + tc + SUBLANE, :]
                    per_seq.append([pltpu.roll(win, tau, 0)[SUBLANE:SUBLANE + tc] if tau
                                    else win[SUBLANE:SUBLANE + tc] for tau in range(S5_LAGS)])
                else:
                    win = upad[q, c * tc + SUBLANE:c * tc + tc + 2 * SUBLANE, :]
                    per_seq.append([pltpu.roll(win, tc + SUBLANE - tau, 0)[0:tc] if tau else win[0:tc]
                                    for tau in range(S5_LAGS)])
            lags = [jnp.concatenate([per_seq[q][tau] for q in range(nseq)], axis=0) if nseq > 1
                    else per_seq[0][tau] for tau in range(S5_LAGS)]
            low = lax.broadcasted_iota(jnp.int32, (nseq * tc, LANE), 1) < LANE // 2
            slabs = ([], [])
            for k in range(S5_LAGS // 2):
                a, b = lags[2 * k], lags[2 * k + 1]
                slabs[0].append(jnp.where(low, a, pltpu.roll(b, LANE // 2, 1)).astype(BF16))
                slabs[1].append(jnp.where(low, pltpu.roll(a, LANE // 2, 1), b).astype(BF16))
            hw = S5_SGW // 2
            for half in range(2):
                w = jnp.dot(jnp.concatenate(slabs[half], axis=1), w_ref[0, 0, half],
                            preferred_element_type=F32)
                wbr[:, half * hw:(half + 1) * hw] = w[:, :hw]
                wbi[:, half * hw:(half + 1) * hw] = w[:, hw:]
            if ci == 0:
                for q in range(nseq):
                    h0r = h0r_ref[q, 0, 0]
                    h0i = h0i_ref[q, 0, 0]
                    r0 = q * tc + (0 if fwd else tc - SUBLANE)
                    wbr[r0:r0 + SUBLANE, :] = wbr[r0:r0 + SUBLANE, :] + (pwr * h0r - pwi * h0i)
                    wbi[r0:r0 + SUBLANE, :] = wbi[r0:r0 + SUBLANE, :] + (pwr * h0i + pwi * h0r)

            def step(i, carry):
                t = i if fwd else nt - 1 - i
                new = []
                for q in range(nseq):
                    cr, ci_ = carry[q]
                    rows = pl.ds(pl.multiple_of(q * tc + t * SUBLANE, SUBLANE), SUBLANE)
                    nr = a8r * cr - a8i * ci_ + wbr[rows, :]
                    ni = a8r * ci_ + a8i * cr + wbi[rows, :]
                    wbr[rows, :] = nr
                    wbi[rows, :] = ni
                    new.append((nr, ni))
                return tuple(new)

            state = lax.fori_loop(0, nt, step, state)
            hcat = jnp.concatenate([wbr[...].astype(BF16), wbi[...].astype(BF16)], axis=1)
            y = jnp.dot(hcat, c_ref[0, 0], preferred_element_type=F32)
            for q in range(nseq):
                y_ref[0, q * seq + c * tc:q * seq + (c + 1) * tc, :] = y[q * tc:(q + 1) * tc]
        last = SUBLANE - 1 if fwd else 0
        for q in range(nseq):
            hfr_ref[q, 0, 0] = state[q][0][last:last + 1]
            hfi_ref[q, 0, 0] = state[q][1][last:last + 1]

    @pl.when(d == 0)
    def _():
        run(True)

    @pl.when(d == 1)
    def _():
        run(False)


def _s5_scan(z, prep, h0r, h0i, *, ctx):
    w_in, w_out, a8r, a8i, pwr, pwi = prep
    seq = SEQ if ctx else DEC_SEQ
    nb = BATCH if ctx else DEC_BATCH
    row0 = 0 if ctx else T_CTX // DEC_SEQ
    tc = min(seq, 512)
    nseq = S5_CTX_NSEQ if ctx else 1
    par = lambda d, s, b: (d, s, 0, 0)
    st = lambda d, s, b: (b, d, s, 0, 0)
    return pl.pallas_call(
        functools.partial(_s5_kernel, seq=seq, tc=tc, nseq=nseq),
        grid=(2, S5_SG, nb // nseq),
        in_specs=[pl.BlockSpec((nseq * seq, LANE), lambda d, s, b: (row0 + b, Z_SX // LANE + s)),
                  pl.BlockSpec((1, 1, 2, S5_LAGS * LANE // 2, S5_SGW), lambda d, s, b: (d, s, 0, 0, 0)),
                  pl.BlockSpec((1, 1, 2 * S5_SGW, LANE), par),
                  pl.BlockSpec((1, 1, SUBLANE, S5_SGW), par),
                  pl.BlockSpec((1, 1, SUBLANE, S5_SGW), par),
                  pl.BlockSpec((1, 1, SUBLANE, S5_SGW), par),
                  pl.BlockSpec((1, 1, SUBLANE, S5_SGW), par),
                  pl.BlockSpec((nseq, 1, 1, 1, S5_SGW), st),
                  pl.BlockSpec((nseq, 1, 1, 1, S5_SGW), st)],
        out_specs=[pl.BlockSpec((1, nseq * seq, LANE), lambda d, s, b: (d, b, s)),
                   pl.BlockSpec((nseq, 1, 1, 1, S5_SGW), st),
                   pl.BlockSpec((nseq, 1, 1, 1, S5_SGW), st)],
        out_shape=[jax.ShapeDtypeStruct((2, nb * seq, MIX_W), F32),
                   jax.ShapeDtypeStruct((nb, 2, S5_SG, 1, S5_SGW), F32),
                   jax.ShapeDtypeStruct((nb, 2, S5_SG, 1, S5_SGW), F32)],
        scratch_shapes=[pltpu.VMEM((nseq, seq + 2 * SUBLANE, LANE), F32),
                        pltpu.VMEM((nseq * tc, S5_SGW), F32),
                        pltpu.VMEM((nseq * tc, S5_SGW), F32)],
        compiler_params=_cparams(("arbitrary", "arbitrary", "arbitrary")),
        name="s5_ctx" if ctx else "s5_lat",
    )(z, w_in, w_out, a8r, a8i, pwr, pwi, h0r, h0i)


def _mlstm_kernel(q_ref, k_ref, v_ref, mo_ref, gc_ref, gr_ref, bc_ref, br_ref, c0_ref, n0_ref, m0_ref,
                  ng_ref, h_ref, c_ref, n_ref, m_ref, hs_scr, *, seq, chunk0):
    nc = seq // CHUNK
    ii = lax.broadcasted_iota(jnp.int32, (CHUNK, CHUNK), 0)
    jj = lax.broadcasted_iota(jnp.int32, (CHUNK, CHUNK), 1)
    neg_inf = jnp.float32(-jnp.inf)

    def chunk_step(hh, d, cidx, carry):
        cols = slice(hh * MLSTM_DIM, (hh + 1) * MLSTM_DIM)
        fwd = d == 0
        mask = (jj <= ii) if fwd else (jj >= ii)
        mask_t = (ii <= jj) if fwd else (ii >= jj)
        last = CHUNK - 1 if fwd else 0
        c_st, n_st, m_st = carry
        rows = pl.ds(pl.multiple_of(cidx * CHUNK, CHUNK), CHUNK)
        q = q_ref[rows, cols]
        k = k_ref[rows, cols] * (MLSTM_DIM ** -0.5)
        v = v_ref[rows, cols]
        gcol = gc_ref[hh, rows, :] + bc_ref[hh]
        grow = gr_ref[hh, chunk0 + cidx] + br_ref[hh]
        li_c = gcol[:, 2 * d:2 * d + 1]
        lf_c = jax.nn.log_sigmoid(gcol[:, 2 * d + 1:2 * d + 2])
        li_r = grow[2 * d:2 * d + 1, :]
        lf_r = jax.nn.log_sigmoid(grow[2 * d + 1:2 * d + 2, :])
        b_c = jnp.sum(jnp.where(mask, lf_r, 0.0), axis=1, keepdims=True)
        b_r = jnp.sum(jnp.where(mask_t, lf_c, 0.0), axis=0, keepdims=True)
        dmat = jnp.where(mask, b_c - b_r + li_r, neg_inf)
        inter = b_c + m_st
        m_j = jnp.maximum(inter, jnp.max(dmat, axis=1, keepdims=True))
        qb = q.astype(BF16)
        vb = v.astype(BF16)
        s = lax.dot_general(qb, k.astype(BF16), (((1,), (1,)), ((), ())),
                            preferred_element_type=F32) * jnp.exp(dmat - m_j)
        w_inter = jnp.exp(inter - m_j)
        num = (jnp.dot(s.astype(BF16), vb, preferred_element_type=F32)
               + w_inter * jnp.dot(qb, c_st.astype(BF16), preferred_element_type=F32))
        den = jnp.sum(s, axis=1, keepdims=True) + w_inter * jnp.sum(q * n_st, axis=1, keepdims=True)
        hs_scr[d, rows, cols] = num / jnp.maximum(jnp.abs(den), jnp.exp(-m_j))
        m_end = m_j[last:last + 1, :]
        b_end = b_c[last:last + 1, :]
        w_c = jnp.exp(b_end - b_c + li_c - m_end)
        decay = jnp.exp(b_end + m_st - m_end)
        kw = k * w_c
        c_new = decay * c_st + lax.dot_general(kw.astype(BF16), vb, (((0,), (0,)), ((), ())),
                                               preferred_element_type=F32)
        n_new = decay * n_st + jnp.sum(kw, axis=0, keepdims=True)
        return c_new, n_new, m_end

    chains = [(hh, d) for hh in range(MLSTM_HG) for d in range(2)]

    def all_chains(ci, carry):
        return tuple(chunk_step(hh, d, ci if d == 0 else nc - 1 - ci, carry[n])
                     for n, (hh, d) in enumerate(chains))

    init = tuple((c0_ref[0, d, hh], n0_ref[0, d, hh], m0_ref[0, d, hh]) for hh, d in chains)
    final = lax.fori_loop(0, nc, all_chains, init)
    for n, (hh, d) in enumerate(chains):
        c_ref[0, d, hh], n_ref[0, d, hh], m_ref[0, d, hh] = final[n]
    for hh in range(MLSTM_HG):
        cols = slice(hh * MLSTM_DIM, (hh + 1) * MLSTM_DIM)
        h = _rms(hs_scr[0, :, cols] + hs_scr[1, :, cols], ng_ref[hh])
        h_ref[:, cols] = (h * jax.nn.sigmoid(mo_ref[:, cols])).astype(BF16)


def _mlstm(z, gcol, grow, bias_c, bias_r, c0, n0, m0, norm_g, *, ctx):
    seq = SEQ if ctx else DEC_SEQ
    nb = BATCH if ctx else DEC_BATCH
    row0 = 0 if ctx else T_CTX // DEC_SEQ
    hd = MLSTM_DIM
    nh = MLSTM_HEADS
    hg = MLSTM_HG
    hw = hg * hd
    zcol = lambda off: (lambda b, h: (row0 + b, off // hw + h))
    st5 = lambda b, h: (b, 0, h, 0, 0)
    return pl.pallas_call(
        functools.partial(_mlstm_kernel, seq=seq, chunk0=0),
        grid=(nb, nh // hg),
        in_specs=[pl.BlockSpec((seq, hw), zcol(Z_MQ)),
                  pl.BlockSpec((seq, hw), zcol(Z_MK)),
                  pl.BlockSpec((seq, hw), zcol(Z_MV)),
                  pl.BlockSpec((seq, hw), zcol(Z_MO)),
                  pl.BlockSpec((hg, seq, 4), lambda b, h: (h, row0 + b, 0)),
                  pl.BlockSpec((hg, seq // CHUNK, 4, CHUNK), lambda b, h: (h, row0 + b, 0, 0)),
                  pl.BlockSpec((hg, 1, 4), lambda b, h: (h, 0, 0)),
                  pl.BlockSpec((hg, 4, 1), lambda b, h: (h, 0, 0)),
                  pl.BlockSpec((1, 2, hg, hd, hd), st5),
                  pl.BlockSpec((1, 2, hg, 1, hd), st5),
                  pl.BlockSpec((1, 2, hg, 1, 1), st5),
                  pl.BlockSpec((hg, 1, hd), lambda b, h: (h, 0, 0))],
        out_specs=[pl.BlockSpec((seq, hw), lambda b, h: (b, h)),
                   pl.BlockSpec((1, 2, hg, hd, hd), st5),
                   pl.BlockSpec((1, 2, hg, 1, hd), st5),
                   pl.BlockSpec((1, 2, hg, 1, 1), st5)],
        out_shape=[jax.ShapeDtypeStruct((nb * seq, MIX_W), BF16),
                   jax.ShapeDtypeStruct((nb, 2, nh, hd, hd), F32),
                   jax.ShapeDtypeStruct((nb, 2, nh, 1, hd), F32),
                   jax.ShapeDtypeStruct((nb, 2, nh, 1, 1), F32)],
        scratch_shapes=[pltpu.VMEM((2, seq, hw), F32)],
        compiler_params=_cparams(("arbitrary", "arbitrary")),
        name="mlstm_ctx" if ctx else "mlstm_lat",
    )(z, z, z, z, gcol, grow, bias_c, bias_r, c0, n0, m0, norm_g.reshape(nh, 1, hd))


OUT_TM = 256


def _out_kernel(*refs, router):
    (ac_ref, al_ref, yc_ref, yl_ref, sx_ref, sd_ref, sw_ref, sb_ref, cc_ref, cl_ref,
     gu_ref, gv_ref, gn_ref, gw_ref, gb_ref, xc_ref, xl_ref, mod_ref, g_ref, w_ref), refs = refs[:20], refs[20:]
    if router:
        rw_ref, rb_ref, x1_ref, h2_ref, rt_ref = refs
    else:
        x1_ref, h2_ref = refs
    y2 = _pick(OUT_TM, yc_ref, yl_ref)
    y = jax.nn.gelu(y2[0] + y2[1] + sd_ref[...] * sx_ref[...])
    gate = jnp.dot(y.astype(BF16), sw_ref[...], preferred_element_type=F32) + sb_ref[...]
    s5_out = (y * jax.nn.sigmoid(gate)).astype(BF16)
    vn = _rms(gv_ref[...], gn_ref[...]).astype(BF16)
    gw = MIX_W // GMLP_GROUPS
    chunks = []
    for c in range(OUT_TM // CHUNK):
        r = slice(c * CHUNK, (c + 1) * CHUNK)
        groups = []
        for g in range(GMLP_GROUPS):
            cs = slice(g * gw, (g + 1) * gw)
            mixed = jnp.dot(gw_ref[g], vn[r, cs], preferred_element_type=F32) + gb_ref[:, g:g + 1]
            groups.append((gu_ref[r, cs] * mixed).astype(BF16))
        chunks.append(jnp.concatenate(groups, axis=1))
    gm_out = jnp.concatenate(chunks, axis=0)
    mix = jnp.dot(_pick(OUT_TM, ac_ref, al_ref), w_ref[0:MIX_W, :], preferred_element_type=F32)
    mix += jnp.dot(s5_out, w_ref[MIX_W:2 * MIX_W, :], preferred_element_type=F32)
    mix += jnp.dot(_pick(OUT_TM, cc_ref, cl_ref), w_ref[2 * MIX_W:3 * MIX_W, :], preferred_element_type=F32)
    mix += jnp.dot(gm_out, w_ref[3 * MIX_W:4 * MIX_W, :], preferred_element_type=F32)
    x1 = _pick(OUT_TM, xc_ref, xl_ref) + mod_ref[0, 2:3, :] * mix
    x1_ref[...] = x1
    h2 = _rms(x1, g_ref[...]) * (1.0 + mod_ref[0, 4:5, :]) + mod_ref[0, 3:4, :]
    hi = h2.astype(BF16)
    h2_ref[...] = h2 if router else hi
    if router:
        lo = (h2 - hi.astype(F32)).astype(BF16)
        both = jnp.dot(hi, rw_ref[...], preferred_element_type=F32)
        logits = (both[:, :LANE] + jnp.dot(lo, rw_ref[:, :LANE], preferred_element_type=F32)
                  + both[:, LANE:]) + rb_ref[...]
        lane = lax.broadcasted_iota(jnp.int32, logits.shape, 1)
        neg_inf = jnp.float32(-jnp.inf)
        lg = jnp.where(lane < N_EXPERTS, logits, neg_inf)
        m1 = jnp.max(lg, axis=-1, keepdims=True)
        i1 = jnp.min(jnp.where(lg == m1, lane, LANE), axis=-1, keepdims=True)
        lg2 = jnp.where(lane == i1, neg_inf, lg)
        m2 = jnp.max(lg2, axis=-1, keepdims=True)
        i2 = jnp.min(jnp.where(lg2 == m2, lane, LANE), axis=-1, keepdims=True)
        e = jnp.exp(m2 - m1)
        w1 = 1.0 / (1.0 + e)
        w2 = e / (1.0 + e)
        rt = jnp.where(lane == 0, i1.astype(F32), 0.0)
        rt = jnp.where(lane == 1, i2.astype(F32), rt)
        rt = jnp.where(lane == 2, w1, rt)
        rt = jnp.where(lane == 3, w2, rt)
        rt_ref[...] = rt


def _out_proj(z, att_pair, s5_args, ml_pair, gm_args, x_pair, mod_l, g2, w_out, router=None):
    tm = OUT_TM
    row = lambda i: (i, 0)
    fixed = lambda i: (0, 0)
    pair = _pair_specs(tm, (tm, MIX_W))
    zcols = lambda off: pl.BlockSpec((tm, MIX_W), lambda i: (i, off // MIX_W))
    vec = pl.BlockSpec((1, MIX_W), fixed)
    y_pair, d_skip, w_glu, b_glu = s5_args
    gm_norm_g, gm_w_s, gm_b_s = gm_args
    in_specs = (pair + _pair_specs(tm, (2, tm, MIX_W), rows_dim=1)
                + [zcols(Z_SX), vec, pl.BlockSpec((MIX_W, MIX_W), fixed), vec]
                + pair
                + [zcols(Z_GU), zcols(Z_GV), vec, pl.BlockSpec((GMLP_GROUPS, CHUNK, CHUNK), lambda i: (0, 0, 0)),
                   pl.BlockSpec((CHUNK, GMLP_GROUPS), fixed)]
                + _pair_specs(tm, (tm, D_MODEL))
                + [pl.BlockSpec((1, MOD_CHUNKS, D_MODEL), lambda i: (_mod_group(i, tm), 0, 0)),
                   pl.BlockSpec((1, D_MODEL), fixed),
                   pl.BlockSpec((D_MODEL, D_MODEL), fixed, pipeline_mode=pl.Buffered(1))])
    args = [*att_pair, *y_pair, z, d_skip.reshape(1, MIX_W), w_glu.astype(BF16), b_glu.reshape(1, MIX_W),
            *ml_pair, z, z, gm_norm_g.reshape(1, MIX_W), gm_w_s.astype(BF16), gm_b_s.T,
            *x_pair, mod_l, g2.reshape(1, D_MODEL), w_out]
    out_specs = [pl.BlockSpec((tm, D_MODEL), row), pl.BlockSpec((tm, D_MODEL), row)]
    out_shape = [jax.ShapeDtypeStruct((T_ALL, D_MODEL), F32),
                 jax.ShapeDtypeStruct((T_ALL, D_MODEL), BF16 if router is None else F32)]
    if router is not None:
        rw, rb = router
        rw = jnp.pad(rw, ((0, 0), (0, LANE - N_EXPERTS)))
        rh = rw.astype(BF16)
        rl = (rw - rh.astype(F32)).astype(BF16)
        in_specs += [pl.BlockSpec((D_MODEL, 2 * LANE), fixed), pl.BlockSpec((1, LANE), fixed)]
        args += [jnp.concatenate([rh, rl], axis=1), jnp.pad(rb, (0, LANE - N_EXPERTS)).reshape(1, LANE)]
        out_specs.append(pl.BlockSpec((tm, LANE), row))
        out_shape.append(jax.ShapeDtypeStruct((T_ALL, LANE), F32))
    return pl.pallas_call(
        functools.partial(_out_kernel, router=router is not None),
        grid=(T_ALL // tm,),
        in_specs=in_specs, out_specs=out_specs, out_shape=out_shape,
        compiler_params=_cparams(("arbitrary",)),
        name="out_proj_router" if router is not None else "out_proj",
    )(*args)


def _ffn_kernel(x_ref, wg_ref, wu_ref, wd_ref, o_ref):
    @pl.when(pl.program_id(1) == 0)
    def _():
        o_ref[...] = jnp.zeros_like(o_ref)

    def sub(s, carry):
        rows = pl.ds(pl.multiple_of(s * FFN_DENSE_SUB, FFN_DENSE_SUB), FFN_DENSE_SUB)
        xs = x_ref[rows, :]
        g = jnp.dot(xs, wg_ref[...], preferred_element_type=F32)
        u = jnp.dot(xs, wu_ref[...], preferred_element_type=F32)
        a = (g * jax.nn.sigmoid(g) * u).astype(BF16)
        o_ref[rows, :] += jnp.dot(a, wd_ref[...], preferred_element_type=F32)
        return carry

    lax.fori_loop(0, FFN_TM // FFN_DENSE_SUB, sub, 0)


def _ffn(x, w_gate, w_up, w_down):
    return pl.pallas_call(
        _ffn_kernel,
        grid=(T_ALL // FFN_TM, D_FF // FFN_TF),
        in_specs=[pl.BlockSpec((FFN_TM, D_MODEL), lambda i, f: (i, 0)),
                  pl.BlockSpec((D_MODEL, FFN_TF), lambda i, f: (0, f)),
                  pl.BlockSpec((D_MODEL, FFN_TF), lambda i, f: (0, f)),
                  pl.BlockSpec((FFN_TF, D_MODEL), lambda i, f: (f, 0))],
        out_specs=pl.BlockSpec((FFN_TM, D_MODEL), lambda i, f: (i, 0)),
        out_shape=jax.ShapeDtypeStruct((T_ALL, D_MODEL), F32),
        compiler_params=_cparams(("arbitrary", "arbitrary")),
        name="ffn",
    )(x, w_gate, w_up, w_down)


def _moe_ffn_kernel(te_ref, ns_ref, src_ref, x_hbm, wg_ref, wu_ref, wd_ref, o_ref, xf, xb, sems):
    i = pl.program_id(0)
    f = pl.program_id(1)
    ntiles = pl.num_programs(0)

    def row_copy(src_row, slot, r):
        return pltpu.make_async_copy(x_hbm.at[pl.ds(src_row, 1)], xf.at[slot, pl.ds(r, 1)], sems.at[slot])

    def issue(tile, slot):
        def body(r8, carry):
            for u in range(ISSUE_UNROLL):
                r = r8 * ISSUE_UNROLL + u
                row_copy(src_ref[tile * FFN_TM + r], slot, r).start()
            return carry

        lax.fori_loop(0, ns_ref[tile] * (FFN_SUB // ISSUE_UNROLL), body, 0)

    def wait_rows(tile, slot):
        def body(s, carry):
            rows = pl.ds(pl.multiple_of(s * FFN_SUB, FFN_SUB), FFN_SUB)
            pltpu.make_async_copy(x_hbm.at[pl.ds(0, FFN_SUB)], xf.at[slot, rows], sems.at[slot]).wait()
            return carry

        lax.fori_loop(0, ns_ref[tile], body, 0)

    @pl.when(f == 0)
    def _():
        slot = i % 2

        @pl.when(i == 0)
        def _():
            issue(0, 0)

        wait_rows(i, slot)

        @pl.when(i + 1 < ntiles)
        def _():
            issue(i + 1, 1 - slot)

        def cast(s, carry):
            rows = pl.ds(pl.multiple_of(s * FFN_SUB, FFN_SUB), FFN_SUB)
            xb[rows, :] = xf[slot, rows, :].astype(BF16)
            return carry

        lax.fori_loop(0, ns_ref[i], cast, 0)
        o_ref[...] = jnp.zeros_like(o_ref)

    ns = ns_ref[i]

    def block(row0, size):
        rows = pl.ds(pl.multiple_of(row0, size), size)
        xs = xb[rows, :]
        g = jnp.dot(xs, wg_ref[0], preferred_element_type=F32)
        u = jnp.dot(xs, wu_ref[0], preferred_element_type=F32)
        a = (g * jax.nn.sigmoid(g) * u).astype(BF16)
        o_ref[rows, :] += jnp.dot(a, wd_ref[0], preferred_element_type=F32)

    pairs = ns // 2

    def pair(s, carry):
        block(s * (2 * FFN_SUB), 2 * FFN_SUB)
        return carry

    lax.fori_loop(0, pairs, pair, 0)

    @pl.when(ns % 2 == 1)
    def _():
        block(pairs * (2 * FFN_SUB), FFN_SUB)


def _moe_ffn(h2, tile_e, tile_ns, src, w_gate, w_up, w_down):
    tf = FFN_TF
    nf = D_FF // tf

    def fcol(i, f, ns):
        return jnp.where(ns[i] > 0, f, nf - 1)

    return pl.pallas_call(
        _moe_ffn_kernel,
        grid_spec=pltpu.PrefetchScalarGridSpec(
            num_scalar_prefetch=3,
            grid=(MOE_TILES, nf),
            in_specs=[pl.BlockSpec(memory_space=pl.ANY),
                      pl.BlockSpec((1, D_MODEL, tf), lambda i, f, te, ns, sr: (te[i], 0, fcol(i, f, ns))),
                      pl.BlockSpec((1, D_MODEL, tf), lambda i, f, te, ns, sr: (te[i], 0, fcol(i, f, ns))),
                      pl.BlockSpec((1, tf, D_MODEL), lambda i, f, te, ns, sr: (te[i], fcol(i, f, ns), 0))],
            out_specs=pl.BlockSpec((FFN_TM, D_MODEL), lambda i, f, te, ns, sr: (i, 0)),
            scratch_shapes=[pltpu.VMEM((2, FFN_TM, D_MODEL), F32),
                            pltpu.VMEM((FFN_TM, D_MODEL), BF16),
                            pltpu.SemaphoreType.DMA((2,))]),
        out_shape=jax.ShapeDtypeStruct((MOE_TILES * FFN_TM, D_MODEL), F32),
        compiler_params=_cparams(("arbitrary", "arbitrary")),
        name="moe_ffn",
    )(tile_e, tile_ns, src, h2, w_gate, w_up, w_down)


RES_TM = 256


def _res_kernel(x_ref, y_ref, mod_ref, g_ref, oc_ref, ol_ref, *, final):
    x2 = x_ref[...] + mod_ref[0, 5:6, :] * y_ref[...]
    _store_pair(RES_TM, _rms(x2, g_ref[...]) if final else x2, oc_ref, ol_ref)


def _moe_res_kernel(slot_ref, x_ref, rt_ref, mod_ref, g_ref, ys_hbm, oc_ref, ol_ref, ybuf, sems, *, final):
    i = pl.program_id(0)
    nt = pl.num_programs(0)

    def row_copy(src_row, buf, k, r):
        return pltpu.make_async_copy(ys_hbm.at[pl.ds(src_row, 1)], ybuf.at[buf, k, pl.ds(r, 1)], sems.at[buf])

    def issue(tile, buf):
        def body(r8, carry):
            for u in range(ISSUE_UNROLL):
                r = r8 * ISSUE_UNROLL + u
                for k in range(2):
                    row_copy(slot_ref[k * T_ALL + tile * RES_TM + r], buf, k, r).start()
            return carry

        lax.fori_loop(0, RES_TM // ISSUE_UNROLL, body, 0)

    buf = i % 2

    @pl.when(i == 0)
    def _():
        issue(0, 0)

    for k in range(2):
        pltpu.make_async_copy(ys_hbm.at[pl.ds(0, RES_TM)], ybuf.at[buf, k], sems.at[buf]).wait()

    @pl.when(i + 1 < nt)
    def _():
        issue(i + 1, 1 - buf)

    rt = rt_ref[...]
    ffn = rt[:, 2:3] * ybuf[buf, 0] + rt[:, 3:4] * ybuf[buf, 1]
    x2 = x_ref[...] + mod_ref[0, 5:6, :] * ffn
    _store_pair(RES_TM, _rms(x2, g_ref[...]) if final else x2, oc_ref, ol_ref)


def _residual(x1, y, mod_l, final_g, *, route=None, slot=None, final):
    tm = RES_TM
    nt = T_ALL // tm
    out_shape = [jax.ShapeDtypeStruct((T_CTX, D_MODEL), F32), jax.ShapeDtypeStruct((T_LAT, D_MODEL), F32)]
    out_specs = _pair_specs(tm, (tm, D_MODEL))
    if route is None:
        row = lambda i: (i, 0)
        return pl.pallas_call(
            functools.partial(_res_kernel, final=final),
            grid=(nt,),
            in_specs=[pl.BlockSpec((tm, D_MODEL), row), pl.BlockSpec((tm, D_MODEL), row),
                      pl.BlockSpec((1, MOD_CHUNKS, D_MODEL), lambda i: (_mod_group(i, tm), 0, 0)),
                      pl.BlockSpec((1, D_MODEL), lambda i: (0, 0))],
            out_specs=out_specs,
            out_shape=out_shape,
            compiler_params=_cparams(("arbitrary",)),
            name="ffn_residual",
        )(x1, y, mod_l, final_g.reshape(1, D_MODEL))
    row = lambda i, sl: (i, 0)
    return pl.pallas_call(
        functools.partial(_moe_res_kernel, final=final),
        grid_spec=pltpu.PrefetchScalarGridSpec(
            num_scalar_prefetch=1,
            grid=(nt,),
            in_specs=[pl.BlockSpec((tm, D_MODEL), row), pl.BlockSpec((tm, LANE), row),
                      pl.BlockSpec((1, MOD_CHUNKS, D_MODEL), lambda i, sl: (_mod_group(i, tm), 0, 0)),
                      pl.BlockSpec((1, D_MODEL), lambda i, sl: (0, 0)),
                      pl.BlockSpec(memory_space=pl.ANY)],
            out_specs=out_specs,
            scratch_shapes=[pltpu.VMEM((2, 2, tm, D_MODEL), F32), pltpu.SemaphoreType.DMA((2,))]),
        out_shape=out_shape,
        compiler_params=_cparams(("arbitrary",)),
        name="moe_residual",
    )(slot, x1, route, mod_l, final_g.reshape(1, D_MODEL), y)


def _route_plan(route):
    e_flat = jnp.concatenate([route[:, 0], route[:, 1]]).astype(jnp.int32)
    onehot = (e_flat[:, None] == jnp.arange(N_EXPERTS, dtype=jnp.int32)[None, :]).astype(jnp.int32)
    ranks = jnp.cumsum(onehot, axis=0) - onehot
    rank = jnp.sum(ranks * onehot, axis=1)
    counts = jnp.sum(onehot, axis=0)
    tiles = (counts + FFN_TM - 1) // FFN_TM
    tile_start = jnp.cumsum(tiles) - tiles
    slot = tile_start[e_flat] * FFN_TM + rank
    n_slots = MOE_TILES * FFN_TM
    tok = jnp.concatenate([jnp.arange(T_ALL, dtype=jnp.int32)] * 2)
    src = jnp.zeros((n_slots,), jnp.int32).at[slot].set(tok)
    tile_ids = jnp.arange(MOE_TILES, dtype=jnp.int32)
    ends = jnp.cumsum(tiles)
    tile_e = jnp.sum((tile_ids[:, None] >= ends[None, :]).astype(jnp.int32), axis=1)
    used = tile_e < N_EXPERTS
    last_e = jnp.max(jnp.where(counts > 0, jnp.arange(N_EXPERTS, dtype=jnp.int32), 0))
    tile_e = jnp.where(used, tile_e, last_e)
    rows_in_tile = jnp.clip(counts[tile_e] - (tile_ids - tile_start[tile_e]) * FFN_TM, 0, FFN_TM)
    tile_ns = jnp.where(used, (rows_in_tile + FFN_SUB - 1) // FFN_SUB, 0).astype(jnp.int32)
    return src, slot, tile_e.astype(jnp.int32), tile_ns


def _rope_tables():
    length = DEC_SEQ
    r = jnp.repeat(jnp.arange(length // GRID_W, dtype=F32), GRID_W)
    col = (jnp.arange(length) % GRID_W).astype(F32)
    half = HEAD_DIM // 2
    inv = ROPE_THETA ** (-jnp.arange(0, half, 2, dtype=F32) / half)
    ar, ac = r[:, None] * inv, col[:, None] * inv
    cos = jnp.concatenate([jnp.cos(ar), jnp.cos(ar), jnp.cos(ac), jnp.cos(ac)], axis=-1)
    sin = jnp.concatenate([-jnp.sin(ar), jnp.sin(ar), -jnp.sin(ac), jnp.sin(ac)], axis=-1)
    return cos, sin


def kernel(x_prompt, x_sample, c, c_ctx, cache_attn_k, cache_attn_v, state_s5_re, state_s5_im, state_mlstm_c, state_mlstm_n, state_mlstm_m, norm1_g, norm2_g, w_mod, b_mod, w_in, w_out, q_norm_g, k_norm_g, s5_a_re, s5_a_im, s5_log_dt, s5_b_re, s5_b_im, s5_c_re, s5_c_im, s5_d, s5_w_glu, s5_b_glu, mlstm_i_bias, mlstm_f_bias, mlstm_norm_g, gmlp_norm_g, gmlp_w_s, gmlp_b_s, ffn_w_gate, ffn_w_up, ffn_w_down, moe_router, moe_router_bias, moe_w_gate, moe_w_up, moe_w_down, final_norm_g):
    x = (x_prompt.reshape(T_CTX, D_MODEL), x_sample.reshape(T_LAT, D_MODEL))
    cond = jnp.concatenate([c_ctx[None, :], c], axis=0)
    mod = _modulation(cond, w_mod, b_mod)
    cos, sin = _rope_tables()
    nh = MLSTM_HEADS
    zeros_s5 = jnp.zeros((BATCH, 2, S5_SG, 1, S5_SGW), F32)
    zeros_c = jnp.zeros((BATCH, 2, nh, MLSTM_DIM, MLSTM_DIM), F32)
    zeros_n = jnp.zeros((BATCH, 2, nh, 1, MLSTM_DIM), F32)
    zeros_m = jnp.zeros((BATCH, 2, nh, 1, 1), F32)

    ctx_states = []
    for l in range(DEPTH):
        use_moe = l % 2 == 1
        j = l // 2
        wl = w_in[l]
        w_p = jnp.concatenate([wl[:, :3584], wl[:, 3600:], wl[:, 3584:3600],
                               jnp.zeros((D_MODEL, Z_COLS - 4624), F32)], axis=1).astype(BF16)
        z, x = _in_proj(x, mod[l], norm1_g[l], w_p)

        att_c, k_new, v_new = _attention(z, None, q_norm_g[l], k_norm_g[l], ctx=True)
        (att_l,) = _attention(z, None, q_norm_g[l], k_norm_g[l], ctx=False,
                              kpast=cache_attn_k[:, l], vpast=cache_attn_v[:, l], cos=cos, sin=sin)

        prep = _s5_prep(s5_a_re[l], s5_a_im[l], s5_log_dt[l], s5_b_re[l], s5_b_im[l], s5_c_re[l], s5_c_im[l])
        y_c, hf_re, hf_im = _s5_scan(z, prep, zeros_s5, zeros_s5, ctx=True)
        st = lambda s: s[:, l].reshape(DEC_BATCH, 2, S5_SG, 1, S5_SGW)
        y_l, _, _ = _s5_scan(z, prep, st(state_s5_re), st(state_s5_im), ctx=False)
        s5_args = ((y_c, y_l), s5_d[l], s5_w_glu[l], s5_b_glu[l])

        mg = z[:, Z_MG:Z_MG + 16].reshape(T_ALL, 2, 2, nh)
        gcol = mg.transpose(3, 0, 1, 2).reshape(nh, T_ALL, 4)
        grow = mg.reshape(T_ALL // CHUNK, CHUNK, 4, nh).transpose(3, 0, 2, 1)
        bias = jnp.stack([mlstm_i_bias[l], mlstm_f_bias[l]], axis=1)
        bias_c = bias.transpose(2, 0, 1).reshape(nh, 1, 4)
        bias_r = bias_c.reshape(nh, 4, 1)
        ml_c_out, c_new, n_new, m_new = _mlstm(z, gcol, grow, bias_c, bias_r, zeros_c, zeros_n, zeros_m,
                                               mlstm_norm_g[l], ctx=True)
        ml_l_out, _, _, _ = _mlstm(z, gcol, grow, bias_c, bias_r, state_mlstm_c[:, l],
                                   state_mlstm_n[:, l].reshape(DEC_BATCH, 2, nh, 1, MLSTM_DIM),
                                   state_mlstm_m[:, l].reshape(DEC_BATCH, 2, nh, 1, 1),
                                   mlstm_norm_g[l], ctx=False)

        gm_args = (gmlp_norm_g[l], gmlp_w_s[l], gmlp_b_s[l])

        ctx_states.append((k_new, v_new,
                           hf_re.reshape(BATCH, 2, S5_GROUPS, S5_STATE), hf_im.reshape(BATCH, 2, S5_GROUPS, S5_STATE),
                           c_new, n_new.reshape(BATCH, 2, nh, MLSTM_DIM), m_new.reshape(BATCH, 2, nh)))

        w_o = w_out[l].astype(BF16)
        final = l == DEPTH - 1
        mixed = (z, (att_c, att_l), s5_args, (ml_c_out, ml_l_out), gm_args)
        if use_moe:
            x1, h2, route = _out_proj(*mixed, x, mod[l], norm2_g[l], w_o,
                                      router=(moe_router[j], moe_router_bias[j]))
            src, slot, tile_e, tile_ns = _route_plan(route)
            ys = _moe_ffn(h2, tile_e, tile_ns, src, moe_w_gate[j].astype(BF16), moe_w_up[j].astype(BF16),
                          moe_w_down[j].astype(BF16))
            x = _residual(x1, ys, mod[l], final_norm_g, route=route, slot=slot, final=final)
        else:
            x1, h2 = _out_proj(*mixed, x, mod[l], norm2_g[l], w_o)
            y = _ffn(h2, ffn_w_gate[j].astype(BF16), ffn_w_up[j].astype(BF16), ffn_w_down[j].astype(BF16))
            if final:
                x = _residual(x1, y, mod[l], final_norm_g, final=True)
            else:
                x = (x1, y, mod[l])

    y_prompt = x[0].reshape(BATCH, SEQ, D_MODEL)
    y_sample = x[1].reshape(DEC_BATCH, DEC_SEQ, D_MODEL)
    stack = lambda i: jnp.stack([s[i] for s in ctx_states], axis=1)
    return (y_prompt, y_sample, stack(0), stack(1), stack(2), stack(3), stack(4), stack(5), stack(6))
```

```python
import functools
import math

import numpy as np
import jax
import jax.numpy as jnp
from jax import lax
from jax.experimental import pallas as pl
from jax.experimental.pallas import tpu as pltpu

F32 = jnp.float32
BF16 = jnp.bfloat16

D_MODEL = 2048
BATCH = 16
SEQ = 256
DEPTH = 2
DEC_BATCH = 2
DEC_SEQ = 2048
PAST_LEN = 256
GRID_W = 64
MIX_W = 512
ATT_HEADS = 4
ATT_KV_HEADS = 2
HEAD_DIM = 128
ROPE_THETA = 10000.0
S5_CH = 16
S5_GROUPS = 32
S5_STATE = 64
MLSTM_HEADS = 4
MLSTM_DIM = 128
MLSTM_HG = 1
CHUNK = 128
GMLP_GROUPS = 4
D_FF = 7168
N_EXPERTS = 8
MOD_CHUNKS = 6
EPS = 1e-6

T_CTX = BATCH * SEQ
T_LAT = DEC_BATCH * DEC_SEQ
T_ALL = T_CTX + T_LAT
N_GROUPS_MOD = 1 + DEC_BATCH

Z_AQ, Z_AK, Z_AV, Z_SX = 0, 512, 768, 1024
Z_MQ, Z_MK, Z_MV, Z_MO = 1536, 2048, 2560, 3072
Z_GU, Z_GV, Z_MG = 3584, 4096, 4608
Z_COLS = 4736
W_GATES0, W_GATES1 = 3584, 3600
LANE = 128
SUBLANE = 8

VMEM_LIMIT = 56 * 1024 * 1024

S5_SG = 4
S5_SGW = 8 * S5_STATE
S5_LAGS = 8
S5_CTX_NSEQ = 1

FFN_TM = 1024
FFN_SUB = 256
FFN_DENSE_SUB = 512
FFN_TF = 512
MOE_TILES = 2 * T_ALL // FFN_TM + N_EXPERTS
ISSUE_UNROLL = 8


def _cparams(sem=None):
    return pltpu.CompilerParams(dimension_semantics=sem, vmem_limit_bytes=VMEM_LIMIT)


def _mod_group(i, tm):
    return jnp.maximum(i * tm // DEC_SEQ - (T_CTX // DEC_SEQ - 1), 0)


def _rms(x, g):
    return x * lax.rsqrt(jnp.mean(x * x, axis=-1, keepdims=True) + EPS) * g


MOD_TK = 256


def _mod_kernel(cb_ref, w_ref, b_ref, o_ref, silu_scr):
    k = pl.program_id(1)
    n = MOD_CHUNKS * D_MODEL

    @pl.when((pl.program_id(0) == 0) & (k == 0))
    def _():
        c = cb_ref[...]
        silu_scr[...] = c * jax.nn.sigmoid(c)

    @pl.when(k == 0)
    def _():
        o_ref[0] = jnp.broadcast_to(b_ref[0], (SUBLANE, n))

    rows = pl.ds(pl.multiple_of(k * MOD_TK, MOD_TK), MOD_TK)
    for r in range(N_GROUPS_MOD):
        s = silu_scr[r, rows, :]
        for j in range(n // LANE):
            cols = slice(j * LANE, (j + 1) * LANE)
            o_ref[0, r:r + 1, cols] += jnp.sum(w_ref[0, :, cols] * s, axis=0, keepdims=True)


def _modulation(cond, w_mod, b_mod):
    cb = jnp.broadcast_to(cond[:, :, None], (N_GROUPS_MOD, D_MODEL, LANE))
    n = MOD_CHUNKS * D_MODEL
    out = pl.pallas_call(
        _mod_kernel,
        grid=(DEPTH, D_MODEL // MOD_TK),
        in_specs=[pl.BlockSpec((N_GROUPS_MOD, D_MODEL, LANE), lambda l, k: (0, 0, 0)),
                  pl.BlockSpec((1, MOD_TK, n), lambda l, k: (l, k, 0)),
                  pl.BlockSpec((1, 1, n), lambda l, k: (l, 0, 0))],
        out_specs=pl.BlockSpec((1, SUBLANE, n), lambda l, k: (l, 0, 0)),
        out_shape=jax.ShapeDtypeStruct((DEPTH, SUBLANE, n), F32),
        scratch_shapes=[pltpu.VMEM((N_GROUPS_MOD, D_MODEL, LANE), F32)],
        compiler_params=_cparams(("arbitrary", "arbitrary")),
        name="adaln_mod",
    )(cb, w_mod, b_mod.reshape(DEPTH, 1, n))
    return out[:, :N_GROUPS_MOD].reshape(DEPTH, N_GROUPS_MOD, MOD_CHUNKS, D_MODEL)


IN_TM = 256
IN_CHUNK = 512


def _pair_specs(tm, block, rows_dim=0):
    nc = T_CTX // tm

    def index(row):
        return tuple(row if d == rows_dim else 0 for d in range(len(block)))

    return [pl.BlockSpec(block, lambda i, *_: index(jnp.minimum(i, nc - 1))),
            pl.BlockSpec(block, lambda i, *_: index(jnp.maximum(i - nc, 0)))]


def _pick(tm, ctx_ref, lat_ref):
    return jnp.where(pl.program_id(0) < T_CTX // tm, ctx_ref[...], lat_ref[...])


def _store_pair(tm, val, oc_ref, ol_ref):
    i = pl.program_id(0)

    @pl.when(i < T_CTX // tm)
    def _():
        oc_ref[...] = val

    @pl.when(i >= T_CTX // tm)
    def _():
        ol_ref[...] = val


def _in_kernel(*refs, pending):
    if pending:
        x1_ref, y_ref, modp_ref, mod_ref, g_ref, wa_ref, wb_ref, wc_ref, z_ref, oc_ref, ol_ref = refs
        x = x1_ref[...] + modp_ref[0, 5:6, :] * y_ref[...]
        _store_pair(IN_TM, x, oc_ref, ol_ref)
    else:
        xc_ref, xl_ref, mod_ref, g_ref, wa_ref, wb_ref, wc_ref, z_ref = refs
        x = _pick(IN_TM, xc_ref, xl_ref)
    shift = mod_ref[0, 0:1, :]
    scale = mod_ref[0, 1:2, :]
    h = (_rms(x, g_ref[...]) * (1.0 + scale) + shift).astype(BF16)
    for w_ref, z0 in ((wa_ref, 0), (wb_ref, Z_GU), (wc_ref, Z_MG)):
        width = w_ref.shape[1]
        for c0 in range(0, width, IN_CHUNK):
            cw = min(IN_CHUNK, width - c0)
            z_ref[:, z0 + c0:z0 + c0 + cw] = jnp.dot(h, w_ref[:, c0:c0 + cw], preferred_element_type=F32)


def _in_proj(x, mod_l, g, w_in_l):
    tm = IN_TM
    w_segs = [w_in_l[:, :W_GATES0].astype(BF16), w_in_l[:, W_GATES1:].astype(BF16),
              jnp.pad(w_in_l[:, W_GATES0:W_GATES1], ((0, 0), (0, Z_COLS - Z_MG - (W_GATES1 - W_GATES0)))).astype(BF16)]
    w_specs = [pl.BlockSpec(w.shape, lambda i: (0, 0), pipeline_mode=pl.Buffered(1)) for w in w_segs]
    pending = len(x) == 3
    row = lambda i: (i, 0)
    mod_spec = pl.BlockSpec((1, MOD_CHUNKS, D_MODEL), lambda i: (_mod_group(i, tm), 0, 0))
    if pending:
        x_specs = [pl.BlockSpec((tm, D_MODEL), row), pl.BlockSpec((tm, D_MODEL), row), mod_spec]
    else:
        x_specs = _pair_specs(tm, (tm, D_MODEL))
    out_specs = [pl.BlockSpec((tm, Z_COLS), row)]
    out_shape = [jax.ShapeDtypeStruct((T_ALL, Z_COLS), F32)]
    if pending:
        out_specs += _pair_specs(tm, (tm, D_MODEL))
        out_shape += [jax.ShapeDtypeStruct((T_CTX, D_MODEL), F32), jax.ShapeDtypeStruct((T_LAT, D_MODEL), F32)]
    outs = pl.pallas_call(
        functools.partial(_in_kernel, pending=pending),
        grid=(T_ALL // tm,),
        in_specs=x_specs + [
                  mod_spec,
                  pl.BlockSpec((1, D_MODEL), lambda i: (0, 0))] + w_specs,
        out_specs=out_specs,
        out_shape=out_shape,
        compiler_params=_cparams(("arbitrary",)),
        name="in_proj_res" if pending else "in_proj",
    )(*x, mod_l, g.reshape(1, D_MODEL), *w_segs)
    return (outs[0], (outs[1], outs[2])) if pending else (outs[0], x)


def _rope(t, c, s):
    lane = lax.broadcasted_iota(jnp.int32, t.shape, 1)
    first = (lane % (HEAD_DIM // 2)) < (HEAD_DIM // 4)
    swapped = jnp.where(first, pltpu.roll(t, HEAD_DIM - HEAD_DIM // 4, 1), pltpu.roll(t, HEAD_DIM // 4, 1))
    return t * c + swapped * s


def _attn_kernel(*refs, seq, past, rope):
    if rope:
        (aq_ref, ak_ref, av_ref, kp_ref, vp_ref, cos_ref, sin_ref, qg_ref, kg_ref,
         att_ref, kb_scr, vb_scr) = refs
    else:
        aq_ref, ak_ref, av_ref, qg_ref, kg_ref, att_ref, knew_ref, vnew_ref, kb_scr, vb_scr = refs
    kn = _rms(ak_ref[...], kg_ref[...])
    v = av_ref[...]
    if rope:
        kn = _rope(kn, cos_ref[...], sin_ref[...])
        kb_scr[seq:seq + past, :] = kp_ref[0, 0].astype(BF16)
        vb_scr[seq:seq + past, :] = vp_ref[0, 0].astype(BF16)
    else:
        knew_ref[0, 0] = kn
        vnew_ref[0, 0] = v
    kb_scr[0:seq, :] = kn.astype(BF16)
    vb_scr[0:seq, :] = v.astype(BF16)
    grp = ATT_HEADS // ATT_KV_HEADS

    def q_block(qb, carry):
        rows = pl.ds(pl.multiple_of(qb * CHUNK, CHUNK), CHUNK)
        qs = []
        for g in range(grp):
            q = _rms(aq_ref[rows, g * HEAD_DIM:(g + 1) * HEAD_DIM], qg_ref[...])
            if rope:
                q = _rope(q, cos_ref[rows, :], sin_ref[rows, :])
            qs.append(q)
        q2 = jnp.concatenate(qs, axis=0).astype(BF16)
        s = lax.dot_general(q2, kb_scr[...], (((1,), (1,)), ((), ())),
                            preferred_element_type=F32) * (HEAD_DIM ** -0.5)
        m = jnp.max(s, axis=-1, keepdims=True)
        p = jnp.exp(s - m)
        den = jnp.sum(p, axis=-1, keepdims=True)
        o = jnp.dot(p.astype(BF16), vb_scr[...], preferred_element_type=F32) / den
        for g in range(grp):
            att_ref[rows, g * HEAD_DIM:(g + 1) * HEAD_DIM] = o[g * CHUNK:(g + 1) * CHUNK].astype(BF16)
        return carry

    lax.fori_loop(0, seq // CHUNK, q_block, 0)


def _attention(z, att_out_shape, qg, kg, *, ctx, kpast=None, vpast=None, cos=None, sin=None):
    seq = SEQ if ctx else DEC_SEQ
    nb = BATCH if ctx else DEC_BATCH
    row0 = 0 if ctx else T_CTX // DEC_SEQ
    past = 0 if ctx else PAST_LEN
    qw = HEAD_DIM * (ATT_HEADS // ATT_KV_HEADS)
    in_specs = [pl.BlockSpec((seq, qw), lambda b, h: (row0 + b, Z_AQ // qw + h)),
                pl.BlockSpec((seq, HEAD_DIM), lambda b, h: (row0 + b, Z_AK // HEAD_DIM + h)),
                pl.BlockSpec((seq, HEAD_DIM), lambda b, h: (row0 + b, Z_AV // HEAD_DIM + h))]
    args = [z, z, z]
    if not ctx:
        in_specs += [pl.BlockSpec((1, 1, past, HEAD_DIM), lambda b, h: (b, h, 0, 0)),
                     pl.BlockSpec((1, 1, past, HEAD_DIM), lambda b, h: (b, h, 0, 0)),
                     pl.BlockSpec((seq, HEAD_DIM), lambda b, h: (0, 0)),
                     pl.BlockSpec((seq, HEAD_DIM), lambda b, h: (0, 0))]
        args += [kpast, vpast, cos, sin]
    in_specs += [pl.BlockSpec((1, HEAD_DIM), lambda b, h: (0, 0)),
                 pl.BlockSpec((1, HEAD_DIM), lambda b, h: (0, 0))]
    args += [qg.reshape(1, HEAD_DIM), kg.reshape(1, HEAD_DIM)]
    out_specs = [pl.BlockSpec((seq, qw), lambda b, h: (b, h))]
    out_shape = [jax.ShapeDtypeStruct((nb * seq, MIX_W), BF16)]
    if ctx:
        out_specs += [pl.BlockSpec((1, 1, seq, HEAD_DIM), lambda b, h: (b, h, 0, 0))] * 2
        out_shape += [jax.ShapeDtypeStruct((nb, ATT_KV_HEADS, seq, HEAD_DIM), F32)] * 2
    return pl.pallas_call(
        functools.partial(_attn_kernel, seq=seq, past=past, rope=not ctx),
        grid=(nb, ATT_KV_HEADS),
        in_specs=in_specs, out_specs=out_specs, out_shape=out_shape,
        scratch_shapes=[pltpu.VMEM((seq + past, HEAD_DIM), BF16),
                        pltpu.VMEM((seq + past, HEAD_DIM), BF16)],
        compiler_params=_cparams(("arbitrary", "arbitrary")),
        name="attn_ctx" if ctx else "attn_lat",
    )(*args)


def _s5_prep_kernel(are_ref, aim_ref, ldt_ref, bre_ref, bim_ref, pre_ref, pim_ref, wre_ref, wim_ref):
    a_re = are_ref[...]
    a_im = aim_ref[...]
    dt = jnp.exp(ldt_ref[...])
    pows = []
    for tau in range(S5_LAGS + 1):
        mag = jnp.exp((tau * dt) * a_re)
        ang = (tau * dt) * a_im
        pr, pi = mag * jnp.cos(ang), mag * jnp.sin(ang)
        pre_ref[tau] = pr
        pim_ref[tau] = pi
        pows.append((pr, pi))
    nr, ni = pows[1][0] - 1.0, pows[1][1]
    den = a_re * a_re + a_im * a_im
    cr = (nr * a_re + ni * a_im) / den
    ci = (ni * a_re - nr * a_im) / den
    for d in range(2):
        b_r, b_i = bre_ref[d], bim_ref[d]
        bb_r = cr[d:d + 1] * b_r - ci[d:d + 1] * b_i
        bb_i = cr[d:d + 1] * b_i + ci[d:d + 1] * b_r
        for tau in range(S5_LAGS):
            pr, pi = pows[tau][0][d:d + 1], pows[tau][1][d:d + 1]
            wre_ref[d, tau] = pr * bb_r - pi * bb_i
            wim_ref[d, tau] = pr * bb_i + pi * bb_r


def _s5_prep(a_re, a_im, log_dt, b_re, b_im, c_re, c_im):
    gp = S5_GROUPS * S5_STATE
    ldt = jnp.broadcast_to(log_dt[:, :, None], (2, S5_GROUPS, S5_STATE)).reshape(2, gp)
    bt = lambda b: b.transpose(0, 3, 1, 2).reshape(2, S5_CH, gp)
    pre, pim, wre, wim = pl.pallas_call(
        _s5_prep_kernel,
        out_shape=[jax.ShapeDtypeStruct((S5_LAGS + 1, 2, gp), F32)] * 2
        + [jax.ShapeDtypeStruct((2, S5_LAGS, S5_CH, gp), F32)] * 2,
        name="s5_prep",
    )(a_re.reshape(2, gp), a_im.reshape(2, gp), ldt, bt(b_re), bt(b_im))
    eye = jnp.eye(8, dtype=F32)

    hw = S5_SGW // 2
    half_mask = (jnp.arange(hw)[None, :] // S5_STATE == jnp.arange(4)[:, None]).astype(F32)

    def w_layout(w):
        w = w.reshape(2, S5_LAGS // 2, 2, S5_CH, S5_SG, 2, hw).transpose(0, 4, 5, 1, 2, 3, 6)
        w = w[:, :, :, :, :, None, :, :] * half_mask[:, None, :]
        return w.reshape(2, S5_SG, 2, S5_LAGS * LANE // 2, hw)

    w_in = jnp.concatenate([w_layout(wre), w_layout(wim)], axis=-1).astype(BF16)

    def c_layout(c):
        c = c.reshape(2, S5_SG, 8, S5_CH, S5_STATE).transpose(0, 1, 2, 4, 3)
        c = c[:, :, :, :, None, :] * eye[None, None, :, None, :, None]
        return c.reshape(2, S5_SG, S5_SGW, LANE)

    w_out = jnp.concatenate([c_layout(c_re), -c_layout(c_im)], axis=2).astype(BF16)

    def p_layout(p):
        p = p[1:].reshape(S5_LAGS, 2, S5_SG, S5_SGW).transpose(1, 2, 0, 3)
        return jnp.stack([p[0], p[1, :, ::-1]], axis=0)

    def a8_layout(p):
        p = p[S5_LAGS].reshape(2, S5_SG, 1, S5_SGW)
        return jnp.broadcast_to(p, (2, S5_SG, SUBLANE, S5_SGW))

    return w_in, w_out, a8_layout(pre), a8_layout(pim), p_layout(pre), p_layout(pim)


def _s5_kernel(u_ref, w_ref, c_ref, a8r_ref, a8i_ref, pwr_ref, pwi_ref, h0r_ref, h0i_ref,
               y_ref, hfr_ref, hfi_ref, upad, wbr, wbi, *, seq, tc, nseq):
    d = pl.program_id(0)
    zeros = jnp.zeros((SUBLANE, LANE), F32)
    for q in range(nseq):
        upad[q, 0:SUBLANE, :] = zeros
        upad[q, SUBLANE:seq + SUBLANE, :] = u_ref[q * seq:(q + 1) * seq, :]
        upad[q, seq + SUBLANE:seq + 2 * SUBLANE, :] = zeros
    nch = seq // tc
    nt = tc // SUBLANE
    a8r = a8r_ref[0, 0]
    a8i = a8i_ref[0, 0]

    def run(fwd):
        pwr = pwr_ref[0, 0]
        pwi = pwi_ref[0, 0]
        state = tuple((jnp.zeros((SUBLANE, S5_SGW), F32), jnp.zeros((SUBLANE, S5_SGW), F32))
                      for _ in range(nseq))
        for ci in range(nch):
            c = ci if fwd else nch - 1 - ci
            per_seq = []
            for q in range(nseq):
                if fwd:
                    win = upad[q, c * tc:c * tc + tc + SUBLANE, :]
                    per_seq.append([pltpu.roll(win, tau, 0)[SUBLANE:SUBLANE + tc] if tau
                                    else win[SUBLANE:SUBLANE + tc] for tau in range(S5_LAGS)])
                else:
                    win = upad[q, c * tc + SUBLANE:c * tc + tc + 2 * SUBLANE, :]
                    per_seq.append([pltpu.roll(win, tc + SUBLANE - tau, 0)[0:tc] if tau else win[0:tc]
                                    for tau in range(S5_LAGS)])
            lags = [jnp.concatenate([per_seq[q][tau] for q in range(nseq)], axis=0) if nseq > 1
                    else per_seq[0][tau] for tau in range(S5_LAGS)]
            low = lax.broadcasted_iota(jnp.int32, (nseq * tc, LANE), 1) < LANE // 2
            slabs = ([], [])
            for k in range(S5_LAGS // 2):
                a, b = lags[2 * k], lags[2 * k + 1]
                slabs[0].append(jnp.where(low, a, pltpu.roll(b, LANE // 2, 1)).astype(BF16))
                slabs[1].append(jnp.where(low, pltpu.roll(a, LANE // 2, 1), b).astype(BF16))
            hw = S5_SGW // 2
            for half in range(2):
                w = jnp.dot(jnp.concatenate(slabs[half], axis=1), w_ref[0, 0, half],
                            preferred_element_type=F32)
                wbr[:, half * hw:(half + 1) * hw] = w[:, :hw]
                wbi[:, half * hw:(half + 1) * hw] = w[:, hw:]
            if ci == 0:
                for q in range(nseq):
                    h0r = h0r_ref[q, 0, 0]
                    h0i = h0i_ref[q, 0, 0]
                    r0 = q * tc + (0 if fwd else tc - SUBLANE)
                    wbr[r0:r0 + SUBLANE, :] = wbr[r0:r0 + SUBLANE, :] + (pwr * h0r - pwi * h0i)
                    wbi[r0:r0 + SUBLANE, :] = wbi[r0:r0 + SUBLANE, :] + (pwr * h0i + pwi * h0r)

            def step(i, carry):
                t = i if fwd else nt - 1 - i
                new = []
                for q in range(nseq):
                    cr, ci_ = carry[q]
                    rows = pl.ds(pl.multiple_of(q * tc + t * SUBLANE, SUBLANE), SUBLANE)
                    nr = a8r * cr - a8i * ci_ + wbr[rows, :]
                    ni = a8r * ci_ + a8i * cr + wbi[rows, :]
                    wbr[rows, :] = nr
                    wbi[rows, :] = ni
                    new.append((nr, ni))
                return tuple(new)

            state = lax.fori_loop(0, nt, step, state)
            hcat = jnp.concatenate([wbr[...].astype(BF16), wbi[...].astype(BF16)], axis=1)
            y = jnp.dot(hcat, c_ref[0, 0], preferred_element_type=F32)
            for q in range(nseq):
                y_ref[0, q * seq + c * tc:q * seq + (c + 1) * tc, :] = y[q * tc:(q + 1) * tc]
        last = SUBLANE - 1 if fwd else 0
        for q in range(nseq):
            hfr_ref[q, 0, 0] = state[q][0][last:last + 1]
            hfi_ref[q, 0, 0] = state[q][1][last:last + 1]

    @pl.when(d == 0)
    def _():
        run(True)

    @pl.when(d == 1)
    def _():
        run(False)


def _s5_scan(z, prep, h0r, h0i, *, ctx):
    w_in, w_out, a8r, a8i, pwr, pwi = prep
    seq = SEQ if ctx else DEC_SEQ
    nb = BATCH if ctx else DEC_BATCH
    row0 = 0 if ctx else T_CTX // DEC_SEQ
    tc = min(seq, 512)
    nseq = S5_CTX_NSEQ if ctx else 1
    par = lambda d, s, b: (d, s, 0, 0)
    st = lambda d, s, b: (b, d, s, 0, 0)
    return pl.pallas_call(
        functools.partial(_s5_kernel, seq=seq, tc=tc, nseq=nseq),
        grid=(2, S5_SG, nb // nseq),
        in_specs=[pl.BlockSpec((nseq * seq, LANE), lambda d, s, b: (row0 + b, Z_SX // LANE + s)),
                  pl.BlockSpec((1, 1, 2, S5_LAGS * LANE // 2, S5_SGW), lambda d, s, b: (d, s, 0, 0, 0)),
                  pl.BlockSpec((1, 1, 2 * S5_SGW, LANE), par),
                  pl.BlockSpec((1, 1, SUBLANE, S5_SGW), par),
                  pl.BlockSpec((1, 1, SUBLANE, S5_SGW), par),
                  pl.BlockSpec((1, 1, SUBLANE, S5_SGW), par),
                  pl.BlockSpec((1, 1, SUBLANE, S5_SGW), par),
                  pl.BlockSpec((nseq, 1, 1, 1, S5_SGW), st),
                  pl.BlockSpec((nseq, 1, 1, 1, S5_SGW), st)],
        out_specs=[pl.BlockSpec((1, nseq * seq, LANE), lambda d, s, b: (d, b, s)),
                   pl.BlockSpec((nseq, 1, 1, 1, S5_SGW), st),
                   pl.BlockSpec((nseq, 1, 1, 1, S5_SGW), st)],
        out_shape=[jax.ShapeDtypeStruct((2, nb * seq, MIX_W), F32),
                   jax.ShapeDtypeStruct((nb, 2, S5_SG, 1, S5_SGW), F32),
                   jax.ShapeDtypeStruct((nb, 2, S5_SG, 1, S5_SGW), F32)],
        scratch_shapes=[pltpu.VMEM((nseq, seq + 2 * SUBLANE, LANE), F32),
                        pltpu.VMEM((nseq * tc, S5_SGW), F32),
                        pltpu.VMEM((nseq * tc, S5_SGW), F32)],
        compiler_params=_cparams(("arbitrary", "arbitrary", "arbitrary")),
        name="s5_ctx" if ctx else "s5_lat",
    )(z, w_in, w_out, a8r, a8i, pwr, pwi, h0r, h0i)


def _mlstm_kernel(q_ref, k_ref, v_ref, mo_ref, gc_ref, gr_ref, bc_ref, br_ref, c0_ref, n0_ref, m0_ref,
                  ng_ref, h_ref, c_ref, n_ref, m_ref, hs_scr, *, seq, chunk0):
    nc = seq // CHUNK
    ii = lax.broadcasted_iota(jnp.int32, (CHUNK, CHUNK), 0)
    jj = lax.broadcasted_iota(jnp.int32, (CHUNK, CHUNK), 1)
    neg_inf = jnp.float32(-jnp.inf)

    def chunk_step(hh, d, cidx, carry):
        cols = slice(hh * MLSTM_DIM, (hh + 1) * MLSTM_DIM)
        fwd = d == 0
        mask = (jj <= ii) if fwd else (jj >= ii)
        mask_t = (ii <= jj) if fwd else (ii >= jj)
        last = CHUNK - 1 if fwd else 0
        c_st, n_st, m_st = carry
        rows = pl.ds(pl.multiple_of(cidx * CHUNK, CHUNK), CHUNK)
        q = q_ref[rows, cols]
        k = k_ref[rows, cols] * (MLSTM_DIM ** -0.5)
        v = v_ref[rows, cols]
        gcol = gc_ref[hh, rows, :] + bc_ref[hh]
        grow = gr_ref[hh, chunk0 + cidx] + br_ref[hh]
        li_c = gcol[:, 2 * d:2 * d + 1]
        lf_c = jax.nn.log_sigmoid(gcol[:, 2 * d + 1:2 * d + 2])
        li_r = grow[2 * d:2 * d + 1, :]
        lf_r = jax.nn.log_sigmoid(grow[2 * d + 1:2 * d + 2, :])
        b_c = jnp.sum(jnp.where(mask, lf_r, 0.0), axis=1, keepdims=True)
        b_r = jnp.sum(jnp.where(mask_t, lf_c, 0.0), axis=0, keepdims=True)
        dmat = jnp.where(mask, b_c - b_r + li_r, neg_inf)
        inter = b_c + m_st
        m_j = jnp.maximum(inter, jnp.max(dmat, axis=1, keepdims=True))
        qb = q.astype(BF16)
        vb = v.astype(BF16)
        s = lax.dot_general(qb, k.astype(BF16), (((1,), (1,)), ((), ())),
                            preferred_element_type=F32) * jnp.exp(dmat - m_j)
        w_inter = jnp.exp(inter - m_j)
        num = (jnp.dot(s.astype(BF16), vb, preferred_element_type=F32)
               + w_inter * jnp.dot(qb, c_st.astype(BF16), preferred_element_type=F32))
        den = jnp.sum(s, axis=1, keepdims=True) + w_inter * jnp.sum(q * n_st, axis=1, keepdims=True)
        hs_scr[d, rows, cols] = num / jnp.maximum(jnp.abs(den), jnp.exp(-m_j))
        m_end = m_j[last:last + 1, :]
        b_end = b_c[last:last + 1, :]
        w_c = jnp.exp(b_end - b_c + li_c - m_end)
        decay = jnp.exp(b_end + m_st - m_end)
        kw = k * w_c
        c_new = decay * c_st + lax.dot_general(kw.astype(BF16), vb, (((0,), (0,)), ((), ())),
                                               preferred_element_type=F32)
        n_new = decay * n_st + jnp.sum(kw, axis=0, keepdims=True)
        return c_new, n_new, m_end

    chains = [(hh, d) for hh in range(MLSTM_HG) for d in range(2)]

    def all_chains(ci, carry):
        return tuple(chunk_step(hh, d, ci if d == 0 else nc - 1 - ci, carry[n])
                     for n, (hh, d) in enumerate(chains))

    init = tuple((c0_ref[0, d, hh], n0_ref[0, d, hh], m0_ref[0, d, hh]) for hh, d in chains)
    final = lax.fori_loop(0, nc, all_chains, init)
    for n, (hh, d) in enumerate(chains):
        c_ref[0, d, hh], n_ref[0, d, hh], m_ref[0, d, hh] = final[n]
    for hh in range(MLSTM_HG):
        cols = slice(hh * MLSTM_DIM, (hh + 1) * MLSTM_DIM)
        h = _rms(hs_scr[0, :, cols] + hs_scr[1, :, cols], ng_ref[hh])
        h_ref[:, cols] = (h * jax.nn.sigmoid(mo_ref[:, cols])).astype(BF16)


def _mlstm(z, gcol, grow, bias_c, bias_r, c0, n0, m0, norm_g, *, ctx):
    seq = SEQ if ctx else DEC_SEQ
    nb = BATCH if ctx else DEC_BATCH
    row0 = 0 if ctx else T_CTX // DEC_SEQ
    hd = MLSTM_DIM
    nh = MLSTM_HEADS
    hg = MLSTM_HG
    hw = hg * hd
    zcol = lambda off: (lambda b, h: (row0 + b, off // hw + h))
    st5 = lambda b, h: (b, 0, h, 0, 0)
    return pl.pallas_call(
        functools.partial(_mlstm_kernel, seq=seq, chunk0=0),
        grid=(nb, nh // hg),
        in_specs=[pl.BlockSpec((seq, hw), zcol(Z_MQ)),
                  pl.BlockSpec((seq, hw), zcol(Z_MK)),
                  pl.BlockSpec((seq, hw), zcol(Z_MV)),
                  pl.BlockSpec((seq, hw), zcol(Z_MO)),
                  pl.BlockSpec((hg, seq, 4), lambda b, h: (h, row0 + b, 0)),
                  pl.BlockSpec((hg, seq // CHUNK, 4, CHUNK), lambda b, h: (h, row0 + b, 0, 0)),
                  pl.BlockSpec((hg, 1, 4), lambda b, h: (h, 0, 0)),
                  pl.BlockSpec((hg, 4, 1), lambda b, h: (h, 0, 0)),
                  pl.BlockSpec((1, 2, hg, hd, hd), st5),
                  pl.BlockSpec((1, 2, hg, 1, hd), st5),
                  pl.BlockSpec((1, 2, hg, 1, 1), st5),
                  pl.BlockSpec((hg, 1, hd), lambda b, h: (h, 0, 0))],
        out_specs=[pl.BlockSpec((seq, hw), lambda b, h: (b, h)),
                   pl.BlockSpec((1, 2, hg, hd, hd), st5),
                   pl.BlockSpec((1, 2, hg, 1, hd), st5),
                   pl.BlockSpec((1, 2, hg, 1, 1), st5)],
        out_shape=[jax.ShapeDtypeStruct((nb * seq, MIX_W), BF16),
                   jax.ShapeDtypeStruct((nb, 2, nh, hd, hd), F32),
                   jax.ShapeDtypeStruct((nb, 2, nh, 1, hd), F32),
                   jax.ShapeDtypeStruct((nb, 2, nh, 1, 1), F32)],
        scratch_shapes=[pltpu.VMEM((2, seq, hw), F32)],
        compiler_params=_cparams(("arbitrary", "arbitrary")),
        name="mlstm_ctx" if ctx else "mlstm_lat",
    )(z, z, z, z, gcol, grow, bias_c, bias_r, c0, n0, m0, norm_g.reshape(nh, 1, hd))


OUT_TM = 256


def _out_kernel(*refs, router):
    (ac_ref, al_ref, yc_ref, yl_ref, sx_ref, sd_ref, sw_ref, sb_ref, cc_ref, cl_ref,
     gu_ref, gv_ref, gn_ref, gw_ref, gb_ref, xc_ref, xl_ref, mod_ref, g_ref, w_ref), refs = refs[:20], refs[20:]
    if router:
        rw_ref, rb_ref, x1_ref, h2_ref, rt_ref = refs
    else:
        x1_ref, h2_ref = refs
    y2 = _pick(OUT_TM, yc_ref, yl_ref)
    y = jax.nn.gelu(y2[0] + y2[1] + sd_ref[...] * sx_ref[...])
    gate = jnp.dot(y.astype(BF16), sw_ref[...], preferred_element_type=F32) + sb_ref[...]
    s5_out = (y * jax.nn.sigmoid(gate)).astype(BF16)
    vn = _rms(gv_ref[...], gn_ref[...]).astype(BF16)
    gw = MIX_W // GMLP_GROUPS
    chunks = []
    for c in range(OUT_TM // CHUNK):
        r = slice(c * CHUNK, (c + 1) * CHUNK)
        groups = []
        for g in range(GMLP_GROUPS):
            cs = slice(g * gw, (g + 1) * gw)
            mixed = jnp.dot(gw_ref[g], vn[r, cs], preferred_element_type=F32) + gb_ref[:, g:g + 1]
            groups.append((gu_ref[r, cs] * mixed).astype(BF16))
        chunks.append(jnp.concatenate(groups, axis=1))
    gm_out = jnp.concatenate(chunks, axis=0)
    mix = jnp.dot(_pick(OUT_TM, ac_ref, al_ref), w_ref[0:MIX_W, :], preferred_element_type=F32)
    mix += jnp.dot(s5_out, w_ref[MIX_W:2 * MIX_W, :], preferred_element_type=F32)
    mix += jnp.dot(_pick(OUT_TM, cc_ref, cl_ref), w_ref[2 * MIX_W:3 * MIX_W, :], preferred_element_type=F32)
    mix += jnp.dot(gm_out, w_ref[3 * MIX_W:4 * MIX_W, :], preferred_element_type=F32)
    x1 = _pick(OUT_TM, xc_ref, xl_ref) + mod_ref[0, 2:3, :] * mix
    x1_ref[...] = x1
    h2 = _rms(x1, g_ref[...]) * (1.0 + mod_ref[0, 4:5, :]) + mod_ref[0, 3:4, :]
    hi = h2.astype(BF16)
    h2_ref[...] = h2 if router else hi
    if router:
        lo = (h2 - hi.astype(F32)).astype(BF16)
        both = jnp.dot(hi, rw_ref[...], preferred_element_type=F32)
        logits = (both[:, :LANE] + jnp.dot(lo, rw_ref[:, :LANE], preferred_element_type=F32)
                  + both[:, LANE:]) + rb_ref[...]
        lane = lax.broadcasted_iota(jnp.int32, logits.shape, 1)
        neg_inf = jnp.float32(-jnp.inf)
        lg = jnp.where(lane < N_EXPERTS, logits, neg_inf)
        m1 = jnp.max(lg, axis=-1, keepdims=True)
        i1 = jnp.min(jnp.where(lg == m1, lane, LANE), axis=-1, keepdims=True)
        lg2 = jnp.where(lane == i1, neg_inf, lg)
        m2 = jnp.max(lg2, axis=-1, keepdims=True)
        i2 = jnp.min(jnp.where(lg2 == m2, lane, LANE), axis=-1, keepdims=True)
        e = jnp.exp(m2 - m1)
        w1 = 1.0 / (1.0 + e)
        w2 = e / (1.0 + e)
        rt = jnp.where(lane == 0, i1.astype(F32), 0.0)
        rt = jnp.where(lane == 1, i2.astype(F32), rt)
        rt = jnp.where(lane == 2, w1, rt)
        rt = jnp.where(lane == 3, w2, rt)
        rt_ref[...] = rt


def _out_proj(z, att_pair, s5_args, ml_pair, gm_args, x_pair, mod_l, g2, w_out, router=None):
    tm = OUT_TM
    row = lambda i: (i, 0)
    fixed = lambda i: (0, 0)
    pair = _pair_specs(tm, (tm, MIX_W))
    zcols = lambda off: pl.BlockSpec((tm, MIX_W), lambda i: (i, off // MIX_W))
    vec = pl.BlockSpec((1, MIX_W), fixed)
    y_pair, d_skip, w_glu, b_glu = s5_args
    gm_norm_g, gm_w_s, gm_b_s = gm_args
    in_specs = (pair + _pair_specs(tm, (2, tm, MIX_W), rows_dim=1)
                + [zcols(Z_SX), vec, pl.BlockSpec((MIX_W, MIX_W), fixed), vec]
                + pair
                + [zcols(Z_GU), zcols(Z_GV), vec, pl.BlockSpec((GMLP_GROUPS, CHUNK, CHUNK), lambda i: (0, 0, 0)),
                   pl.BlockSpec((CHUNK, GMLP_GROUPS), fixed)]
                + _pair_specs(tm, (tm, D_MODEL))
                + [pl.BlockSpec((1, MOD_CHUNKS, D_MODEL), lambda i: (_mod_group(i, tm), 0, 0)),
                   pl.BlockSpec((1, D_MODEL), fixed),
                   pl.BlockSpec((D_MODEL, D_MODEL), fixed, pipeline_mode=pl.Buffered(1))])
    args = [*att_pair, *y_pair, z, d_skip.reshape(1, MIX_W), w_glu.astype(BF16), b_glu.reshape(1, MIX_W),
            *ml_pair, z, z, gm_norm_g.reshape(1, MIX_W), gm_w_s.astype(BF16), gm_b_s.T,
            *x_pair, mod_l, g2.reshape(1, D_MODEL), w_out]
    out_specs = [pl.BlockSpec((tm, D_MODEL), row), pl.BlockSpec((tm, D_MODEL), row)]
    out_shape = [jax.ShapeDtypeStruct((T_ALL, D_MODEL), F32),
                 jax.ShapeDtypeStruct((T_ALL, D_MODEL), BF16 if router is None else F32)]
    if router is not None:
        rw, rb = router
        rw = jnp.pad(rw, ((0, 0), (0, LANE - N_EXPERTS)))
        rh = rw.astype(BF16)
        rl = (rw - rh.astype(F32)).astype(BF16)
        in_specs += [pl.BlockSpec((D_MODEL, 2 * LANE), fixed), pl.BlockSpec((1, LANE), fixed)]
        args += [jnp.concatenate([rh, rl], axis=1), jnp.pad(rb, (0, LANE - N_EXPERTS)).reshape(1, LANE)]
        out_specs.append(pl.BlockSpec((tm, LANE), row))
        out_shape.append(jax.ShapeDtypeStruct((T_ALL, LANE), F32))
    return pl.pallas_call(
        functools.partial(_out_kernel, router=router is not None),
        grid=(T_ALL // tm,),
        in_specs=in_specs, out_specs=out_specs, out_shape=out_shape,
        compiler_params=_cparams(("arbitrary",)),
        name="out_proj_router" if router is not None else "out_proj",
    )(*args)


def _ffn_kernel(x_ref, wg_ref, wu_ref, wd_ref, o_ref):
    @pl.when(pl.program_id(1) == 0)
    def _():
        o_ref[...] = jnp.zeros_like(o_ref)

    def sub(s, carry):
        rows = pl.ds(pl.multiple_of(s * FFN_DENSE_SUB, FFN_DENSE_SUB), FFN_DENSE_SUB)
        xs = x_ref[rows, :]
        g = jnp.dot(xs, wg_ref[...], preferred_element_type=F32)
        u = jnp.dot(xs, wu_ref[...], preferred_element_type=F32)
        a = (g * jax.nn.sigmoid(g) * u).astype(BF16)
        o_ref[rows, :] += jnp.dot(a, wd_ref[...], preferred_element_type=F32)
        return carry

    lax.fori_loop(0, FFN_TM // FFN_DENSE_SUB, sub, 0)


def _ffn(x, w_gate, w_up, w_down):
    return pl.pallas_call(
        _ffn_kernel,
        grid=(T_ALL // FFN_TM, D_FF // FFN_TF),
        in_specs=[pl.BlockSpec((FFN_TM, D_MODEL), lambda i, f: (i, 0)),
                  pl.BlockSpec((D_MODEL, FFN_TF), lambda i, f: (0, f)),
                  pl.BlockSpec((D_MODEL, FFN_TF), lambda i, f: (0, f)),
                  pl.BlockSpec((FFN_TF, D_MODEL), lambda i, f: (f, 0))],
        out_specs=pl.BlockSpec((FFN_TM, D_MODEL), lambda i, f: (i, 0)),
        out_shape=jax.ShapeDtypeStruct((T_ALL, D_MODEL), F32),
        compiler_params=_cparams(("arbitrary", "arbitrary")),
        name="ffn",
    )(x, w_gate, w_up, w_down)


def _moe_ffn_kernel(te_ref, ns_ref, src_ref, x_hbm, wg_ref, wu_ref, wd_ref, o_ref, xf, xb, sems):
    i = pl.program_id(0)
    f = pl.program_id(1)
    ntiles = pl.num_programs(0)

    def row_copy(src_row, slot, r):
        return pltpu.make_async_copy(x_hbm.at[pl.ds(src_row, 1)], xf.at[slot, pl.ds(r, 1)], sems.at[slot])

    def issue(tile, slot):
        def body(r8, carry):
            for u in range(ISSUE_UNROLL):
                r = r8 * ISSUE_UNROLL + u
                row_copy(src_ref[tile * FFN_TM + r], slot, r).start()
            return carry

        lax.fori_loop(0, ns_ref[tile] * (FFN_SUB // ISSUE_UNROLL), body, 0)

    def wait_rows(tile, slot):
        def body(s, carry):
            rows = pl.ds(pl.multiple_of(s * FFN_SUB, FFN_SUB), FFN_SUB)
            pltpu.make_async_copy(x_hbm.at[pl.ds(0, FFN_SUB)], xf.at[slot, rows], sems.at[slot]).wait()
            return carry

        lax.fori_loop(0, ns_ref[tile], body, 0)

    @pl.when(f == 0)
    def _():
        slot = i % 2

        @pl.when(i == 0)
        def _():
            issue(0, 0)

        wait_rows(i, slot)

        @pl.when(i + 1 < ntiles)
        def _():
            issue(i + 1, 1 - slot)

        def cast(s, carry):
            rows = pl.ds(pl.multiple_of(s * FFN_SUB, FFN_SUB), FFN_SUB)
            xb[rows, :] = xf[slot, rows, :].astype(BF16)
            return carry

        lax.fori_loop(0, ns_ref[i], cast, 0)
        o_ref[...] = jnp.zeros_like(o_ref)

    ns = ns_ref[i]

    def block(row0, size):
        rows = pl.ds(pl.multiple_of(row0, size), size)
        xs = xb[rows, :]
        g = jnp.dot(xs, wg_ref[0], preferred_element_type=F32)
        u = jnp.dot(xs, wu_ref[0], preferred_element_type=F32)
        a = (g * jax.nn.sigmoid(g) * u).astype(BF16)
        o_ref[rows, :] += jnp.dot(a, wd_ref[0], preferred_element_type=F32)

    pairs = ns // 2

    def pair(s, carry):
        block(s * (2 * FFN_SUB), 2 * FFN_SUB)
        return carry

    lax.fori_loop(0, pairs, pair, 0)

    @pl.when(ns % 2 == 1)
    def _():
        block(pairs * (2 * FFN_SUB), FFN_SUB)


def _moe_ffn(h2, tile_e, tile_ns, src, w_gate, w_up, w_down):
    tf = FFN_TF
    nf = D_FF // tf

    def fcol(i, f, ns):
        return jnp.where(ns[i] > 0, f, nf - 1)

    return pl.pallas_call(
        _moe_ffn_kernel,
        grid_spec=pltpu.PrefetchScalarGridSpec(
            num_scalar_prefetch=3,
            grid=(MOE_TILES, nf),
            in_specs=[pl.BlockSpec(memory_space=pl.ANY),
                      pl.BlockSpec((1, D_MODEL, tf), lambda i, f, te, ns, sr: (te[i], 0, fcol(i, f, ns))),
                      pl.BlockSpec((1, D_MODEL, tf), lambda i, f, te, ns, sr: (te[i], 0, fcol(i, f, ns))),
                      pl.BlockSpec((1, tf, D_MODEL), lambda i, f, te, ns, sr: (te[i], fcol(i, f, ns), 0))],
            out_specs=pl.BlockSpec((FFN_TM, D_MODEL), lambda i, f, te, ns, sr: (i, 0)),
            scratch_shapes=[pltpu.VMEM((2, FFN_TM, D_MODEL), F32),
                            pltpu.VMEM((FFN_TM, D_MODEL), BF16),
                            pltpu.SemaphoreType.DMA((2,))]),
        out_shape=jax.ShapeDtypeStruct((MOE_TILES * FFN_TM, D_MODEL), F32),
        compiler_params=_cparams(("arbitrary", "arbitrary")),
        name="moe_ffn",
    )(tile_e, tile_ns, src, h2, w_gate, w_up, w_down)


RES_TM = 256


def _res_kernel(x_ref, y_ref, mod_ref, g_ref, oc_ref, ol_ref, *, final):
    x2 = x_ref[...] + mod_ref[0, 5:6, :] * y_ref[...]
    _store_pair(RES_TM, _rms(x2, g_ref[...]) if final else x2, oc_ref, ol_ref)


def _moe_res_kernel(slot_ref, x_ref, rt_ref, mod_ref, g_ref, ys_hbm, oc_ref, ol_ref, ybuf, sems, *, final):
    i = pl.program_id(0)
    nt = pl.num_programs(0)

    def row_copy(src_row, buf, k, r):
        return pltpu.make_async_copy(ys_hbm.at[pl.ds(src_row, 1)], ybuf.at[buf, k, pl.ds(r, 1)], sems.at[buf])

    def issue(tile, buf):
        def body(r8, carry):
            for u in range(ISSUE_UNROLL):
                r = r8 * ISSUE_UNROLL + u
                for k in range(2):
                    row_copy(slot_ref[k * T_ALL + tile * RES_TM + r], buf, k, r).start()
            return carry

        lax.fori_loop(0, RES_TM // ISSUE_UNROLL, body, 0)

    buf = i % 2

    @pl.when(i == 0)
    def _():
        issue(0, 0)

    for k in range(2):
        pltpu.make_async_copy(ys_hbm.at[pl.ds(0, RES_TM)], ybuf.at[buf, k], sems.at[buf]).wait()

    @pl.when(i + 1 < nt)
    def _():
        issue(i + 1, 1 - buf)

    rt = rt_ref[...]
    ffn = rt[:, 2:3] * ybuf[buf, 0] + rt[:, 3:4] * ybuf[buf, 1]
    x2 = x_ref[...] + mod_ref[0, 5:6, :] * ffn
    _store_pair(RES_TM, _rms(x2, g_ref[...]) if final else x2, oc_ref, ol_ref)


def _residual(x1, y, mod_l, final_g, *, route=None, slot=None, final):
    tm = RES_TM
    nt = T_ALL // tm
    out_shape = [jax.ShapeDtypeStruct((T_CTX, D_MODEL), F32), jax.ShapeDtypeStruct((T_LAT, D_MODEL), F32)]
    out_specs = _pair_specs(tm, (tm, D_MODEL))
    if route is None:
        row = lambda i: (i, 0)
        return pl.pallas_call(
            functools.partial(_res_kernel, final=final),
            grid=(nt,),
            in_specs=[pl.BlockSpec((tm, D_MODEL), row), pl.BlockSpec((tm, D_MODEL), row),
                      pl.BlockSpec((1, MOD_CHUNKS, D_MODEL), lambda i: (_mod_group(i, tm), 0, 0)),
                      pl.BlockSpec((1, D_MODEL), lambda i: (0, 0))],
            out_specs=out_specs,
            out_shape=out_shape,
            compiler_params=_cparams(("arbitrary",)),
            name="ffn_residual",
        )(x1, y, mod_l, final_g.reshape(1, D_MODEL))
    row = lambda i, sl: (i, 0)
    return pl.pallas_call(
        functools.partial(_moe_res_kernel, final=final),
        grid_spec=pltpu.PrefetchScalarGridSpec(
            num_scalar_prefetch=1,
            grid=(nt,),
            in_specs=[pl.BlockSpec((tm, D_MODEL), row), pl.BlockSpec((tm, LANE), row),
                      pl.BlockSpec((1, MOD_CHUNKS, D_MODEL), lambda i, sl: (_mod_group(i, tm), 0, 0)),
                      pl.BlockSpec((1, D_MODEL), lambda i, sl: (0, 0)),
                      pl.BlockSpec(memory_space=pl.ANY)],
            out_specs=out_specs,
            scratch_shapes=[pltpu.VMEM((2, 2, tm, D_MODEL), F32), pltpu.SemaphoreType.DMA((2,))]),
        out_shape=out_shape,
        compiler_params=_cparams(("arbitrary",)),
        name="moe_residual",
    )(slot, x1, route, mod_l, final_g.reshape(1, D_MODEL), y)


def _route_plan(route):
    e_flat = jnp.concatenate([route[:, 0], route[:, 1]]).astype(jnp.int32)
    onehot = (e_flat[:, None] == jnp.arange(N_EXPERTS, dtype=jnp.int32)[None, :]).astype(jnp.int32)
    ranks = jnp.cumsum(onehot, axis=0) - onehot
    rank = jnp.sum(ranks * onehot, axis=1)
    counts = jnp.sum(onehot, axis=0)
    tiles = (counts + FFN_TM - 1) // FFN_TM
    tile_start = jnp.cumsum(tiles) - tiles
    slot = tile_start[e_flat] * FFN_TM + rank
    n_slots = MOE_TILES * FFN_TM
    tok = jnp.concatenate([jnp.arange(T_ALL, dtype=jnp.int32)] * 2)
    src = jnp.zeros((n_slots,), jnp.int32).at[slot].set(tok, unique_indices=True)
    tile_ids = jnp.arange(MOE_TILES, dtype=jnp.int32)
    ends = jnp.cumsum(tiles)
    tile_e = jnp.sum((tile_ids[:, None] >= ends[None, :]).astype(jnp.int32), axis=1)
    used = tile_e < N_EXPERTS
    last_e = jnp.max(jnp.where(counts > 0, jnp.arange(N_EXPERTS, dtype=jnp.int32), 0))
    tile_e = jnp.where(used, tile_e, last_e)
    rows_in_tile = jnp.clip(counts[tile_e] - (tile_ids - tile_start[tile_e]) * FFN_TM, 0, FFN_TM)
    tile_ns = jnp.where(used, (rows_in_tile + FFN_SUB - 1) // FFN_SUB, 0).astype(jnp.int32)
    return src, slot, tile_e.astype(jnp.int32), tile_ns


def _rope_tables():
    length = DEC_SEQ
    r = jnp.repeat(jnp.arange(length // GRID_W, dtype=F32), GRID_W)
    col = (jnp.arange(length) % GRID_W).astype(F32)
    half = HEAD_DIM // 2
    inv = ROPE_THETA ** (-jnp.arange(0, half, 2, dtype=F32) / half)
    ar, ac = r[:, None] * inv, col[:, None] * inv
    cos = jnp.concatenate([jnp.cos(ar), jnp.cos(ar), jnp.cos(ac), jnp.cos(ac)], axis=-1)
    sin = jnp.concatenate([-jnp.sin(ar), jnp.sin(ar), -jnp.sin(ac), jnp.sin(ac)], axis=-1)
    return cos, sin


def kernel(x_prompt, x_sample, c, c_ctx, cache_attn_k, cache_attn_v, state_s5_re, state_s5_im, state_mlstm_c, state_mlstm_n, state_mlstm_m, norm1_g, norm2_g, w_mod, b_mod, w_in, w_out, q_norm_g, k_norm_g, s5_a_re, s5_a_im, s5_log_dt, s5_b_re, s5_b_im, s5_c_re, s5_c_im, s5_d, s5_w_glu, s5_b_glu, mlstm_i_bias, mlstm_f_bias, mlstm_norm_g, gmlp_norm_g, gmlp_w_s, gmlp_b_s, ffn_w_gate, ffn_w_up, ffn_w_down, moe_router, moe_router_bias, moe_w_gate, moe_w_up, moe_w_down, final_norm_g):
    x = (x_prompt.reshape(T_CTX, D_MODEL), x_sample.reshape(T_LAT, D_MODEL))
    cond = jnp.concatenate([c_ctx[None, :], c], axis=0)
    mod = _modulation(cond, w_mod, b_mod)
    cos, sin = _rope_tables()
    nh = MLSTM_HEADS
    zeros_s5 = jnp.zeros((BATCH, 2, S5_SG, 1, S5_SGW), F32)
    zeros_c = jnp.zeros((BATCH, 2, nh, MLSTM_DIM, MLSTM_DIM), F32)
    zeros_n = jnp.zeros((BATCH, 2, nh, 1, MLSTM_DIM), F32)
    zeros_m = jnp.zeros((BATCH, 2, nh, 1, 1), F32)

    ctx_states = []
    for l in range(DEPTH):
        use_moe = l % 2 == 1
        j = l // 2
        z, x = _in_proj(x, mod[l], norm1_g[l], w_in[l])

        att_c, k_new, v_new = _attention(z, None, q_norm_g[l], k_norm_g[l], ctx=True)
        (att_l,) = _attention(z, None, q_norm_g[l], k_norm_g[l], ctx=False,
                              kpast=cache_attn_k[:, l], vpast=cache_attn_v[:, l], cos=cos, sin=sin)

        prep = _s5_prep(s5_a_re[l], s5_a_im[l], s5_log_dt[l], s5_b_re[l], s5_b_im[l], s5_c_re[l], s5_c_im[l])
        y_c, hf_re, hf_im = _s5_scan(z, prep, zeros_s5, zeros_s5, ctx=True)
        st = lambda s: s[:, l].reshape(DEC_BATCH, 2, S5_SG, 1, S5_SGW)
        y_l, _, _ = _s5_scan(z, prep, st(state_s5_re), st(state_s5_im), ctx=False)
        s5_args = ((y_c, y_l), s5_d[l], s5_w_glu[l], s5_b_glu[l])

        mg = z[:, Z_MG:Z_MG + 16].reshape(T_ALL, 2, 2, nh)
        gcol = mg.transpose(3, 0, 1, 2).reshape(nh, T_ALL, 4)
        grow = mg.reshape(T_ALL // CHUNK, CHUNK, 4, nh).transpose(3, 0, 2, 1)
        bias = jnp.stack([mlstm_i_bias[l], mlstm_f_bias[l]], axis=1)
        bias_c = bias.transpose(2, 0, 1).reshape(nh, 1, 4)
        bias_r = bias_c.reshape(nh, 4, 1)
        ml_c_out, c_new, n_new, m_new = _mlstm(z, gcol, grow, bias_c, bias_r, zeros_c, zeros_n, zeros_m,
                                               mlstm_norm_g[l], ctx=True)
        ml_l_out, _, _, _ = _mlstm(z, gcol, grow, bias_c, bias_r, state_mlstm_c[:, l],
                                   state_mlstm_n[:, l].reshape(DEC_BATCH, 2, nh, 1, MLSTM_DIM),
                                   state_mlstm_m[:, l].reshape(DEC_BATCH, 2, nh, 1, 1),
                                   mlstm_norm_g[l], ctx=False)

        gm_args = (gmlp_norm_g[l], gmlp_w_s[l], gmlp_b_s[l])

        ctx_states.append((k_new, v_new,
                           hf_re.reshape(BATCH, 2, S5_GROUPS, S5_STATE), hf_im.reshape(BATCH, 2, S5_GROUPS, S5_STATE),
                           c_new, n_new.reshape(BATCH, 2, nh, MLSTM_DIM), m_new.reshape(BATCH, 2, nh)))

        w_o = w_out[l].astype(BF16)
        final = l == DEPTH - 1
        mixed = (z, (att_c, att_l), s5_args, (ml_c_out, ml_l_out), gm_args)
        if use_moe:
            x1, h2, route = _out_proj(*mixed, x, mod[l], norm2_g[l], w_o,
                                      router=(moe_router[j], moe_router_bias[j]))
            src, slot, tile_e, tile_ns = _route_plan(route)
            ys = _moe_ffn(h2, tile_e, tile_ns, src, moe_w_gate[j].astype(BF16), moe_w_up[j].astype(BF16),
                          moe_w_down[j].astype(BF16))
            x = _residual(x1, ys, mod[l], final_norm_g, route=route, slot=slot, final=final)
        else:
            x1, h2 = _out_proj(*mixed, x, mod[l], norm2_g[l], w_o)
            y = _ffn(h2, ffn_w_gate[j].astype(BF16), ffn_w_up[j].astype(BF16), ffn_w_down[j].astype(BF16))
            if final:
                x = _residual(x1, y, mod[l], final_norm_g, final=True)
            else:
                x = (x1, y, mod[l])

    y_prompt = x[0].reshape(BATCH, SEQ, D_MODEL)
    y_sample = x[1].reshape(DEC_BATCH, DEC_SEQ, D_MODEL)
    stack = lambda i: jnp.stack([s[i] for s in ctx_states], axis=1)
    return (y_prompt, y_sample, stack(0), stack(1), stack(2), stack(3), stack(4), stack(5), stack(6))
```

```python
import functools

import jax
import jax.numpy as jnp
from jax import lax
from jax.experimental import pallas as pl
from jax.experimental.pallas import tpu as pltpu

F32 = jnp.float32
BF16 = jnp.bfloat16

D_MODEL = 2048
BATCH = 16
SEQ = 256
DEPTH = 2
DEC_BATCH = 2
DEC_SEQ = 2048
PAST_LEN = 256
GRID_W = 64
MIX_W = 512
ATT_HEADS = 4
ATT_KV_HEADS = 2
HEAD_DIM = 128
ROPE_THETA = 10000.0
S5_CH = 16
S5_GROUPS = 32
S5_STATE = 64
MLSTM_HEADS = 4
MLSTM_DIM = 128
MLSTM_HG = 1
CHUNK = 128
GMLP_GROUPS = 4
D_FF = 7168
N_EXPERTS = 8
MOD_CHUNKS = 6
EPS = 1e-6
LOG2_E = 1.4426950408889634

T_CTX = BATCH * SEQ
T_LAT = DEC_BATCH * DEC_SEQ
T_ALL = T_CTX + T_LAT
N_GROUPS_MOD = 1 + DEC_BATCH

Z_AQ, Z_AK, Z_AV, Z_SX = 0, 512, 768, 1024
Z_MQ, Z_MK, Z_MV, Z_MO = 1536, 2048, 2560, 3072
Z_GU, Z_GV, Z_MG = 3584, 4096, 4608
Z_COLS = 4736
W_GATES0, W_GATES1 = 3584, 3600
LANE = 128
SUBLANE = 8

VMEM_LIMIT = 56 * 1024 * 1024

S5_SG = 4
S5_SGW = 8 * S5_STATE
S5_LAGS = 8
S5_CTX_NSEQ = 1

FFN_TM = 1024
FFN_SUB = 256
FFN_DENSE_SUB = 512
FFN_TF = 512
MOE_TILES = 2 * T_ALL // FFN_TM + N_EXPERTS
ISSUE_UNROLL = 8


def _cparams(sem=None):
    return pltpu.CompilerParams(dimension_semantics=sem, vmem_limit_bytes=VMEM_LIMIT)


def _mod_group(i, tm):
    return jnp.maximum(i * tm // DEC_SEQ - (T_CTX // DEC_SEQ - 1), 0)


def _rms(x, g):
    return x * lax.rsqrt(jnp.mean(x * x, axis=-1, keepdims=True) + EPS) * g


MOD_TK = 256


def _mod_kernel(cb_ref, w_ref, b_ref, o_ref, silu_scr):
    k = pl.program_id(1)
    n = MOD_CHUNKS * D_MODEL

    @pl.when((pl.program_id(0) == 0) & (k == 0))
    def _():
        c = cb_ref[...]
        silu_scr[...] = c * jax.nn.sigmoid(c)

    @pl.when(k == 0)
    def _():
        o_ref[0] = jnp.broadcast_to(b_ref[0], (SUBLANE, n))

    rows = pl.ds(pl.multiple_of(k * MOD_TK, MOD_TK), MOD_TK)
    for r in range(N_GROUPS_MOD):
        s = silu_scr[r, rows, :]
        for j in range(n // LANE):
            cols = slice(j * LANE, (j + 1) * LANE)
            o_ref[0, r:r + 1, cols] += jnp.sum(w_ref[0, :, cols] * s, axis=0, keepdims=True)


def _modulation(cond, w_mod, b_mod):
    cb = jnp.broadcast_to(cond[:, :, None], (N_GROUPS_MOD, D_MODEL, LANE))
    n = MOD_CHUNKS * D_MODEL
    out = pl.pallas_call(
        _mod_kernel,
        grid=(DEPTH, D_MODEL // MOD_TK),
        in_specs=[pl.BlockSpec((N_GROUPS_MOD, D_MODEL, LANE), lambda l, k: (0, 0, 0)),
                  pl.BlockSpec((1, MOD_TK, n), lambda l, k: (l, k, 0)),
                  pl.BlockSpec((1, 1, n), lambda l, k: (l, 0, 0))],
        out_specs=pl.BlockSpec((1, SUBLANE, n), lambda l, k: (l, 0, 0)),
        out_shape=jax.ShapeDtypeStruct((DEPTH, SUBLANE, n), F32),
        scratch_shapes=[pltpu.VMEM((N_GROUPS_MOD, D_MODEL, LANE), F32)],
        compiler_params=_cparams(("arbitrary", "arbitrary")),
        name="adaln_mod",
    )(cb, w_mod, b_mod.reshape(DEPTH, 1, n))
    return out[:, :N_GROUPS_MOD].reshape(DEPTH, N_GROUPS_MOD, MOD_CHUNKS, D_MODEL)


IN_TM = 256
IN_CHUNK = 512


def _pair_specs(tm, block, rows_dim=0):
    nc = T_CTX // tm

    def index(row):
        return tuple(row if d == rows_dim else 0 for d in range(len(block)))

    return [pl.BlockSpec(block, lambda i, *_: index(jnp.minimum(i, nc - 1))),
            pl.BlockSpec(block, lambda i, *_: index(jnp.maximum(i - nc, 0)))]


def _pick(tm, ctx_ref, lat_ref):
    return jnp.where(pl.program_id(0) < T_CTX // tm, ctx_ref[...], lat_ref[...])


def _store_pair(tm, val, oc_ref, ol_ref):
    i = pl.program_id(0)

    @pl.when(i < T_CTX // tm)
    def _():
        oc_ref[...] = val

    @pl.when(i >= T_CTX // tm)
    def _():
        ol_ref[...] = val


def _in_kernel(*refs, pending):
    if pending:
        x1_ref, y_ref, modp_ref, mod_ref, g_ref, wa_ref, wb_ref, wc_ref, z_ref, oc_ref, ol_ref = refs
        x = x1_ref[...] + modp_ref[0, 5:6, :] * y_ref[...]
        _store_pair(IN_TM, x, oc_ref, ol_ref)
    else:
        xc_ref, xl_ref, mod_ref, g_ref, wa_ref, wb_ref, wc_ref, z_ref = refs
        x = _pick(IN_TM, xc_ref, xl_ref)
    shift = mod_ref[0, 0:1, :]
    scale = mod_ref[0, 1:2, :]
    h = (_rms(x, g_ref[...]) * (1.0 + scale) + shift).astype(BF16)
    for w_ref, z0 in ((wa_ref, 0), (wb_ref, Z_GU), (wc_ref, Z_MG)):
        width = w_ref.shape[1]
        for c0 in range(0, width, IN_CHUNK):
            cw = min(IN_CHUNK, width - c0)
            z_ref[:, z0 + c0:z0 + c0 + cw] = jnp.dot(h, w_ref[:, c0:c0 + cw], preferred_element_type=F32)


def _in_proj(x, mod_l, g, w_in_l):
    tm = IN_TM
    w_segs = [w_in_l[:, :W_GATES0].astype(BF16), w_in_l[:, W_GATES1:].astype(BF16),
              jnp.pad(w_in_l[:, W_GATES0:W_GATES1], ((0, 0), (0, Z_COLS - Z_MG - (W_GATES1 - W_GATES0)))).astype(BF16)]
    w_specs = [pl.BlockSpec(w.shape, lambda i: (0, 0), pipeline_mode=pl.Buffered(1)) for w in w_segs]
    pending = len(x) == 3
    row = lambda i: (i, 0)
    mod_spec = pl.BlockSpec((1, MOD_CHUNKS, D_MODEL), lambda i: (_mod_group(i, tm), 0, 0))
    if pending:
        x_specs = [pl.BlockSpec((tm, D_MODEL), row), pl.BlockSpec((tm, D_MODEL), row), mod_spec]
    else:
        x_specs = _pair_specs(tm, (tm, D_MODEL))
    out_specs = [pl.BlockSpec((tm, Z_COLS), row)]
    out_shape = [jax.ShapeDtypeStruct((T_ALL, Z_COLS), F32)]
    if pending:
        out_specs += _pair_specs(tm, (tm, D_MODEL))
        out_shape += [jax.ShapeDtypeStruct((T_CTX, D_MODEL), F32), jax.ShapeDtypeStruct((T_LAT, D_MODEL), F32)]
    outs = pl.pallas_call(
        functools.partial(_in_kernel, pending=pending),
        grid=(T_ALL // tm,),
        in_specs=x_specs + [
                  mod_spec,
                  pl.BlockSpec((1, D_MODEL), lambda i: (0, 0))] + w_specs,
        out_specs=out_specs,
        out_shape=out_shape,
        compiler_params=_cparams(("arbitrary",)),
        name="in_proj_res" if pending else "in_proj",
    )(*x, mod_l, g.reshape(1, D_MODEL), *w_segs)
    return (outs[0], (outs[1], outs[2])) if pending else (outs[0], x)


def _rope(t, c, s):
    lane = lax.broadcasted_iota(jnp.int32, t.shape, 1)
    first = (lane % (HEAD_DIM // 2)) < (HEAD_DIM // 4)
    swapped = jnp.where(first, pltpu.roll(t, HEAD_DIM - HEAD_DIM // 4, 1), pltpu.roll(t, HEAD_DIM // 4, 1))
    return t * c + swapped * s


def _attn_kernel(*refs, seq, past, rope):
    if rope:
        (aq_ref, ak_ref, av_ref, kp_ref, vp_ref, cos_ref, sin_ref, qg_ref, kg_ref,
         att_ref, kb_scr, vb_scr) = refs
    else:
        aq_ref, ak_ref, av_ref, qg_ref, kg_ref, att_ref, knew_ref, vnew_ref, kb_scr, vb_scr = refs
    kn = _rms(ak_ref[...], kg_ref[...])
    v = av_ref[...]
    if rope:
        kn = _rope(kn, cos_ref[...], sin_ref[...])
        kb_scr[seq:seq + past, :] = kp_ref[0, 0].astype(BF16)
        vb_scr[seq:seq + past, :] = vp_ref[0, 0].astype(BF16)
    else:
        knew_ref[0, 0] = kn
        vnew_ref[0, 0] = v
    kb_scr[0:seq, :] = kn.astype(BF16)
    vb_scr[0:seq, :] = v.astype(BF16)
    grp = ATT_HEADS // ATT_KV_HEADS

    def q_block(qb, carry):
        rows = pl.ds(pl.multiple_of(qb * CHUNK, CHUNK), CHUNK)
        qs = []
        for g in range(grp):
            q = _rms(aq_ref[rows, g * HEAD_DIM:(g + 1) * HEAD_DIM], qg_ref[...])
            if rope:
                q = _rope(q, cos_ref[rows, :], sin_ref[rows, :])
            qs.append(q * (HEAD_DIM ** -0.5 * LOG2_E))
        q2 = jnp.concatenate(qs, axis=0).astype(BF16)
        s = lax.dot_general(q2, kb_scr[...], (((1,), (1,)), ((), ())), preferred_element_type=F32)
        m = jnp.max(s, axis=-1, keepdims=True)
        p = jnp.exp2(s - m)
        den = jnp.sum(p, axis=-1, keepdims=True)
        o = jnp.dot(p.astype(BF16), vb_scr[...], preferred_element_type=F32) / den
        for g in range(grp):
            att_ref[rows, g * HEAD_DIM:(g + 1) * HEAD_DIM] = o[g * CHUNK:(g + 1) * CHUNK].astype(BF16)
        return carry

    lax.fori_loop(0, seq // CHUNK, q_block, 0)


def _attention(z, qg, kg, *, ctx, kpast=None, vpast=None, cos=None, sin=None):
    seq = SEQ if ctx else DEC_SEQ
    nb = BATCH if ctx else DEC_BATCH
    row0 = 0 if ctx else T_CTX // DEC_SEQ
    past = 0 if ctx else PAST_LEN
    qw = HEAD_DIM * (ATT_HEADS // ATT_KV_HEADS)
    in_specs = [pl.BlockSpec((seq, qw), lambda b, h: (row0 + b, Z_AQ // qw + h)),
                pl.BlockSpec((seq, HEAD_DIM), lambda b, h: (row0 + b, Z_AK // HEAD_DIM + h)),
                pl.BlockSpec((seq, HEAD_DIM), lambda b, h: (row0 + b, Z_AV // HEAD_DIM + h))]
    args = [z, z, z]
    if not ctx:
        in_specs += [pl.BlockSpec((1, 1, past, HEAD_DIM), lambda b, h: (b, h, 0, 0)),
                     pl.BlockSpec((1, 1, past, HEAD_DIM), lambda b, h: (b, h, 0, 0)),
                     pl.BlockSpec((seq, HEAD_DIM), lambda b, h: (0, 0)),
                     pl.BlockSpec((seq, HEAD_DIM), lambda b, h: (0, 0))]
        args += [kpast, vpast, cos, sin]
    in_specs += [pl.BlockSpec((1, HEAD_DIM), lambda b, h: (0, 0)),
                 pl.BlockSpec((1, HEAD_DIM), lambda b, h: (0, 0))]
    args += [qg.reshape(1, HEAD_DIM), kg.reshape(1, HEAD_DIM)]
    out_specs = [pl.BlockSpec((seq, qw), lambda b, h: (b, h))]
    out_shape = [jax.ShapeDtypeStruct((nb * seq, MIX_W), BF16)]
    if ctx:
        out_specs += [pl.BlockSpec((1, 1, seq, HEAD_DIM), lambda b, h: (b, h, 0, 0))] * 2
        out_shape += [jax.ShapeDtypeStruct((nb, ATT_KV_HEADS, seq, HEAD_DIM), F32)] * 2
    return pl.pallas_call(
        functools.partial(_attn_kernel, seq=seq, past=past, rope=not ctx),
        grid=(nb, ATT_KV_HEADS),
        in_specs=in_specs, out_specs=out_specs, out_shape=out_shape,
        scratch_shapes=[pltpu.VMEM((seq + past, HEAD_DIM), BF16),
                        pltpu.VMEM((seq + past, HEAD_DIM), BF16)],
        compiler_params=_cparams(("arbitrary", "arbitrary")),
        name="attn_ctx" if ctx else "attn_lat",
    )(*args)


def _s5_prep_kernel(are_ref, aim_ref, ldt_ref, bre_ref, bim_ref, pre_ref, pim_ref, wre_ref, wim_ref):
    a_re = are_ref[...]
    a_im = aim_ref[...]
    dt = jnp.exp(ldt_ref[...])
    pows = []
    for tau in range(S5_LAGS + 1):
        mag = jnp.exp((tau * dt) * a_re)
        ang = (tau * dt) * a_im
        pr, pi = mag * jnp.cos(ang), mag * jnp.sin(ang)
        pre_ref[tau] = pr
        pim_ref[tau] = pi
        pows.append((pr, pi))
    nr, ni = pows[1][0] - 1.0, pows[1][1]
    den = a_re * a_re + a_im * a_im
    cr = (nr * a_re + ni * a_im) / den
    ci = (ni * a_re - nr * a_im) / den
    for d in range(2):
        b_r, b_i = bre_ref[d], bim_ref[d]
        bb_r = cr[d:d + 1] * b_r - ci[d:d + 1] * b_i
        bb_i = cr[d:d + 1] * b_i + ci[d:d + 1] * b_r
        for tau in range(S5_LAGS):
            pr, pi = pows[tau][0][d:d + 1], pows[tau][1][d:d + 1]
            wre_ref[d, tau] = pr * bb_r - pi * bb_i
            wim_ref[d, tau] = pr * bb_i + pi * bb_r


def _s5_prep(a_re, a_im, log_dt, b_re, b_im, c_re, c_im):
    gp = S5_GROUPS * S5_STATE
    ldt = jnp.broadcast_to(log_dt[:, :, None], (2, S5_GROUPS, S5_STATE)).reshape(2, gp)
    bt = lambda b: b.transpose(0, 3, 1, 2).reshape(2, S5_CH, gp)
    pre, pim, wre, wim = pl.pallas_call(
        _s5_prep_kernel,
        out_shape=[jax.ShapeDtypeStruct((S5_LAGS + 1, 2, gp), F32)] * 2
        + [jax.ShapeDtypeStruct((2, S5_LAGS, S5_CH, gp), F32)] * 2,
        name="s5_prep",
    )(a_re.reshape(2, gp), a_im.reshape(2, gp), ldt, bt(b_re), bt(b_im))
    eye = jnp.eye(8, dtype=F32)

    hw = S5_SGW // 2
    half_mask = (jnp.arange(hw)[None, :] // S5_STATE == jnp.arange(4)[:, None]).astype(F32)

    def w_layout(w):
        w = w.reshape(2, S5_LAGS // 2, 2, S5_CH, S5_SG, 2, hw).transpose(0, 4, 5, 1, 2, 3, 6)
        w = w[:, :, :, :, :, None, :, :] * half_mask[:, None, :]
        return w.reshape(2, S5_SG, 2, S5_LAGS * LANE // 2, hw)

    w_in = jnp.concatenate([w_layout(wre), w_layout(wim)], axis=-1).astype(BF16)

    def c_layout(c):
        c = c.reshape(2, S5_SG, 8, S5_CH, S5_STATE).transpose(0, 1, 2, 4, 3)
        c = c[:, :, :, :, None, :] * eye[None, None, :, None, :, None]
        return c.reshape(2, S5_SG, S5_SGW, LANE)

    w_out = jnp.concatenate([c_layout(c_re), -c_layout(c_im)], axis=2).astype(BF16)

    def p_layout(p):
        p = p[1:].reshape(S5_LAGS, 2, S5_SG, S5_SGW).transpose(1, 2, 0, 3)
        return jnp.stack([p[0], p[1, :, ::-1]], axis=0)

    def a8_layout(p):
        p = p[S5_LAGS].reshape(2, S5_SG, 1, S5_SGW)
        return jnp.broadcast_to(p, (2, S5_SG, SUBLANE, S5_SGW))

    return w_in, w_out, a8_layout(pre), a8_layout(pim), p_layout(pre), p_layout(pim)


def _s5_kernel(u_ref, w_ref, c_ref, a8r_ref, a8i_ref, pwr_ref, pwi_ref, h0r_ref, h0i_ref,
               y_ref, hfr_ref, hfi_ref, upad, wbr, wbi, *, seq, tc, nseq):
    d = pl.program_id(0)
    zeros = jnp.zeros((SUBLANE, LANE), F32)
    for q in range(nseq):
        upad[q, 0:SUBLANE, :] = zeros
        upad[q, SUBLANE:seq + SUBLANE, :] = u_ref[q * seq:(q + 1) * seq, :]
        upad[q, seq + SUBLANE:seq + 2 * SUBLANE, :] = zeros
    nch = seq // tc
    nt = tc // SUBLANE
    a8r = a8r_ref[0, 0]
    a8i = a8i_ref[0, 0]

    def run(fwd):
        pwr = pwr_ref[0, 0]
        pwi = pwi_ref[0, 0]
        state = tuple((jnp.zeros((SUBLANE, S5_SGW), F32), jnp.zeros((SUBLANE, S5_SGW), F32))
                      for _ in range(nseq))
        for ci in range(nch):
            c = ci if fwd else nch - 1 - ci
            per_seq = []
            for q in range(nseq):
                if fwd:
                    win = upad[q, c * tc:c * tc + tc + SUBLANE, :]
                    per_seq.append([pltpu.roll(win, tau, 0)[SUBLANE:SUBLANE + tc] if tau
                                    else win[SUBLANE:SUBLANE + tc] for tau in range(S5_LAGS)])
                else:
                    win = upad[q, c * tc + SUBLANE:c * tc + tc + 2 * SUBLANE, :]
                    per_seq.append([pltpu.roll(win, tc + SUBLANE - tau, 0)[0:tc] if tau else win[0:tc]
                                    for tau in range(S5_LAGS)])
            lags = [jnp.concatenate([per_seq[q][tau] for q in range(nseq)], axis=0) if nseq > 1
                    else per_seq[0][tau] for tau in range(S5_LAGS)]
            low = lax.broadcasted_iota(jnp.int32, (nseq * tc, LANE), 1) < LANE // 2
            slabs = ([], [])
            for k in range(S5_LAGS // 2):
                a, b = lags[2 * k], lags[2 * k + 1]
                slabs[0].append(jnp.where(low, a, pltpu.roll(b, LANE // 2, 1)).astype(BF16))
                slabs[1].append(jnp.where(low, pltpu.roll(a, LANE // 2, 1), b).astype(BF16))
            hw = S5_SGW // 2
            for half in range(2):
                w = jnp.dot(jnp.concatenate(slabs[half], axis=1), w_ref[0, 0, half],
                            preferred_element_type=F32)
                wbr[:, half * hw:(half + 1) * hw] = w[:, :hw]
                wbi[:, half * hw:(half + 1) * hw] = w[:, hw:]
            if ci == 0:
                for q in range(nseq):
                    h0r = h0r_ref[q, 0, 0]
                    h0i = h0i_ref[q, 0, 0]
                    r0 = q * tc + (0 if fwd else tc - SUBLANE)
                    wbr[r0:r0 + SUBLANE, :] = wbr[r0:r0 + SUBLANE, :] + (pwr * h0r - pwi * h0i)
                    wbi[r0:r0 + SUBLANE, :] = wbi[r0:r0 + SUBLANE, :] + (pwr * h0i + pwi * h0r)

            def step(i, carry):
                t = i if fwd else nt - 1 - i
                new = []
                for q in range(nseq):
                    cr, ci_ = carry[q]
                    rows = pl.ds(pl.multiple_of(q * tc + t * SUBLANE, SUBLANE), SUBLANE)
                    nr = a8r * cr - a8i * ci_ + wbr[rows, :]
                    ni = a8r * ci_ + a8i * cr + wbi[rows, :]
                    wbr[rows, :] = nr
                    wbi[rows, :] = ni
                    new.append((nr, ni))
                return tuple(new)

            state = lax.fori_loop(0, nt, step, state)
            hcat = jnp.concatenate([wbr[...].astype(BF16), wbi[...].astype(BF16)], axis=1)
            y = jnp.dot(hcat, c_ref[0, 0], preferred_element_type=F32)
            for q in range(nseq):
                y_ref[0, q * seq + c * tc:q * seq + (c + 1) * tc, :] = y[q * tc:(q + 1) * tc]
        last = SUBLANE - 1 if fwd else 0
        for q in range(nseq):
            hfr_ref[q, 0, 0] = state[q][0][last:last + 1]
            hfi_ref[q, 0, 0] = state[q][1][last:last + 1]

    @pl.when(d == 0)
    def _():
        run(True)

    @pl.when(d == 1)
    def _():
        run(False)


def _s5_scan(z, prep, h0r, h0i, *, ctx):
    w_in, w_out, a8r, a8i, pwr, pwi = prep
    seq = SEQ if ctx else DEC_SEQ
    nb = BATCH if ctx else DEC_BATCH
    row0 = 0 if ctx else T_CTX // DEC_SEQ
    tc = min(seq, 512)
    nseq = S5_CTX_NSEQ if ctx else 1
    par = lambda d, s, b: (d, s, 0, 0)
    st = lambda d, s, b: (b, d, s, 0, 0)
    return pl.pallas_call(
        functools.partial(_s5_kernel, seq=seq, tc=tc, nseq=nseq),
        grid=(2, S5_SG, nb // nseq),
        in_specs=[pl.BlockSpec((nseq * seq, LANE), lambda d, s, b: (row0 + b, Z_SX // LANE + s)),
                  pl.BlockSpec((1, 1, 2, S5_LAGS * LANE // 2, S5_SGW), lambda d, s, b: (d, s, 0, 0, 0)),
                  pl.BlockSpec((1, 1, 2 * S5_SGW, LANE), par),
                  pl.BlockSpec((1, 1, SUBLANE, S5_SGW), par),
                  pl.BlockSpec((1, 1, SUBLANE, S5_SGW), par),
                  pl.BlockSpec((1, 1, SUBLANE, S5_SGW), par),
                  pl.BlockSpec((1, 1, SUBLANE, S5_SGW), par),
                  pl.BlockSpec((nseq, 1, 1, 1, S5_SGW), st),
                  pl.BlockSpec((nseq, 1, 1, 1, S5_SGW), st)],
        out_specs=[pl.BlockSpec((1, nseq * seq, LANE), lambda d, s, b: (d, b, s)),
                   pl.BlockSpec((nseq, 1, 1, 1, S5_SGW), st),
                   pl.BlockSpec((nseq, 1, 1, 1, S5_SGW), st)],
        out_shape=[jax.ShapeDtypeStruct((2, nb * seq, MIX_W), F32),
                   jax.ShapeDtypeStruct((nb, 2, S5_SG, 1, S5_SGW), F32),
                   jax.ShapeDtypeStruct((nb, 2, S5_SG, 1, S5_SGW), F32)],
        scratch_shapes=[pltpu.VMEM((nseq, seq + 2 * SUBLANE, LANE), F32),
                        pltpu.VMEM((nseq * tc, S5_SGW), F32),
                        pltpu.VMEM((nseq * tc, S5_SGW), F32)],
        compiler_params=_cparams(("arbitrary", "arbitrary", "arbitrary")),
        name="s5_ctx" if ctx else "s5_lat",
    )(z, w_in, w_out, a8r, a8i, pwr, pwi, h0r, h0i)


def _mlstm_kernel(q_ref, k_ref, v_ref, mo_ref, gc_ref, gr_ref, bc_ref, br_ref, c0_ref, n0_ref, m0_ref,
                  ng_ref, h_ref, c_ref, n_ref, m_ref, hs_scr, *, seq, chunk0):
    nc = seq // CHUNK
    ii = lax.broadcasted_iota(jnp.int32, (CHUNK, CHUNK), 0)
    jj = lax.broadcasted_iota(jnp.int32, (CHUNK, CHUNK), 1)
    neg_inf = jnp.float32(-jnp.inf)

    def chunk_step(hh, d, cidx, carry):
        cols = slice(hh * MLSTM_DIM, (hh + 1) * MLSTM_DIM)
        fwd = d == 0
        mask = (jj <= ii) if fwd else (jj >= ii)
        mask_t = (ii <= jj) if fwd else (ii >= jj)
        last = CHUNK - 1 if fwd else 0
        c_st, n_st, m_st = carry
        rows = pl.ds(pl.multiple_of(cidx * CHUNK, CHUNK), CHUNK)
        q = q_ref[rows, cols]
        k = k_ref[rows, cols] * (MLSTM_DIM ** -0.5)
        v = v_ref[rows, cols]
        gcol = gc_ref[hh, rows, :] + bc_ref[hh]
        grow = gr_ref[hh, chunk0 + cidx] + br_ref[hh]
        li_c = gcol[:, 2 * d:2 * d + 1]
        lf_c = jax.nn.log_sigmoid(gcol[:, 2 * d + 1:2 * d + 2])
        li_r = grow[2 * d:2 * d + 1, :]
        lf_r = jax.nn.log_sigmoid(grow[2 * d + 1:2 * d + 2, :])
        b_c = jnp.sum(jnp.where(mask, lf_r, 0.0), axis=1, keepdims=True)
        b_r = jnp.sum(jnp.where(mask_t, lf_c, 0.0), axis=0, keepdims=True)
        dmat = jnp.where(mask, b_c - b_r + li_r, neg_inf)
        inter = b_c + m_st
        m_j = jnp.maximum(inter, jnp.max(dmat, axis=1, keepdims=True))
        qb = q.astype(BF16)
        vb = v.astype(BF16)
        s = lax.dot_general(qb, k.astype(BF16), (((1,), (1,)), ((), ())),
                            preferred_element_type=F32) * jnp.exp(dmat - m_j)
        w_inter = jnp.exp(inter - m_j)
        num = (jnp.dot(s.astype(BF16), vb, preferred_element_type=F32)
               + w_inter * jnp.dot(qb, c_st.astype(BF16), preferred_element_type=F32))
        den = jnp.sum(s, axis=1, keepdims=True) + w_inter * jnp.sum(q * n_st, axis=1, keepdims=True)
        hs_scr[d, rows, cols] = num / jnp.maximum(jnp.abs(den), jnp.exp(-m_j))
        m_end = m_j[last:last + 1, :]
        b_end = b_c[last:last + 1, :]
        w_c = jnp.exp(b_end - b_c + li_c - m_end)
        decay = jnp.exp(b_end + m_st - m_end)
        kw = k * w_c
        c_new = decay * c_st + lax.dot_general(kw.astype(BF16), vb, (((0,), (0,)), ((), ())),
                                               preferred_element_type=F32)
        n_new = decay * n_st + jnp.sum(kw, axis=0, keepdims=True)
        return c_new, n_new, m_end

    chains = [(hh, d) for hh in range(MLSTM_HG) for d in range(2)]

    def all_chains(ci, carry):
        return tuple(chunk_step(hh, d, ci if d == 0 else nc - 1 - ci, carry[n])
                     for n, (hh, d) in enumerate(chains))

    init = tuple((c0_ref[0, d, hh], n0_ref[0, d, hh], m0_ref[0, d, hh]) for hh, d in chains)
    final = lax.fori_loop(0, nc, all_chains, init)
    for n, (hh, d) in enumerate(chains):
        c_ref[0, d, hh], n_ref[0, d, hh], m_ref[0, d, hh] = final[n]
    for hh in range(MLSTM_HG):
        cols = slice(hh * MLSTM_DIM, (hh + 1) * MLSTM_DIM)
        h = _rms(hs_scr[0, :, cols] + hs_scr[1, :, cols], ng_ref[hh])
        h_ref[:, cols] = (h * jax.nn.sigmoid(mo_ref[:, cols])).astype(BF16)


def _mlstm(z, gcol, grow, bias_c, bias_r, c0, n0, m0, norm_g, *, ctx):
    seq = SEQ if ctx else DEC_SEQ
    nb = BATCH if ctx else DEC_BATCH
    row0 = 0 if ctx else T_CTX // DEC_SEQ
    hd = MLSTM_DIM
    nh = MLSTM_HEADS
    hg = MLSTM_HG
    hw = hg * hd
    zcol = lambda off: (lambda b, h: (row0 + b, off // hw + h))
    st5 = lambda b, h: (b, 0, h, 0, 0)
    return pl.pallas_call(
        functools.partial(_mlstm_kernel, seq=seq, chunk0=0),
        grid=(nb, nh // hg),
        in_specs=[pl.BlockSpec((seq, hw), zcol(Z_MQ)),
                  pl.BlockSpec((seq, hw), zcol(Z_MK)),
                  pl.BlockSpec((seq, hw), zcol(Z_MV)),
                  pl.BlockSpec((seq, hw), zcol(Z_MO)),
                  pl.BlockSpec((hg, seq, 4), lambda b, h: (h, row0 + b, 0)),
                  pl.BlockSpec((hg, seq // CHUNK, 4, CHUNK), lambda b, h: (h, row0 + b, 0, 0)),
                  pl.BlockSpec((hg, 1, 4), lambda b, h: (h, 0, 0)),
                  pl.BlockSpec((hg, 4, 1), lambda b, h: (h, 0, 0)),
                  pl.BlockSpec((1, 2, hg, hd, hd), st5),
                  pl.BlockSpec((1, 2, hg, 1, hd), st5),
                  pl.BlockSpec((1, 2, hg, 1, 1), st5),
                  pl.BlockSpec((hg, 1, hd), lambda b, h: (h, 0, 0))],
        out_specs=[pl.BlockSpec((seq, hw), lambda b, h: (b, h)),
                   pl.BlockSpec((1, 2, hg, hd, hd), st5),
                   pl.BlockSpec((1, 2, hg, 1, hd), st5),
                   pl.BlockSpec((1, 2, hg, 1, 1), st5)],
        out_shape=[jax.ShapeDtypeStruct((nb * seq, MIX_W), BF16),
                   jax.ShapeDtypeStruct((nb, 2, nh, hd, hd), F32),
                   jax.ShapeDtypeStruct((nb, 2, nh, 1, hd), F32),
                   jax.ShapeDtypeStruct((nb, 2, nh, 1, 1), F32)],
        scratch_shapes=[pltpu.VMEM((2, seq, hw), F32)],
        compiler_params=_cparams(("arbitrary", "arbitrary")),
        name="mlstm_ctx" if ctx else "mlstm_lat",
    )(z, z, z, z, gcol, grow, bias_c, bias_r, c0, n0, m0, norm_g.reshape(nh, 1, hd))


OUT_TM = 256


def _out_kernel(*refs, router):
    (ac_ref, al_ref, yc_ref, yl_ref, sx_ref, sd_ref, sw_ref, sb_ref, cc_ref, cl_ref,
     gu_ref, gv_ref, gn_ref, gw_ref, gb_ref, xc_ref, xl_ref, mod_ref, g_ref, w_ref), refs = refs[:20], refs[20:]
    if router:
        rw_ref, rb_ref, x1_ref, h2_ref, rt_ref = refs
    else:
        x1_ref, h2_ref = refs
    y2 = _pick(OUT_TM, yc_ref, yl_ref)
    y = jax.nn.gelu(y2[0] + y2[1] + sd_ref[...] * sx_ref[...])
    gate = jnp.dot(y.astype(BF16), sw_ref[...], preferred_element_type=F32) + sb_ref[...]
    s5_out = (y * jax.nn.sigmoid(gate)).astype(BF16)
    vn = _rms(gv_ref[...], gn_ref[...]).astype(BF16)
    gw = MIX_W // GMLP_GROUPS
    chunks = []
    for c in range(OUT_TM // CHUNK):
        r = slice(c * CHUNK, (c + 1) * CHUNK)
        groups = []
        for g in range(GMLP_GROUPS):
            cs = slice(g * gw, (g + 1) * gw)
            mixed = jnp.dot(gw_ref[g], vn[r, cs], preferred_element_type=F32) + gb_ref[:, g:g + 1]
            groups.append((gu_ref[r, cs] * mixed).astype(BF16))
        chunks.append(jnp.concatenate(groups, axis=1))
    gm_out = jnp.concatenate(chunks, axis=0)
    mix = jnp.dot(_pick(OUT_TM, ac_ref, al_ref), w_ref[0:MIX_W, :], preferred_element_type=F32)
    mix += jnp.dot(s5_out, w_ref[MIX_W:2 * MIX_W, :], preferred_element_type=F32)
    mix += jnp.dot(_pick(OUT_TM, cc_ref, cl_ref), w_ref[2 * MIX_W:3 * MIX_W, :], preferred_element_type=F32)
    mix += jnp.dot(gm_out, w_ref[3 * MIX_W:4 * MIX_W, :], preferred_element_type=F32)
    x1 = _pick(OUT_TM, xc_ref, xl_ref) + mod_ref[0, 2:3, :] * mix
    x1_ref[...] = x1
    h2 = _rms(x1, g_ref[...]) * (1.0 + mod_ref[0, 4:5, :]) + mod_ref[0, 3:4, :]
    hi = h2.astype(BF16)
    h2_ref[...] = h2 if router else hi
    if router:
        lo = (h2 - hi.astype(F32)).astype(BF16)
        both = jnp.dot(hi, rw_ref[...], preferred_element_type=F32)
        logits = (both[:, :LANE] + jnp.dot(lo, rw_ref[:, :LANE], preferred_element_type=F32)
                  + both[:, LANE:]) + rb_ref[...]
        lane = lax.broadcasted_iota(jnp.int32, logits.shape, 1)
        neg_inf = jnp.float32(-jnp.inf)
        lg = jnp.where(lane < N_EXPERTS, logits, neg_inf)
        m1 = jnp.max(lg, axis=-1, keepdims=True)
        i1 = jnp.min(jnp.where(lg == m1, lane, LANE), axis=-1, keepdims=True)
        lg2 = jnp.where(lane == i1, neg_inf, lg)
        m2 = jnp.max(lg2, axis=-1, keepdims=True)
        i2 = jnp.min(jnp.where(lg2 == m2, lane, LANE), axis=-1, keepdims=True)
        e = jnp.exp(m2 - m1)
        w1 = 1.0 / (1.0 + e)
        w2 = e / (1.0 + e)
        rt = jnp.where(lane == 0, i1.astype(F32), 0.0)
        rt = jnp.where(lane == 1, i2.astype(F32), rt)
        rt = jnp.where(lane == 2, w1, rt)
        rt = jnp.where(lane == 3, w2, rt)
        rt_ref[...] = rt


def _out_proj(z, att_pair, s5_args, ml_pair, gm_args, x_pair, mod_l, g2, w_out, router=None):
    tm = OUT_TM
    row = lambda i: (i, 0)
    fixed = lambda i: (0, 0)
    pair = _pair_specs(tm, (tm, MIX_W))
    zcols = lambda off: pl.BlockSpec((tm, MIX_W), lambda i: (i, off // MIX_W))
    vec = pl.BlockSpec((1, MIX_W), fixed)
    y_pair, d_skip, w_glu, b_glu = s5_args
    gm_norm_g, gm_w_s, gm_b_s = gm_args
    in_specs = (pair + _pair_specs(tm, (2, tm, MIX_W), rows_dim=1)
                + [zcols(Z_SX), vec, pl.BlockSpec((MIX_W, MIX_W), fixed), vec]
                + pair
                + [zcols(Z_GU), zcols(Z_GV), vec, pl.BlockSpec((GMLP_GROUPS, CHUNK, CHUNK), lambda i: (0, 0, 0)),
                   pl.BlockSpec((CHUNK, GMLP_GROUPS), fixed)]
                + _pair_specs(tm, (tm, D_MODEL))
                + [pl.BlockSpec((1, MOD_CHUNKS, D_MODEL), lambda i: (_mod_group(i, tm), 0, 0)),
                   pl.BlockSpec((1, D_MODEL), fixed),
                   pl.BlockSpec((D_MODEL, D_MODEL), fixed, pipeline_mode=pl.Buffered(1))])
    args = [*att_pair, *y_pair, z, d_skip.reshape(1, MIX_W), w_glu.astype(BF16), b_glu.reshape(1, MIX_W),
            *ml_pair, z, z, gm_norm_g.reshape(1, MIX_W), gm_w_s.astype(BF16), gm_b_s.T,
            *x_pair, mod_l, g2.reshape(1, D_MODEL), w_out]
    out_specs = [pl.BlockSpec((tm, D_MODEL), row), pl.BlockSpec((tm, D_MODEL), row)]
    out_shape = [jax.ShapeDtypeStruct((T_ALL, D_MODEL), F32),
                 jax.ShapeDtypeStruct((T_ALL, D_MODEL), BF16 if router is None else F32)]
    if router is not None:
        rw, rb = router
        rw = jnp.pad(rw, ((0, 0), (0, LANE - N_EXPERTS)))
        rh = rw.astype(BF16)
        rl = (rw - rh.astype(F32)).astype(BF16)
        in_specs += [pl.BlockSpec((D_MODEL, 2 * LANE), fixed), pl.BlockSpec((1, LANE), fixed)]
        args += [jnp.concatenate([rh, rl], axis=1), jnp.pad(rb, (0, LANE - N_EXPERTS)).reshape(1, LANE)]
        out_specs.append(pl.BlockSpec((tm, LANE), row))
        out_shape.append(jax.ShapeDtypeStruct((T_ALL, LANE), F32))
    return pl.pallas_call(
        functools.partial(_out_kernel, router=router is not None),
        grid=(T_ALL // tm,),
        in_specs=in_specs, out_specs=out_specs, out_shape=out_shape,
        compiler_params=_cparams(("arbitrary",)),
        name="out_proj_router" if router is not None else "out_proj",
    )(*args)


def _ffn_kernel(x_ref, wg_ref, wu_ref, wd_ref, o_ref):
    @pl.when(pl.program_id(1) == 0)
    def _():
        o_ref[...] = jnp.zeros_like(o_ref)

    def sub(s, carry):
        rows = pl.ds(pl.multiple_of(s * FFN_DENSE_SUB, FFN_DENSE_SUB), FFN_DENSE_SUB)
        xs = x_ref[rows, :]
        g = jnp.dot(xs, wg_ref[...], preferred_element_type=F32)
        u = jnp.dot(xs, wu_ref[...], preferred_element_type=F32)
        a = (g * jax.nn.sigmoid(g) * u).astype(BF16)
        o_ref[rows, :] += jnp.dot(a, wd_ref[...], preferred_element_type=F32)
        return carry

    lax.fori_loop(0, FFN_TM // FFN_DENSE_SUB, sub, 0)


def _ffn(x, w_gate, w_up, w_down):
    return pl.pallas_call(
        _ffn_kernel,
        grid=(T_ALL // FFN_TM, D_FF // FFN_TF),
        in_specs=[pl.BlockSpec((FFN_TM, D_MODEL), lambda i, f: (i, 0)),
                  pl.BlockSpec((D_MODEL, FFN_TF), lambda i, f: (0, f)),
                  pl.BlockSpec((D_MODEL, FFN_TF), lambda i, f: (0, f)),
                  pl.BlockSpec((FFN_TF, D_MODEL), lambda i, f: (f, 0))],
        out_specs=pl.BlockSpec((FFN_TM, D_MODEL), lambda i, f: (i, 0)),
        out_shape=jax.ShapeDtypeStruct((T_ALL, D_MODEL), F32),
        compiler_params=_cparams(("arbitrary", "arbitrary")),
        name="ffn",
    )(x, w_gate, w_up, w_down)


def _moe_ffn_kernel(te_ref, ns_ref, src_ref, x_hbm, wg_ref, wu_ref, wd_ref, o_ref, xf, xb, sems):
    i = pl.program_id(0)
    f = pl.program_id(1)
    ntiles = pl.num_programs(0)

    def row_copy(src_row, slot, r):
        return pltpu.make_async_copy(x_hbm.at[pl.ds(src_row, 1)], xf.at[slot, pl.ds(r, 1)], sems.at[slot])

    def issue(tile, slot):
        def body(r8, carry):
            for u in range(ISSUE_UNROLL):
                r = r8 * ISSUE_UNROLL + u
                row_copy(src_ref[tile * FFN_TM + r], slot, r).start()
            return carry

        lax.fori_loop(0, ns_ref[tile] * (FFN_SUB // ISSUE_UNROLL), body, 0)

    def wait_rows(tile, slot):
        def body(s, carry):
            rows = pl.ds(pl.multiple_of(s * FFN_SUB, FFN_SUB), FFN_SUB)
            pltpu.make_async_copy(x_hbm.at[pl.ds(0, FFN_SUB)], xf.at[slot, rows], sems.at[slot]).wait()
            return carry

        lax.fori_loop(0, ns_ref[tile], body, 0)

    @pl.when(f == 0)
    def _():
        slot = i % 2

        @pl.when(i == 0)
        def _():
            issue(0, 0)

        wait_rows(i, slot)

        @pl.when(i + 1 < ntiles)
        def _():
            issue(i + 1, 1 - slot)

        def cast(s, carry):
            rows = pl.ds(pl.multiple_of(s * FFN_SUB, FFN_SUB), FFN_SUB)
            xb[rows, :] = xf[slot, rows, :].astype(BF16)
            return carry

        lax.fori_loop(0, ns_ref[i], cast, 0)
        o_ref[...] = jnp.zeros_like(o_ref)

    ns = ns_ref[i]

    def block(row0, size):
        rows = pl.ds(pl.multiple_of(row0, size), size)
        xs = xb[rows, :]
        g = jnp.dot(xs, wg_ref[0], preferred_element_type=F32)
        u = jnp.dot(xs, wu_ref[0], preferred_element_type=F32)
        a = (g * jax.nn.sigmoid(g) * u).astype(BF16)
        o_ref[rows, :] += jnp.dot(a, wd_ref[0], preferred_element_type=F32)

    pairs = ns // 2

    def pair(s, carry):
        block(s * (2 * FFN_SUB), 2 * FFN_SUB)
        return carry

    lax.fori_loop(0, pairs, pair, 0)

    @pl.when(ns % 2 == 1)
    def _():
        block(pairs * (2 * FFN_SUB), FFN_SUB)


def _moe_ffn(h2, tile_e, tile_ns, src, w_gate, w_up, w_down):
    tf = FFN_TF
    nf = D_FF // tf

    def fcol(i, f, ns):
        return jnp.where(ns[i] > 0, f, nf - 1)

    return pl.pallas_call(
        _moe_ffn_kernel,
        grid_spec=pltpu.PrefetchScalarGridSpec(
            num_scalar_prefetch=3,
            grid=(MOE_TILES, nf),
            in_specs=[pl.BlockSpec(memory_space=pl.ANY),
                      pl.BlockSpec((1, D_MODEL, tf), lambda i, f, te, ns, sr: (te[i], 0, fcol(i, f, ns))),
                      pl.BlockSpec((1, D_MODEL, tf), lambda i, f, te, ns, sr: (te[i], 0, fcol(i, f, ns))),
                      pl.BlockSpec((1, tf, D_MODEL), lambda i, f, te, ns, sr: (te[i], fcol(i, f, ns), 0))],
            out_specs=pl.BlockSpec((FFN_TM, D_MODEL), lambda i, f, te, ns, sr: (i, 0)),
            scratch_shapes=[pltpu.VMEM((2, FFN_TM, D_MODEL), F32),
                            pltpu.VMEM((FFN_TM, D_MODEL), BF16),
                            pltpu.SemaphoreType.DMA((2,))]),
        out_shape=jax.ShapeDtypeStruct((MOE_TILES * FFN_TM, D_MODEL), F32),
        compiler_params=_cparams(("arbitrary", "arbitrary")),
        name="moe_ffn",
    )(tile_e, tile_ns, src, h2, w_gate, w_up, w_down)


RES_TM = 256


def _res_kernel(x_ref, y_ref, mod_ref, g_ref, oc_ref, ol_ref, *, final):
    x2 = x_ref[...] + mod_ref[0, 5:6, :] * y_ref[...]
    _store_pair(RES_TM, _rms(x2, g_ref[...]) if final else x2, oc_ref, ol_ref)


def _moe_res_kernel(slot_ref, x_ref, rt_ref, mod_ref, g_ref, ys_hbm, oc_ref, ol_ref, ybuf, sems, *, final):
    i = pl.program_id(0)
    nt = pl.num_programs(0)

    def row_copy(src_row, buf, k, r):
        return pltpu.make_async_copy(ys_hbm.at[pl.ds(src_row, 1)], ybuf.at[buf, k, pl.ds(r, 1)], sems.at[buf])

    def issue(tile, buf):
        def body(r8, carry):
            for u in range(ISSUE_UNROLL):
                r = r8 * ISSUE_UNROLL + u
                for k in range(2):
                    row_copy(slot_ref[k * T_ALL + tile * RES_TM + r], buf, k, r).start()
            return carry

        lax.fori_loop(0, RES_TM // ISSUE_UNROLL, body, 0)

    buf = i % 2

    @pl.when(i == 0)
    def _():
        issue(0, 0)

    for k in range(2):
        pltpu.make_async_copy(ys_hbm.at[pl.ds(0, RES_TM)], ybuf.at[buf, k], sems.at[buf]).wait()

    @pl.when(i + 1 < nt)
    def _():
        issue(i + 1, 1 - buf)

    rt = rt_ref[...]
    ffn = rt[:, 2:3] * ybuf[buf, 0] + rt[:, 3:4] * ybuf[buf, 1]
    x2 = x_ref[...] + mod_ref[0, 5:6, :] * ffn
    _store_pair(RES_TM, _rms(x2, g_ref[...]) if final else x2, oc_ref, ol_ref)


def _residual(x1, y, mod_l, final_g, *, route=None, slot=None, final):
    tm = RES_TM
    nt = T_ALL // tm
    out_shape = [jax.ShapeDtypeStruct((T_CTX, D_MODEL), F32), jax.ShapeDtypeStruct((T_LAT, D_MODEL), F32)]
    out_specs = _pair_specs(tm, (tm, D_MODEL))
    if route is None:
        row = lambda i: (i, 0)
        return pl.pallas_call(
            functools.partial(_res_kernel, final=final),
            grid=(nt,),
            in_specs=[pl.BlockSpec((tm, D_MODEL), row), pl.BlockSpec((tm, D_MODEL), row),
                      pl.BlockSpec((1, MOD_CHUNKS, D_MODEL), lambda i: (_mod_group(i, tm), 0, 0)),
                      pl.BlockSpec((1, D_MODEL), lambda i: (0, 0))],
            out_specs=out_specs,
            out_shape=out_shape,
            compiler_params=_cparams(("arbitrary",)),
            name="ffn_residual",
        )(x1, y, mod_l, final_g.reshape(1, D_MODEL))
    row = lambda i, sl: (i, 0)
    return pl.pallas_call(
        functools.partial(_moe_res_kernel, final=final),
        grid_spec=pltpu.PrefetchScalarGridSpec(
            num_scalar_prefetch=1,
            grid=(nt,),
            in_specs=[pl.BlockSpec((tm, D_MODEL), row), pl.BlockSpec((tm, LANE), row),
                      pl.BlockSpec((1, MOD_CHUNKS, D_MODEL), lambda i, sl: (_mod_group(i, tm), 0, 0)),
                      pl.BlockSpec((1, D_MODEL), lambda i, sl: (0, 0)),
                      pl.BlockSpec(memory_space=pl.ANY)],
            out_specs=out_specs,
            scratch_shapes=[pltpu.VMEM((2, 2, tm, D_MODEL), F32), pltpu.SemaphoreType.DMA((2,))]),
        out_shape=out_shape,
        compiler_params=_cparams(("arbitrary",)),
        name="moe_residual",
    )(slot, x1, route, mod_l, final_g.reshape(1, D_MODEL), y)


def _route_plan(route):
    e_flat = jnp.concatenate([route[:, 0], route[:, 1]]).astype(jnp.int32)
    onehot = (e_flat[:, None] == jnp.arange(N_EXPERTS, dtype=jnp.int32)[None, :]).astype(jnp.int32)
    ranks = jnp.cumsum(onehot, axis=0) - onehot
    rank = jnp.sum(ranks * onehot, axis=1)
    counts = jnp.sum(onehot, axis=0)
    tiles = (counts + FFN_TM - 1) // FFN_TM
    tile_start = jnp.cumsum(tiles) - tiles
    slot = tile_start[e_flat] * FFN_TM + rank
    n_slots = MOE_TILES * FFN_TM
    tok = jnp.concatenate([jnp.arange(T_ALL, dtype=jnp.int32)] * 2)
    src = jnp.zeros((n_slots,), jnp.int32).at[slot].set(tok, unique_indices=True)
    tile_ids = jnp.arange(MOE_TILES, dtype=jnp.int32)
    ends = jnp.cumsum(tiles)
    tile_e = jnp.sum((tile_ids[:, None] >= ends[None, :]).astype(jnp.int32), axis=1)
    used = tile_e < N_EXPERTS
    last_e = jnp.max(jnp.where(counts > 0, jnp.arange(N_EXPERTS, dtype=jnp.int32), 0))
    tile_e = jnp.where(used, tile_e, last_e)
    rows_in_tile = jnp.clip(counts[tile_e] - (tile_ids - tile_start[tile_e]) * FFN_TM, 0, FFN_TM)
    tile_ns = jnp.where(used, (rows_in_tile + FFN_SUB - 1) // FFN_SUB, 0).astype(jnp.int32)
    return src, slot, tile_e.astype(jnp.int32), tile_ns


def _rope_tables():
    length = DEC_SEQ
    r = jnp.repeat(jnp.arange(length // GRID_W, dtype=F32), GRID_W)
    col = (jnp.arange(length) % GRID_W).astype(F32)
    half = HEAD_DIM // 2
    inv = ROPE_THETA ** (-jnp.arange(0, half, 2, dtype=F32) / half)
    ar, ac = r[:, None] * inv, col[:, None] * inv
    cos = jnp.concatenate([jnp.cos(ar), jnp.cos(ar), jnp.cos(ac), jnp.cos(ac)], axis=-1)
    sin = jnp.concatenate([-jnp.sin(ar), jnp.sin(ar), -jnp.sin(ac), jnp.sin(ac)], axis=-1)
    return cos, sin


def kernel(x_prompt, x_sample, c, c_ctx, cache_attn_k, cache_attn_v, state_s5_re, state_s5_im, state_mlstm_c, state_mlstm_n, state_mlstm_m, norm1_g, norm2_g, w_mod, b_mod, w_in, w_out, q_norm_g, k_norm_g, s5_a_re, s5_a_im, s5_log_dt, s5_b_re, s5_b_im, s5_c_re, s5_c_im, s5_d, s5_w_glu, s5_b_glu, mlstm_i_bias, mlstm_f_bias, mlstm_norm_g, gmlp_norm_g, gmlp_w_s, gmlp_b_s, ffn_w_gate, ffn_w_up, ffn_w_down, moe_router, moe_router_bias, moe_w_gate, moe_w_up, moe_w_down, final_norm_g):
    x = (x_prompt.reshape(T_CTX, D_MODEL), x_sample.reshape(T_LAT, D_MODEL))
    cond = jnp.concatenate([c_ctx[None, :], c], axis=0)
    mod = _modulation(cond, w_mod, b_mod)
    cos, sin = _rope_tables()
    nh = MLSTM_HEADS
    zeros_s5 = jnp.zeros((BATCH, 2, S5_SG, 1, S5_SGW), F32)
    zeros_c = jnp.zeros((BATCH, 2, nh, MLSTM_DIM, MLSTM_DIM), F32)
    zeros_n = jnp.zeros((BATCH, 2, nh, 1, MLSTM_DIM), F32)
    zeros_m = jnp.zeros((BATCH, 2, nh, 1, 1), F32)

    ctx_states = []
    for l in range(DEPTH):
        use_moe = l % 2 == 1
        j = l // 2
        z, x = _in_proj(x, mod[l], norm1_g[l], w_in[l])

        att_c, k_new, v_new = _attention(z, q_norm_g[l], k_norm_g[l], ctx=True)
        (att_l,) = _attention(z, q_norm_g[l], k_norm_g[l], ctx=False,
                              kpast=cache_attn_k[:, l], vpast=cache_attn_v[:, l], cos=cos, sin=sin)

        prep = _s5_prep(s5_a_re[l], s5_a_im[l], s5_log_dt[l], s5_b_re[l], s5_b_im[l], s5_c_re[l], s5_c_im[l])
        y_c, hf_re, hf_im = _s5_scan(z, prep, zeros_s5, zeros_s5, ctx=True)
        st = lambda s: s[:, l].reshape(DEC_BATCH, 2, S5_SG, 1, S5_SGW)
        y_l, _, _ = _s5_scan(z, prep, st(state_s5_re), st(state_s5_im), ctx=False)
        s5_args = ((y_c, y_l), s5_d[l], s5_w_glu[l], s5_b_glu[l])

        mg = z[:, Z_MG:Z_MG + 16].reshape(T_ALL, 2, 2, nh)
        gcol = mg.transpose(3, 0, 1, 2).reshape(nh, T_ALL, 4)
        grow = mg.reshape(T_ALL // CHUNK, CHUNK, 4, nh).transpose(3, 0, 2, 1)
        bias = jnp.stack([mlstm_i_bias[l], mlstm_f_bias[l]], axis=1)
        bias_c = bias.transpose(2, 0, 1).reshape(nh, 1, 4)
        bias_r = bias_c.reshape(nh, 4, 1)
        ml_c_out, c_new, n_new, m_new = _mlstm(z, gcol, grow, bias_c, bias_r, zeros_c, zeros_n, zeros_m,
                                               mlstm_norm_g[l], ctx=True)
        ml_l_out, _, _, _ = _mlstm(z, gcol, grow, bias_c, bias_r, state_mlstm_c[:, l],
                                   state_mlstm_n[:, l].reshape(DEC_BATCH, 2, nh, 1, MLSTM_DIM),
                                   state_mlstm_m[:, l].reshape(DEC_BATCH, 2, nh, 1, 1),
                                   mlstm_norm_g[l], ctx=False)

        gm_args = (gmlp_norm_g[l], gmlp_w_s[l], gmlp_b_s[l])

        ctx_states.append((k_new, v_new,
                           hf_re.reshape(BATCH, 2, S5_GROUPS, S5_STATE), hf_im.reshape(BATCH, 2, S5_GROUPS, S5_STATE),
                           c_new, n_new.reshape(BATCH, 2, nh, MLSTM_DIM), m_new.reshape(BATCH, 2, nh)))

        w_o = w_out[l].astype(BF16)
        final = l == DEPTH - 1
        mixed = (z, (att_c, att_l), s5_args, (ml_c_out, ml_l_out), gm_args)
        if use_moe:
            x1, h2, route = _out_proj(*mixed, x, mod[l], norm2_g[l], w_o,
                                      router=(moe_router[j], moe_router_bias[j]))
            src, slot, tile_e, tile_ns = _route_plan(route)
            ys = _moe_ffn(h2, tile_e, tile_ns, src, moe_w_gate[j].astype(BF16), moe_w_up[j].astype(BF16),
                          moe_w_down[j].astype(BF16))
            x = _residual(x1, ys, mod[l], final_norm_g, route=route, slot=slot, final=final)
        else:
            x1, h2 = _out_proj(*mixed, x, mod[l], norm2_g[l], w_o)
            y = _ffn(h2, ffn_w_gate[j].astype(BF16), ffn_w_up[j].astype(BF16), ffn_w_down[j].astype(BF16))
            if final:
                x = _residual(x1, y, mod[l], final_norm_g, final=True)
            else:
                x = (x1, y, mod[l])

    y_prompt = x[0].reshape(BATCH, SEQ, D_MODEL)
    y_sample = x[1].reshape(DEC_BATCH, DEC_SEQ, D_MODEL)
    stack = lambda i: jnp.stack([s[i] for s in ctx_states], axis=1)
    return (y_prompt, y_sample, stack(0), stack(1), stack(2), stack(3), stack(4), stack(5), stack(6))
```

```python
import functools

import jax
import jax.numpy as jnp
from jax import lax
from jax.experimental import pallas as pl
from jax.experimental.pallas import tpu as pltpu

F32 = jnp.float32
BF16 = jnp.bfloat16

D_MODEL = 2048
BATCH = 16
SEQ = 256
DEPTH = 2
DEC_BATCH = 2
DEC_SEQ = 2048
PAST_LEN = 256
GRID_W = 64
MIX_W = 512
ATT_HEADS = 4
ATT_KV_HEADS = 2
HEAD_DIM = 128
ROPE_THETA = 10000.0
S5_CH = 16
S5_GROUPS = 32
S5_STATE = 64
MLSTM_HEADS = 4
MLSTM_DIM = 128
MLSTM_HG = 1
CHUNK = 128
GMLP_GROUPS = 4
D_FF = 7168
N_EXPERTS = 8
MOD_CHUNKS = 6
EPS = 1e-6
LOG2_E = 1.4426950408889634

T_CTX = BATCH * SEQ
T_LAT = DEC_BATCH * DEC_SEQ
T_ALL = T_CTX + T_LAT
N_GROUPS_MOD = 1 + DEC_BATCH

Z_AQ, Z_AK, Z_AV, Z_SX = 0, 512, 768, 1024
Z_MQ, Z_MK, Z_MV, Z_MO = 1536, 2048, 2560, 3072
Z_GU, Z_GV, Z_MG = 3584, 4096, 4608
Z_COLS = 4736
W_GATES0, W_GATES1 = 3584, 3600
LANE = 128
SUBLANE = 8

VMEM_LIMIT = 56 * 1024 * 1024

S5_SG = 4
S5_SGW = 8 * S5_STATE
S5_LAGS = 8
S5_CTX_NSEQ = 1

FFN_TM = 1024
FFN_SUB = 256
FFN_DENSE_SUB = 512
FFN_TF = 512
FFN_DENSE_TF = 1024
MOE_TILES = 2 * T_ALL // FFN_TM + N_EXPERTS
ISSUE_UNROLL = 8


def _cparams(sem=None):
    return pltpu.CompilerParams(dimension_semantics=sem, vmem_limit_bytes=VMEM_LIMIT)


def _mod_group(i, tm):
    return jnp.maximum(i * tm // DEC_SEQ - (T_CTX // DEC_SEQ - 1), 0)


def _rms(x, g):
    return x * lax.rsqrt(jnp.mean(x * x, axis=-1, keepdims=True) + EPS) * g


MOD_TK = 256


def _mod_kernel(cb_ref, w_ref, b_ref, o_ref, silu_scr):
    k = pl.program_id(1)
    n = MOD_CHUNKS * D_MODEL

    @pl.when((pl.program_id(0) == 0) & (k == 0))
    def _():
        c = cb_ref[...]
        silu_scr[...] = c * jax.nn.sigmoid(c)

    @pl.when(k == 0)
    def _():
        o_ref[0] = jnp.broadcast_to(b_ref[0], (SUBLANE, n))

    rows = pl.ds(pl.multiple_of(k * MOD_TK, MOD_TK), MOD_TK)
    for r in range(N_GROUPS_MOD):
        s = silu_scr[r, rows, :]
        for j in range(n // LANE):
            cols = slice(j * LANE, (j + 1) * LANE)
            o_ref[0, r:r + 1, cols] += jnp.sum(w_ref[0, :, cols] * s, axis=0, keepdims=True)


def _modulation(cond, w_mod, b_mod):
    cb = jnp.broadcast_to(cond[:, :, None], (N_GROUPS_MOD, D_MODEL, LANE))
    n = MOD_CHUNKS * D_MODEL
    out = pl.pallas_call(
        _mod_kernel,
        grid=(DEPTH, D_MODEL // MOD_TK),
        in_specs=[pl.BlockSpec((N_GROUPS_MOD, D_MODEL, LANE), lambda l, k: (0, 0, 0)),
                  pl.BlockSpec((1, MOD_TK, n), lambda l, k: (l, k, 0)),
                  pl.BlockSpec((1, 1, n), lambda l, k: (l, 0, 0))],
        out_specs=pl.BlockSpec((1, SUBLANE, n), lambda l, k: (l, 0, 0)),
        out_shape=jax.ShapeDtypeStruct((DEPTH, SUBLANE, n), F32),
        scratch_shapes=[pltpu.VMEM((N_GROUPS_MOD, D_MODEL, LANE), F32)],
        compiler_params=_cparams(("arbitrary", "arbitrary")),
        name="adaln_mod",
    )(cb, w_mod, b_mod.reshape(DEPTH, 1, n))
    return out[:, :N_GROUPS_MOD].reshape(DEPTH, N_GROUPS_MOD, MOD_CHUNKS, D_MODEL)


IN_TM = 256
IN_CHUNK = 512


def _pair_specs(tm, block, rows_dim=0):
    nc = T_CTX // tm

    def index(row):
        return tuple(row if d == rows_dim else 0 for d in range(len(block)))

    return [pl.BlockSpec(block, lambda i, *_: index(jnp.minimum(i, nc - 1))),
            pl.BlockSpec(block, lambda i, *_: index(jnp.maximum(i - nc, 0)))]


def _pick(tm, ctx_ref, lat_ref):
    return jnp.where(pl.program_id(0) < T_CTX // tm, ctx_ref[...], lat_ref[...])


def _store_pair(tm, val, oc_ref, ol_ref):
    i = pl.program_id(0)

    @pl.when(i < T_CTX // tm)
    def _():
        oc_ref[...] = val

    @pl.when(i >= T_CTX // tm)
    def _():
        ol_ref[...] = val


def _in_kernel(*refs, pending):
    if pending:
        x1_ref, y_ref, modp_ref, mod_ref, g_ref, wa_ref, wb_ref, wc_ref, z_ref, oc_ref, ol_ref = refs
        x = x1_ref[...] + modp_ref[0, 5:6, :] * y_ref[...]
        _store_pair(IN_TM, x, oc_ref, ol_ref)
    else:
        xc_ref, xl_ref, mod_ref, g_ref, wa_ref, wb_ref, wc_ref, z_ref = refs
        x = _pick(IN_TM, xc_ref, xl_ref)
    shift = mod_ref[0, 0:1, :]
    scale = mod_ref[0, 1:2, :]
    h = (_rms(x, g_ref[...]) * (1.0 + scale) + shift).astype(BF16)
    for w_ref, z0 in ((wa_ref, 0), (wb_ref, Z_GU), (wc_ref, Z_MG)):
        width = w_ref.shape[1]
        for c0 in range(0, width, IN_CHUNK):
            cw = min(IN_CHUNK, width - c0)
            z_ref[:, z0 + c0:z0 + c0 + cw] = jnp.dot(h, w_ref[:, c0:c0 + cw], preferred_element_type=F32)


def _in_proj(x, mod_l, g, w_in_l):
    tm = IN_TM
    w_segs = [w_in_l[:, :W_GATES0].astype(BF16), w_in_l[:, W_GATES1:].astype(BF16),
              jnp.pad(w_in_l[:, W_GATES0:W_GATES1], ((0, 0), (0, Z_COLS - Z_MG - (W_GATES1 - W_GATES0)))).astype(BF16)]
    w_specs = [pl.BlockSpec(w.shape, lambda i: (0, 0), pipeline_mode=pl.Buffered(1)) for w in w_segs]
    pending = len(x) == 3
    row = lambda i: (i, 0)
    mod_spec = pl.BlockSpec((1, MOD_CHUNKS, D_MODEL), lambda i: (_mod_group(i, tm), 0, 0))
    if pending:
        x_specs = [pl.BlockSpec((tm, D_MODEL), row), pl.BlockSpec((tm, D_MODEL), row), mod_spec]
    else:
        x_specs = _pair_specs(tm, (tm, D_MODEL))
    out_specs = [pl.BlockSpec((tm, Z_COLS), row)]
    out_shape = [jax.ShapeDtypeStruct((T_ALL, Z_COLS), F32)]
    if pending:
        out_specs += _pair_specs(tm, (tm, D_MODEL))
        out_shape += [jax.ShapeDtypeStruct((T_CTX, D_MODEL), F32), jax.ShapeDtypeStruct((T_LAT, D_MODEL), F32)]
    outs = pl.pallas_call(
        functools.partial(_in_kernel, pending=pending),
        grid=(T_ALL // tm,),
        in_specs=x_specs + [
                  mod_spec,
                  pl.BlockSpec((1, D_MODEL), lambda i: (0, 0))] + w_specs,
        out_specs=out_specs,
        out_shape=out_shape,
        compiler_params=_cparams(("arbitrary",)),
        name="in_proj_res" if pending else "in_proj",
    )(*x, mod_l, g.reshape(1, D_MODEL), *w_segs)
    return (outs[0], (outs[1], outs[2])) if pending else (outs[0], x)


def _rope(t, c, s):
    lane = lax.broadcasted_iota(jnp.int32, t.shape, 1)
    first = (lane % (HEAD_DIM // 2)) < (HEAD_DIM // 4)
    swapped = jnp.where(first, pltpu.roll(t, HEAD_DIM - HEAD_DIM // 4, 1), pltpu.roll(t, HEAD_DIM // 4, 1))
    return t * c + swapped * s


def _attn_kernel(*refs, seq, past, rope):
    if rope:
        (aq_ref, ak_ref, av_ref, kp_ref, vp_ref, cos_ref, sin_ref, qg_ref, kg_ref,
         att_ref, kb_scr, vb_scr) = refs
    else:
        aq_ref, ak_ref, av_ref, qg_ref, kg_ref, att_ref, knew_ref, vnew_ref, kb_scr, vb_scr = refs
    kn = _rms(ak_ref[...], kg_ref[...])
    v = av_ref[...]
    if rope:
        kn = _rope(kn, cos_ref[...], sin_ref[...])
        kb_scr[seq:seq + past, :] = kp_ref[0, 0].astype(BF16)
        vb_scr[seq:seq + past, :] = vp_ref[0, 0].astype(BF16)
    else:
        knew_ref[0, 0] = kn
        vnew_ref[0, 0] = v
    kb_scr[0:seq, :] = kn.astype(BF16)
    vb_scr[0:seq, :] = v.astype(BF16)
    grp = ATT_HEADS // ATT_KV_HEADS

    def q_block(qb, carry):
        rows = pl.ds(pl.multiple_of(qb * CHUNK, CHUNK), CHUNK)
        qs = []
        for g in range(grp):
            q = _rms(aq_ref[rows, g * HEAD_DIM:(g + 1) * HEAD_DIM], qg_ref[...])
            if rope:
                q = _rope(q, cos_ref[rows, :], sin_ref[rows, :])
            qs.append(q * (HEAD_DIM ** -0.5 * LOG2_E))
        q2 = jnp.concatenate(qs, axis=0).astype(BF16)
        s = lax.dot_general(q2, kb_scr[...], (((1,), (1,)), ((), ())), preferred_element_type=F32)
        m = jnp.max(s, axis=-1, keepdims=True)
        p = jnp.exp2(s - m)
        den = jnp.sum(p, axis=-1, keepdims=True)
        o = jnp.dot(p.astype(BF16), vb_scr[...], preferred_element_type=F32) / den
        for g in range(grp):
            att_ref[rows, g * HEAD_DIM:(g + 1) * HEAD_DIM] = o[g * CHUNK:(g + 1) * CHUNK].astype(BF16)
        return carry

    lax.fori_loop(0, seq // CHUNK, q_block, 0)


def _attention(z, qg, kg, *, ctx, kpast=None, vpast=None, cos=None, sin=None):
    seq = SEQ if ctx else DEC_SEQ
    nb = BATCH if ctx else DEC_BATCH
    row0 = 0 if ctx else T_CTX // DEC_SEQ
    past = 0 if ctx else PAST_LEN
    qw = HEAD_DIM * (ATT_HEADS // ATT_KV_HEADS)
    in_specs = [pl.BlockSpec((seq, qw), lambda b, h: (row0 + b, Z_AQ // qw + h)),
                pl.BlockSpec((seq, HEAD_DIM), lambda b, h: (row0 + b, Z_AK // HEAD_DIM + h)),
                pl.BlockSpec((seq, HEAD_DIM), lambda b, h: (row0 + b, Z_AV // HEAD_DIM + h))]
    args = [z, z, z]
    if not ctx:
        in_specs += [pl.BlockSpec((1, 1, past, HEAD_DIM), lambda b, h: (b, h, 0, 0)),
                     pl.BlockSpec((1, 1, past, HEAD_DIM), lambda b, h: (b, h, 0, 0)),
                     pl.BlockSpec((seq, HEAD_DIM), lambda b, h: (0, 0)),
                     pl.BlockSpec((seq, HEAD_DIM), lambda b, h: (0, 0))]
        args += [kpast, vpast, cos, sin]
    in_specs += [pl.BlockSpec((1, HEAD_DIM), lambda b, h: (0, 0)),
                 pl.BlockSpec((1, HEAD_DIM), lambda b, h: (0, 0))]
    args += [qg.reshape(1, HEAD_DIM), kg.reshape(1, HEAD_DIM)]
    out_specs = [pl.BlockSpec((seq, qw), lambda b, h: (b, h))]
    out_shape = [jax.ShapeDtypeStruct((nb * seq, MIX_W), BF16)]
    if ctx:
        out_specs += [pl.BlockSpec((1, 1, seq, HEAD_DIM), lambda b, h: (b, h, 0, 0))] * 2
        out_shape += [jax.ShapeDtypeStruct((nb, ATT_KV_HEADS, seq, HEAD_DIM), F32)] * 2
    return pl.pallas_call(
        functools.partial(_attn_kernel, seq=seq, past=past, rope=not ctx),
        grid=(nb, ATT_KV_HEADS),
        in_specs=in_specs, out_specs=out_specs, out_shape=out_shape,
        scratch_shapes=[pltpu.VMEM((seq + past, HEAD_DIM), BF16),
                        pltpu.VMEM((seq + past, HEAD_DIM), BF16)],
        compiler_params=_cparams(("arbitrary", "arbitrary")),
        name="attn_ctx" if ctx else "attn_lat",
    )(*args)


def _s5_prep_kernel(are_ref, aim_ref, ldt_ref, bre_ref, bim_ref, pre_ref, pim_ref, wre_ref, wim_ref):
    a_re = are_ref[...]
    a_im = aim_ref[...]
    dt = jnp.exp(ldt_ref[...])
    pows = []
    for tau in range(S5_LAGS + 1):
        mag = jnp.exp((tau * dt) * a_re)
        ang = (tau * dt) * a_im
        pr, pi = mag * jnp.cos(ang), mag * jnp.sin(ang)
        pre_ref[tau] = pr
        pim_ref[tau] = pi
        pows.append((pr, pi))
    nr, ni = pows[1][0] - 1.0, pows[1][1]
    den = a_re * a_re + a_im * a_im
    cr = (nr * a_re + ni * a_im) / den
    ci = (ni * a_re - nr * a_im) / den
    for d in range(2):
        b_r, b_i = bre_ref[d], bim_ref[d]
        bb_r = cr[d:d + 1] * b_r - ci[d:d + 1] * b_i
        bb_i = cr[d:d + 1] * b_i + ci[d:d + 1] * b_r
        for tau in range(S5_LAGS):
            pr, pi = pows[tau][0][d:d + 1], pows[tau][1][d:d + 1]
            wre_ref[d, tau] = pr * bb_r - pi * bb_i
            wim_ref[d, tau] = pr * bb_i + pi * bb_r


def _s5_prep(a_re, a_im, log_dt, b_re, b_im, c_re, c_im):
    gp = S5_GROUPS * S5_STATE
    ldt = jnp.broadcast_to(log_dt[:, :, None], (2, S5_GROUPS, S5_STATE)).reshape(2, gp)
    bt = lambda b: b.transpose(0, 3, 1, 2).reshape(2, S5_CH, gp)
    pre, pim, wre, wim = pl.pallas_call(
        _s5_prep_kernel,
        out_shape=[jax.ShapeDtypeStruct((S5_LAGS + 1, 2, gp), F32)] * 2
        + [jax.ShapeDtypeStruct((2, S5_LAGS, S5_CH, gp), F32)] * 2,
        name="s5_prep",
    )(a_re.reshape(2, gp), a_im.reshape(2, gp), ldt, bt(b_re), bt(b_im))
    eye = jnp.eye(8, dtype=F32)

    hw = S5_SGW // 2
    half_mask = (jnp.arange(hw)[None, :] // S5_STATE == jnp.arange(4)[:, None]).astype(F32)

    def w_layout(w):
        w = w.reshape(2, S5_LAGS // 2, 2, S5_CH, S5_SG, 2, hw).transpose(0, 4, 5, 1, 2, 3, 6)
        w = w[:, :, :, :, :, None, :, :] * half_mask[:, None, :]
        return w.reshape(2, S5_SG, 2, S5_LAGS * LANE // 2, hw)

    w_in = jnp.concatenate([w_layout(wre), w_layout(wim)], axis=-1).astype(BF16)

    def c_layout(c):
        c = c.reshape(2, S5_SG, 8, S5_CH, S5_STATE).transpose(0, 1, 2, 4, 3)
        c = c[:, :, :, :, None, :] * eye[None, None, :, None, :, None]
        return c.reshape(2, S5_SG, S5_SGW, LANE)

    w_out = jnp.concatenate([c_layout(c_re), -c_layout(c_im)], axis=2).astype(BF16)

    def p_layout(p):
        p = p[1:].reshape(S5_LAGS, 2, S5_SG, S5_SGW).transpose(1, 2, 0, 3)
        return jnp.stack([p[0], p[1, :, ::-1]], axis=0)

    def a8_layout(p):
        p = p[S5_LAGS].reshape(2, S5_SG, 1, S5_SGW)
        return jnp.broadcast_to(p, (2, S5_SG, SUBLANE, S5_SGW))

    return w_in, w_out, a8_layout(pre), a8_layout(pim), p_layout(pre), p_layout(pim)


def _s5_kernel(u_ref, w_ref, c_ref, a8r_ref, a8i_ref, pwr_ref, pwi_ref, h0r_ref, h0i_ref,
               y_ref, hfr_ref, hfi_ref, upad, wbr, wbi, *, seq, tc, nseq):
    d = pl.program_id(0)
    zeros = jnp.zeros((SUBLANE, LANE), F32)
    for q in range(nseq):
        upad[q, 0:SUBLANE, :] = zeros
        upad[q, SUBLANE:seq + SUBLANE, :] = u_ref[q * seq:(q + 1) * seq, :]
        upad[q, seq + SUBLANE:seq + 2 * SUBLANE, :] = zeros
    nch = seq // tc
    nt = tc // SUBLANE
    a8r = a8r_ref[0, 0]
    a8i = a8i_ref[0, 0]

    def run(fwd):
        pwr = pwr_ref[0, 0]
        pwi = pwi_ref[0, 0]
        state = tuple((jnp.zeros((SUBLANE, S5_SGW), F32), jnp.zeros((SUBLANE, S5_SGW), F32))
                      for _ in range(nseq))
        for ci in range(nch):
            c = ci if fwd else nch - 1 - ci
            per_seq = []
            for q in range(nseq):
                if fwd:
                    win = upad[q, c * tc:c * tc + tc + SUBLANE, :]
                    per_seq.append([pltpu.roll(win, tau, 0)[SUBLANE:SUBLANE + tc] if tau
                                    else win[SUBLANE:SUBLANE + tc] for tau in range(S5_LAGS)])
                else:
                    win = upad[q, c * tc + SUBLANE:c * tc + tc + 2 * SUBLANE, :]
                    per_seq.append([pltpu.roll(win, tc + SUBLANE - tau, 0)[0:tc] if tau else win[0:tc]
                                    for tau in range(S5_LAGS)])
            lags = [jnp.concatenate([per_seq[q][tau] for q in range(nseq)], axis=0) if nseq > 1
                    else per_seq[0][tau] for tau in range(S5_LAGS)]
            low = lax.broadcasted_iota(jnp.int32, (nseq * tc, LANE), 1) < LANE // 2
            slabs = ([], [])
            for k in range(S5_LAGS // 2):
                a, b = lags[2 * k], lags[2 * k + 1]
                slabs[0].append(jnp.where(low, a, pltpu.roll(b, LANE // 2, 1)).astype(BF16))
                slabs[1].append(jnp.where(low, pltpu.roll(a, LANE // 2, 1), b).astype(BF16))
            hw = S5_SGW // 2
            for half in range(2):
                w = jnp.dot(jnp.concatenate(slabs[half], axis=1), w_ref[0, 0, half],
                            preferred_element_type=F32)
                wbr[:, half * hw:(half + 1) * hw] = w[:, :hw]
                wbi[:, half * hw:(half + 1) * hw] = w[:, hw:]
            if ci == 0:
                for q in range(nseq):
                    h0r = h0r_ref[q, 0, 0]
                    h0i = h0i_ref[q, 0, 0]
                    r0 = q * tc + (0 if fwd else tc - SUBLANE)
                    wbr[r0:r0 + SUBLANE, :] = wbr[r0:r0 + SUBLANE, :] + (pwr * h0r - pwi * h0i)
                    wbi[r0:r0 + SUBLANE, :] = wbi[r0:r0 + SUBLANE, :] + (pwr * h0i + pwi * h0r)

            def step(i, carry):
                t = i if fwd else nt - 1 - i
                new = []
                for q in range(nseq):
                    cr, ci_ = carry[q]
                    rows = pl.ds(pl.multiple_of(q * tc + t * SUBLANE, SUBLANE), SUBLANE)
                    nr = a8r * cr - a8i * ci_ + wbr[rows, :]
                    ni = a8r * ci_ + a8i * cr + wbi[rows, :]
                    wbr[rows, :] = nr
                    wbi[rows, :] = ni
                    new.append((nr, ni))
                return tuple(new)

            state = lax.fori_loop(0, nt, step, state)
            hcat = jnp.concatenate([wbr[...].astype(BF16), wbi[...].astype(BF16)], axis=1)
            y = jnp.dot(hcat, c_ref[0, 0], preferred_element_type=F32)
            for q in range(nseq):
                y_ref[0, q * seq + c * tc:q * seq + (c + 1) * tc, :] = y[q * tc:(q + 1) * tc]
        last = SUBLANE - 1 if fwd else 0
        for q in range(nseq):
            hfr_ref[q, 0, 0] = state[q][0][last:last + 1]
            hfi_ref[q, 0, 0] = state[q][1][last:last + 1]

    @pl.when(d == 0)
    def _():
        run(True)

    @pl.when(d == 1)
    def _():
        run(False)


def _s5_scan(z, prep, h0r, h0i, *, ctx):
    w_in, w_out, a8r, a8i, pwr, pwi = prep
    seq = SEQ if ctx else DEC_SEQ
    nb = BATCH if ctx else DEC_BATCH
    row0 = 0 if ctx else T_CTX // DEC_SEQ
    tc = min(seq, 512)
    nseq = S5_CTX_NSEQ if ctx else 1
    par = lambda d, s, b: (d, s, 0, 0)
    st = lambda d, s, b: (b, d, s, 0, 0)
    return pl.pallas_call(
        functools.partial(_s5_kernel, seq=seq, tc=tc, nseq=nseq),
        grid=(2, S5_SG, nb // nseq),
        in_specs=[pl.BlockSpec((nseq * seq, LANE), lambda d, s, b: (row0 + b, Z_SX // LANE + s)),
                  pl.BlockSpec((1, 1, 2, S5_LAGS * LANE // 2, S5_SGW), lambda d, s, b: (d, s, 0, 0, 0)),
                  pl.BlockSpec((1, 1, 2 * S5_SGW, LANE), par),
                  pl.BlockSpec((1, 1, SUBLANE, S5_SGW), par),
                  pl.BlockSpec((1, 1, SUBLANE, S5_SGW), par),
                  pl.BlockSpec((1, 1, SUBLANE, S5_SGW), par),
                  pl.BlockSpec((1, 1, SUBLANE, S5_SGW), par),
                  pl.BlockSpec((nseq, 1, 1, 1, S5_SGW), st),
                  pl.BlockSpec((nseq, 1, 1, 1, S5_SGW), st)],
        out_specs=[pl.BlockSpec((1, nseq * seq, LANE), lambda d, s, b: (d, b, s)),
                   pl.BlockSpec((nseq, 1, 1, 1, S5_SGW), st),
                   pl.BlockSpec((nseq, 1, 1, 1, S5_SGW), st)],
        out_shape=[jax.ShapeDtypeStruct((2, nb * seq, MIX_W), F32),
                   jax.ShapeDtypeStruct((nb, 2, S5_SG, 1, S5_SGW), F32),
                   jax.ShapeDtypeStruct((nb, 2, S5_SG, 1, S5_SGW), F32)],
        scratch_shapes=[pltpu.VMEM((nseq, seq + 2 * SUBLANE, LANE), F32),
                        pltpu.VMEM((nseq * tc, S5_SGW), F32),
                        pltpu.VMEM((nseq * tc, S5_SGW), F32)],
        compiler_params=_cparams(("arbitrary", "arbitrary", "arbitrary")),
        name="s5_ctx" if ctx else "s5_lat",
    )(z, w_in, w_out, a8r, a8i, pwr, pwi, h0r, h0i)


def _mlstm_kernel(q_ref, k_ref, v_ref, mo_ref, gc_ref, gr_ref, bc_ref, br_ref, c0_ref, n0_ref, m0_ref,
                  ng_ref, h_ref, c_ref, n_ref, m_ref, hs_scr, *, seq, chunk0):
    nc = seq // CHUNK
    ii = lax.broadcasted_iota(jnp.int32, (CHUNK, CHUNK), 0)
    jj = lax.broadcasted_iota(jnp.int32, (CHUNK, CHUNK), 1)
    neg_inf = jnp.float32(-jnp.inf)

    def chunk_step(hh, d, cidx, carry):
        cols = slice(hh * MLSTM_DIM, (hh + 1) * MLSTM_DIM)
        fwd = d == 0
        mask = (jj <= ii) if fwd else (jj >= ii)
        mask_t = (ii <= jj) if fwd else (ii >= jj)
        last = CHUNK - 1 if fwd else 0
        c_st, n_st, m_st = carry
        rows = pl.ds(pl.multiple_of(cidx * CHUNK, CHUNK), CHUNK)
        q = q_ref[rows, cols]
        k = k_ref[rows, cols] * (MLSTM_DIM ** -0.5)
        v = v_ref[rows, cols]
        gcol = gc_ref[hh, rows, :] + bc_ref[hh]
        grow = gr_ref[hh, chunk0 + cidx] + br_ref[hh]
        li_c = gcol[:, 2 * d:2 * d + 1]
        lf_c = jax.nn.log_sigmoid(gcol[:, 2 * d + 1:2 * d + 2])
        li_r = grow[2 * d:2 * d + 1, :]
        lf_r = jax.nn.log_sigmoid(grow[2 * d + 1:2 * d + 2, :])
        b_c = jnp.sum(jnp.where(mask, lf_r, 0.0), axis=1, keepdims=True)
        b_r = jnp.sum(jnp.where(mask_t, lf_c, 0.0), axis=0, keepdims=True)
        dmat = jnp.where(mask, b_c - b_r + li_r, neg_inf)
        inter = b_c + m_st
        m_j = jnp.maximum(inter, jnp.max(dmat, axis=1, keepdims=True))
        qb = q.astype(BF16)
        vb = v.astype(BF16)
        s = lax.dot_general(qb, k.astype(BF16), (((1,), (1,)), ((), ())),
                            preferred_element_type=F32) * jnp.exp(dmat - m_j)
        w_inter = jnp.exp(inter - m_j)
        num = (jnp.dot(s.astype(BF16), vb, preferred_element_type=F32)
               + w_inter * jnp.dot(qb, c_st.astype(BF16), preferred_element_type=F32))
        den = jnp.sum(s, axis=1, keepdims=True) + w_inter * jnp.sum(q * n_st, axis=1, keepdims=True)
        hs_scr[d, rows, cols] = num / jnp.maximum(jnp.abs(den), jnp.exp(-m_j))
        m_end = m_j[last:last + 1, :]
        b_end = b_c[last:last + 1, :]
        w_c = jnp.exp(b_end - b_c + li_c - m_end)
        decay = jnp.exp(b_end + m_st - m_end)
        kw = k * w_c
        c_new = decay * c_st + lax.dot_general(kw.astype(BF16), vb, (((0,), (0,)), ((), ())),
                                               preferred_element_type=F32)
        n_new = decay * n_st + jnp.sum(kw, axis=0, keepdims=True)
        return c_new, n_new, m_end

    chains = [(hh, d) for hh in range(MLSTM_HG) for d in range(2)]

    def all_chains(ci, carry):
        return tuple(chunk_step(hh, d, ci if d == 0 else nc - 1 - ci, carry[n])
                     for n, (hh, d) in enumerate(chains))

    init = tuple((c0_ref[0, d, hh], n0_ref[0, d, hh], m0_ref[0, d, hh]) for hh, d in chains)
    final = lax.fori_loop(0, nc, all_chains, init)
    for n, (hh, d) in enumerate(chains):
        c_ref[0, d, hh], n_ref[0, d, hh], m_ref[0, d, hh] = final[n]
    for hh in range(MLSTM_HG):
        cols = slice(hh * MLSTM_DIM, (hh + 1) * MLSTM_DIM)
        h = _rms(hs_scr[0, :, cols] + hs_scr[1, :, cols], ng_ref[hh])
        h_ref[:, cols] = (h * jax.nn.sigmoid(mo_ref[:, cols])).astype(BF16)


def _mlstm(z, gcol, grow, bias_c, bias_r, c0, n0, m0, norm_g, *, ctx):
    seq = SEQ if ctx else DEC_SEQ
    nb = BATCH if ctx else DEC_BATCH
    row0 = 0 if ctx else T_CTX // DEC_SEQ
    hd = MLSTM_DIM
    nh = MLSTM_HEADS
    hg = MLSTM_HG
    hw = hg * hd
    zcol = lambda off: (lambda b, h: (row0 + b, off // hw + h))
    st5 = lambda b, h: (b, 0, h, 0, 0)
    return pl.pallas_call(
        functools.partial(_mlstm_kernel, seq=seq, chunk0=0),
        grid=(nb, nh // hg),
        in_specs=[pl.BlockSpec((seq, hw), zcol(Z_MQ)),
                  pl.BlockSpec((seq, hw), zcol(Z_MK)),
                  pl.BlockSpec((seq, hw), zcol(Z_MV)),
                  pl.BlockSpec((seq, hw), zcol(Z_MO)),
                  pl.BlockSpec((hg, seq, 4), lambda b, h: (h, row0 + b, 0)),
                  pl.BlockSpec((hg, seq // CHUNK, 4, CHUNK), lambda b, h: (h, row0 + b, 0, 0)),
                  pl.BlockSpec((hg, 1, 4), lambda b, h: (h, 0, 0)),
                  pl.BlockSpec((hg, 4, 1), lambda b, h: (h, 0, 0)),
                  pl.BlockSpec((1, 2, hg, hd, hd), st5),
                  pl.BlockSpec((1, 2, hg, 1, hd), st5),
                  pl.BlockSpec((1, 2, hg, 1, 1), st5),
                  pl.BlockSpec((hg, 1, hd), lambda b, h: (h, 0, 0))],
        out_specs=[pl.BlockSpec((seq, hw), lambda b, h: (b, h)),
                   pl.BlockSpec((1, 2, hg, hd, hd), st5),
                   pl.BlockSpec((1, 2, hg, 1, hd), st5),
                   pl.BlockSpec((1, 2, hg, 1, 1), st5)],
        out_shape=[jax.ShapeDtypeStruct((nb * seq, MIX_W), BF16),
                   jax.ShapeDtypeStruct((nb, 2, nh, hd, hd), F32),
                   jax.ShapeDtypeStruct((nb, 2, nh, 1, hd), F32),
                   jax.ShapeDtypeStruct((nb, 2, nh, 1, 1), F32)],
        scratch_shapes=[pltpu.VMEM((2, seq, hw), F32)],
        compiler_params=_cparams(("arbitrary", "arbitrary")),
        name="mlstm_ctx" if ctx else "mlstm_lat",
    )(z, z, z, z, gcol, grow, bias_c, bias_r, c0, n0, m0, norm_g.reshape(nh, 1, hd))


OUT_TM = 256


def _out_kernel(*refs, router):
    (ac_ref, al_ref, yc_ref, yl_ref, sx_ref, sd_ref, sw_ref, sb_ref, cc_ref, cl_ref,
     gu_ref, gv_ref, gn_ref, gw_ref, gb_ref, xc_ref, xl_ref, mod_ref, g_ref, w_ref), refs = refs[:20], refs[20:]
    if router:
        rw_ref, rb_ref, x1_ref, h2_ref, rt_ref = refs
    else:
        x1_ref, h2_ref = refs
    y2 = _pick(OUT_TM, yc_ref, yl_ref)
    y = jax.nn.gelu(y2[0] + y2[1] + sd_ref[...] * sx_ref[...])
    gate = jnp.dot(y.astype(BF16), sw_ref[...], preferred_element_type=F32) + sb_ref[...]
    s5_out = (y * jax.nn.sigmoid(gate)).astype(BF16)
    vn = _rms(gv_ref[...], gn_ref[...]).astype(BF16)
    gw = MIX_W // GMLP_GROUPS
    chunks = []
    for c in range(OUT_TM // CHUNK):
        r = slice(c * CHUNK, (c + 1) * CHUNK)
        groups = []
        for g in range(GMLP_GROUPS):
            cs = slice(g * gw, (g + 1) * gw)
            mixed = jnp.dot(gw_ref[g], vn[r, cs], preferred_element_type=F32) + gb_ref[:, g:g + 1]
            groups.append((gu_ref[r, cs] * mixed).astype(BF16))
        chunks.append(jnp.concatenate(groups, axis=1))
    gm_out = jnp.concatenate(chunks, axis=0)
    mix = jnp.dot(_pick(OUT_TM, ac_ref, al_ref), w_ref[0:MIX_W, :], preferred_element_type=F32)
    mix += jnp.dot(s5_out, w_ref[MIX_W:2 * MIX_W, :], preferred_element_type=F32)
    mix += jnp.dot(_pick(OUT_TM, cc_ref, cl_ref), w_ref[2 * MIX_W:3 * MIX_W, :], preferred_element_type=F32)
    mix += jnp.dot(gm_out, w_ref[3 * MIX_W:4 * MIX_W, :], preferred_element_type=F32)
    x1 = _pick(OUT_TM, xc_ref, xl_ref) + mod_ref[0, 2:3, :] * mix
    x1_ref[...] = x1
    h2 = _rms(x1, g_ref[...]) * (1.0 + mod_ref[0, 4:5, :]) + mod_ref[0, 3:4, :]
    hi = h2.astype(BF16)
    h2_ref[...] = h2 if router else hi
    if router:
        lo = (h2 - hi.astype(F32)).astype(BF16)
        both = jnp.dot(hi, rw_ref[...], preferred_element_type=F32)
        logits = (both[:, :LANE] + jnp.dot(lo, rw_ref[:, :LANE], preferred_element_type=F32)
                  + both[:, LANE:]) + rb_ref[...]
        lane = lax.broadcasted_iota(jnp.int32, logits.shape, 1)
        neg_inf = jnp.float32(-jnp.inf)
        lg = jnp.where(lane < N_EXPERTS, logits, neg_inf)
        m1 = jnp.max(lg, axis=-1, keepdims=True)
        i1 = jnp.min(jnp.where(lg == m1, lane, LANE), axis=-1, keepdims=True)
        lg2 = jnp.where(lane == i1, neg_inf, lg)
        m2 = jnp.max(lg2, axis=-1, keepdims=True)
        i2 = jnp.min(jnp.where(lg2 == m2, lane, LANE), axis=-1, keepdims=True)
        e = jnp.exp(m2 - m1)
        w1 = 1.0 / (1.0 + e)
        w2 = e / (1.0 + e)
        rt = jnp.where(lane == 0, i1.astype(F32), 0.0)
        rt = jnp.where(lane == 1, i2.astype(F32), rt)
        rt = jnp.where(lane == 2, w1, rt)
        rt = jnp.where(lane == 3, w2, rt)
        rt_ref[...] = rt


def _out_proj(z, att_pair, s5_args, ml_pair, gm_args, x_pair, mod_l, g2, w_out, router=None):
    tm = OUT_TM
    row = lambda i: (i, 0)
    fixed = lambda i: (0, 0)
    pair = _pair_specs(tm, (tm, MIX_W))
    zcols = lambda off: pl.BlockSpec((tm, MIX_W), lambda i: (i, off // MIX_W))
    vec = pl.BlockSpec((1, MIX_W), fixed)
    y_pair, d_skip, w_glu, b_glu = s5_args
    gm_norm_g, gm_w_s, gm_b_s = gm_args
    in_specs = (pair + _pair_specs(tm, (2, tm, MIX_W), rows_dim=1)
                + [zcols(Z_SX), vec, pl.BlockSpec((MIX_W, MIX_W), fixed), vec]
                + pair
                + [zcols(Z_GU), zcols(Z_GV), vec, pl.BlockSpec((GMLP_GROUPS, CHUNK, CHUNK), lambda i: (0, 0, 0)),
                   pl.BlockSpec((CHUNK, GMLP_GROUPS), fixed)]
                + _pair_specs(tm, (tm, D_MODEL))
                + [pl.BlockSpec((1, MOD_CHUNKS, D_MODEL), lambda i: (_mod_group(i, tm), 0, 0)),
                   pl.BlockSpec((1, D_MODEL), fixed),
                   pl.BlockSpec((D_MODEL, D_MODEL), fixed, pipeline_mode=pl.Buffered(1))])
    args = [*att_pair, *y_pair, z, d_skip.reshape(1, MIX_W), w_glu.astype(BF16), b_glu.reshape(1, MIX_W),
            *ml_pair, z, z, gm_norm_g.reshape(1, MIX_W), gm_w_s.astype(BF16), gm_b_s.T,
            *x_pair, mod_l, g2.reshape(1, D_MODEL), w_out]
    out_specs = [pl.BlockSpec((tm, D_MODEL), row), pl.BlockSpec((tm, D_MODEL), row)]
    out_shape = [jax.ShapeDtypeStruct((T_ALL, D_MODEL), F32),
                 jax.ShapeDtypeStruct((T_ALL, D_MODEL), BF16 if router is None else F32)]
    if router is not None:
        rw, rb = router
        rw = jnp.pad(rw, ((0, 0), (0, LANE - N_EXPERTS)))
        rh = rw.astype(BF16)
        rl = (rw - rh.astype(F32)).astype(BF16)
        in_specs += [pl.BlockSpec((D_MODEL, 2 * LANE), fixed), pl.BlockSpec((1, LANE), fixed)]
        args += [jnp.concatenate([rh, rl], axis=1), jnp.pad(rb, (0, LANE - N_EXPERTS)).reshape(1, LANE)]
        out_specs.append(pl.BlockSpec((tm, LANE), row))
        out_shape.append(jax.ShapeDtypeStruct((T_ALL, LANE), F32))
    return pl.pallas_call(
        functools.partial(_out_kernel, router=router is not None),
        grid=(T_ALL // tm,),
        in_specs=in_specs, out_specs=out_specs, out_shape=out_shape,
        compiler_params=_cparams(("arbitrary",)),
        name="out_proj_router" if router is not None else "out_proj",
    )(*args)


def _ffn_kernel(x_ref, wg_ref, wu_ref, wd_ref, o_ref):
    @pl.when(pl.program_id(1) == 0)
    def _():
        o_ref[...] = jnp.zeros_like(o_ref)

    def sub(s, carry):
        rows = pl.ds(pl.multiple_of(s * FFN_DENSE_SUB, FFN_DENSE_SUB), FFN_DENSE_SUB)
        xs = x_ref[rows, :]
        g = jnp.dot(xs, wg_ref[...], preferred_element_type=F32)
        u = jnp.dot(xs, wu_ref[...], preferred_element_type=F32)
        a = (g * jax.nn.sigmoid(g) * u).astype(BF16)
        o_ref[rows, :] += jnp.dot(a, wd_ref[...], preferred_element_type=F32)
        return carry

    lax.fori_loop(0, FFN_TM // FFN_DENSE_SUB, sub, 0)


def _ffn(x, w_gate, w_up, w_down):
    return pl.pallas_call(
        _ffn_kernel,
        grid=(T_ALL // FFN_TM, D_FF // FFN_DENSE_TF),
        in_specs=[pl.BlockSpec((FFN_TM, D_MODEL), lambda i, f: (i, 0)),
                  pl.BlockSpec((D_MODEL, FFN_DENSE_TF), lambda i, f: (0, f)),
                  pl.BlockSpec((D_MODEL, FFN_DENSE_TF), lambda i, f: (0, f)),
                  pl.BlockSpec((FFN_DENSE_TF, D_MODEL), lambda i, f: (f, 0))],
        out_specs=pl.BlockSpec((FFN_TM, D_MODEL), lambda i, f: (i, 0)),
        out_shape=jax.ShapeDtypeStruct((T_ALL, D_MODEL), F32),
        compiler_params=_cparams(("arbitrary", "arbitrary")),
        name="ffn",
    )(x, w_gate, w_up, w_down)


def _moe_ffn_kernel(te_ref, ns_ref, src_ref, x_hbm, wg_ref, wu_ref, wd_ref, o_ref, xf, xb, sems):
    i = pl.program_id(0)
    f = pl.program_id(1)
    ntiles = pl.num_programs(0)

    def row_copy(src_row, slot, r):
        return pltpu.make_async_copy(x_hbm.at[pl.ds(src_row, 1)], xf.at[slot, pl.ds(r, 1)], sems.at[slot])

    def issue(tile, slot):
        def body(r8, carry):
            for u in range(ISSUE_UNROLL):
                r = r8 * ISSUE_UNROLL + u
                row_copy(src_ref[tile * FFN_TM + r], slot, r).start()
            return carry

        lax.fori_loop(0, ns_ref[tile] * (FFN_SUB // ISSUE_UNROLL), body, 0)

    def wait_rows(tile, slot):
        def body(s, carry):
            rows = pl.ds(pl.multiple_of(s * FFN_SUB, FFN_SUB), FFN_SUB)
            pltpu.make_async_copy(x_hbm.at[pl.ds(0, FFN_SUB)], xf.at[slot, rows], sems.at[slot]).wait()
            return carry

        lax.fori_loop(0, ns_ref[tile], body, 0)

    @pl.when(f == 0)
    def _():
        slot = i % 2

        @pl.when(i == 0)
        def _():
            issue(0, 0)

        wait_rows(i, slot)

        @pl.when(i + 1 < ntiles)
        def _():
            issue(i + 1, 1 - slot)

        def cast(s, carry):
            rows = pl.ds(pl.multiple_of(s * FFN_SUB, FFN_SUB), FFN_SUB)
            xb[rows, :] = xf[slot, rows, :].astype(BF16)
            return carry

        lax.fori_loop(0, ns_ref[i], cast, 0)
        o_ref[...] = jnp.zeros_like(o_ref)

    ns = ns_ref[i]

    def block(row0, size):
        rows = pl.ds(pl.multiple_of(row0, size), size)
        xs = xb[rows, :]
        g = jnp.dot(xs, wg_ref[0], preferred_element_type=F32)
        u = jnp.dot(xs, wu_ref[0], preferred_element_type=F32)
        a = (g * jax.nn.sigmoid(g) * u).astype(BF16)
        o_ref[rows, :] += jnp.dot(a, wd_ref[0], preferred_element_type=F32)

    pairs = ns // 2

    def pair(s, carry):
        block(s * (2 * FFN_SUB), 2 * FFN_SUB)
        return carry

    lax.fori_loop(0, pairs, pair, 0)

    @pl.when(ns % 2 == 1)
    def _():
        block(pairs * (2 * FFN_SUB), FFN_SUB)


def _moe_ffn(h2, tile_e, tile_ns, src, w_gate, w_up, w_down):
    tf = FFN_TF
    nf = D_FF // tf

    def fcol(i, f, ns):
        return jnp.where(ns[i] > 0, f, nf - 1)

    return pl.pallas_call(
        _moe_ffn_kernel,
        grid_spec=pltpu.PrefetchScalarGridSpec(
            num_scalar_prefetch=3,
            grid=(MOE_TILES, nf),
            in_specs=[pl.BlockSpec(memory_space=pl.ANY),
                      pl.BlockSpec((1, D_MODEL, tf), lambda i, f, te, ns, sr: (te[i], 0, fcol(i, f, ns))),
                      pl.BlockSpec((1, D_MODEL, tf), lambda i, f, te, ns, sr: (te[i], 0, fcol(i, f, ns))),
                      pl.BlockSpec((1, tf, D_MODEL), lambda i, f, te, ns, sr: (te[i], fcol(i, f, ns), 0))],
            out_specs=pl.BlockSpec((FFN_TM, D_MODEL), lambda i, f, te, ns, sr: (i, 0)),
            scratch_shapes=[pltpu.VMEM((2, FFN_TM, D_MODEL), F32),
                            pltpu.VMEM((FFN_TM, D_MODEL), BF16),
                            pltpu.SemaphoreType.DMA((2,))]),
        out_shape=jax.ShapeDtypeStruct((MOE_TILES * FFN_TM, D_MODEL), F32),
        compiler_params=_cparams(("arbitrary", "arbitrary")),
        name="moe_ffn",
    )(tile_e, tile_ns, src, h2, w_gate, w_up, w_down)


RES_TM = 256


def _res_kernel(x_ref, y_ref, mod_ref, g_ref, oc_ref, ol_ref, *, final):
    x2 = x_ref[...] + mod_ref[0, 5:6, :] * y_ref[...]
    _store_pair(RES_TM, _rms(x2, g_ref[...]) if final else x2, oc_ref, ol_ref)


def _moe_res_kernel(slot_ref, x_ref, rt_ref, mod_ref, g_ref, ys_hbm, oc_ref, ol_ref, ybuf, sems, *, final):
    i = pl.program_id(0)
    nt = pl.num_programs(0)

    def row_copy(src_row, buf, k, r):
        return pltpu.make_async_copy(ys_hbm.at[pl.ds(src_row, 1)], ybuf.at[buf, k, pl.ds(r, 1)], sems.at[buf])

    def issue(tile, buf):
        def body(r8, carry):
            for u in range(ISSUE_UNROLL):
                r = r8 * ISSUE_UNROLL + u
                for k in range(2):
                    row_copy(slot_ref[k * T_ALL + tile * RES_TM + r], buf, k, r).start()
            return carry

        lax.fori_loop(0, RES_TM // ISSUE_UNROLL, body, 0)

    buf = i % 2

    @pl.when(i == 0)
    def _():
        issue(0, 0)

    for k in range(2):
        pltpu.make_async_copy(ys_hbm.at[pl.ds(0, RES_TM)], ybuf.at[buf, k], sems.at[buf]).wait()

    @pl.when(i + 1 < nt)
    def _():
        issue(i + 1, 1 - buf)

    rt = rt_ref[...]
    ffn = rt[:, 2:3] * ybuf[buf, 0] + rt[:, 3:4] * ybuf[buf, 1]
    x2 = x_ref[...] + mod_ref[0, 5:6, :] * ffn
    _store_pair(RES_TM, _rms(x2, g_ref[...]) if final else x2, oc_ref, ol_ref)


def _residual(x1, y, mod_l, final_g, *, route=None, slot=None, final):
    tm = RES_TM
    nt = T_ALL // tm
    out_shape = [jax.ShapeDtypeStruct((T_CTX, D_MODEL), F32), jax.ShapeDtypeStruct((T_LAT, D_MODEL), F32)]
    out_specs = _pair_specs(tm, (tm, D_MODEL))
    if route is None:
        row = lambda i: (i, 0)
        return pl.pallas_call(
            functools.partial(_res_kernel, final=final),
            grid=(nt,),
            in_specs=[pl.BlockSpec((tm, D_MODEL), row), pl.BlockSpec((tm, D_MODEL), row),
                      pl.BlockSpec((1, MOD_CHUNKS, D_MODEL), lambda i: (_mod_group(i, tm), 0, 0)),
                      pl.BlockSpec((1, D_MODEL), lambda i: (0, 0))],
            out_specs=out_specs,
            out_shape=out_shape,
            compiler_params=_cparams(("arbitrary",)),
            name="ffn_residual",
        )(x1, y, mod_l, final_g.reshape(1, D_MODEL))
    row = lambda i, sl: (i, 0)
    return pl.pallas_call(
        functools.partial(_moe_res_kernel, final=final),
        grid_spec=pltpu.PrefetchScalarGridSpec(
            num_scalar_prefetch=1,
            grid=(nt,),
            in_specs=[pl.BlockSpec((tm, D_MODEL), row), pl.BlockSpec((tm, LANE), row),
                      pl.BlockSpec((1, MOD_CHUNKS, D_MODEL), lambda i, sl: (_mod_group(i, tm), 0, 0)),
                      pl.BlockSpec((1, D_MODEL), lambda i, sl: (0, 0)),
                      pl.BlockSpec(memory_space=pl.ANY)],
            out_specs=out_specs,
            scratch_shapes=[pltpu.VMEM((2, 2, tm, D_MODEL), F32), pltpu.SemaphoreType.DMA((2,))]),
        out_shape=out_shape,
        compiler_params=_cparams(("arbitrary",)),
        name="moe_residual",
    )(slot, x1, route, mod_l, final_g.reshape(1, D_MODEL), y)


def _route_plan(route):
    e_flat = jnp.concatenate([route[:, 0], route[:, 1]]).astype(jnp.int32)
    onehot = (e_flat[:, None] == jnp.arange(N_EXPERTS, dtype=jnp.int32)[None, :]).astype(jnp.int32)
    ranks = jnp.cumsum(onehot, axis=0) - onehot
    rank = jnp.sum(ranks * onehot, axis=1)
    counts = jnp.sum(onehot, axis=0)
    tiles = (counts + FFN_TM - 1) // FFN_TM
    tile_start = jnp.cumsum(tiles) - tiles
    slot = tile_start[e_flat] * FFN_TM + rank
    n_slots = MOE_TILES * FFN_TM
    tok = jnp.concatenate([jnp.arange(T_ALL, dtype=jnp.int32)] * 2)
    src = jnp.zeros((n_slots,), jnp.int32).at[slot].set(tok, unique_indices=True)
    tile_ids = jnp.arange(MOE_TILES, dtype=jnp.int32)
    ends = jnp.cumsum(tiles)
    tile_e = jnp.sum((tile_ids[:, None] >= ends[None, :]).astype(jnp.int32), axis=1)
    used = tile_e < N_EXPERTS
    last_e = jnp.max(jnp.where(counts > 0, jnp.arange(N_EXPERTS, dtype=jnp.int32), 0))
    tile_e = jnp.where(used, tile_e, last_e)
    rows_in_tile = jnp.clip(counts[tile_e] - (tile_ids - tile_start[tile_e]) * FFN_TM, 0, FFN_TM)
    tile_ns = jnp.where(used, (rows_in_tile + FFN_SUB - 1) // FFN_SUB, 0).astype(jnp.int32)
    return src, slot, tile_e.astype(jnp.int32), tile_ns


def _rope_tables():
    length = DEC_SEQ
    r = jnp.repeat(jnp.arange(length // GRID_W, dtype=F32), GRID_W)
    col = (jnp.arange(length) % GRID_W).astype(F32)
    half = HEAD_DIM // 2
    inv = ROPE_THETA ** (-jnp.arange(0, half, 2, dtype=F32) / half)
    ar, ac = r[:, None] * inv, col[:, None] * inv
    cos = jnp.concatenate([jnp.cos(ar), jnp.cos(ar), jnp.cos(ac), jnp.cos(ac)], axis=-1)
    sin = jnp.concatenate([-jnp.sin(ar), jnp.sin(ar), -jnp.sin(ac), jnp.sin(ac)], axis=-1)
    return cos, sin


def kernel(x_prompt, x_sample, c, c_ctx, cache_attn_k, cache_attn_v, state_s5_re, state_s5_im, state_mlstm_c, state_mlstm_n, state_mlstm_m, norm1_g, norm2_g, w_mod, b_mod, w_in, w_out, q_norm_g, k_norm_g, s5_a_re, s5_a_im, s5_log_dt, s5_b_re, s5_b_im, s5_c_re, s5_c_im, s5_d, s5_w_glu, s5_b_glu, mlstm_i_bias, mlstm_f_bias, mlstm_norm_g, gmlp_norm_g, gmlp_w_s, gmlp_b_s, ffn_w_gate, ffn_w_up, ffn_w_down, moe_router, moe_router_bias, moe_w_gate, moe_w_up, moe_w_down, final_norm_g):
    x = (x_prompt.reshape(T_CTX, D_MODEL), x_sample.reshape(T_LAT, D_MODEL))
    cond = jnp.concatenate([c_ctx[None, :], c], axis=0)
    mod = _modulation(cond, w_mod, b_mod)
    cos, sin = _rope_tables()
    nh = MLSTM_HEADS
    zeros_s5 = jnp.zeros((BATCH, 2, S5_SG, 1, S5_SGW), F32)
    zeros_c = jnp.zeros((BATCH, 2, nh, MLSTM_DIM, MLSTM_DIM), F32)
    zeros_n = jnp.zeros((BATCH, 2, nh, 1, MLSTM_DIM), F32)
    zeros_m = jnp.zeros((BATCH, 2, nh, 1, 1), F32)

    ctx_states = []
    for l in range(DEPTH):
        use_moe = l % 2 == 1
        j = l // 2
        z, x = _in_proj(x, mod[l], norm1_g[l], w_in[l])

        att_c, k_new, v_new = _attention(z, q_norm_g[l], k_norm_g[l], ctx=True)
        (att_l,) = _attention(z, q_norm_g[l], k_norm_g[l], ctx=False,
                              kpast=cache_attn_k[:, l], vpast=cache_attn_v[:, l], cos=cos, sin=sin)

        prep = _s5_prep(s5_a_re[l], s5_a_im[l], s5_log_dt[l], s5_b_re[l], s5_b_im[l], s5_c_re[l], s5_c_im[l])
        y_c, hf_re, hf_im = _s5_scan(z, prep, zeros_s5, zeros_s5, ctx=True)
        st = lambda s: s[:, l].reshape(DEC_BATCH, 2, S5_SG, 1, S5_SGW)
        y_l, _, _ = _s5_scan(z, prep, st(state_s5_re), st(state_s5_im), ctx=False)
        s5_args = ((y_c, y_l), s5_d[l], s5_w_glu[l], s5_b_glu[l])

        mg = z[:, Z_MG:Z_MG + 16].reshape(T_ALL, 2, 2, nh)
        gcol = mg.transpose(3, 0, 1, 2).reshape(nh, T_ALL, 4)
        grow = mg.reshape(T_ALL // CHUNK, CHUNK, 4, nh).transpose(3, 0, 2, 1)
        bias = jnp.stack([mlstm_i_bias[l], mlstm_f_bias[l]], axis=1)
        bias_c = bias.transpose(2, 0, 1).reshape(nh, 1, 4)
        bias_r = bias_c.reshape(nh, 4, 1)
        ml_c_out, c_new, n_new, m_new = _mlstm(z, gcol, grow, bias_c, bias_r, zeros_c, zeros_n, zeros_m,
                                               mlstm_norm_g[l], ctx=True)
        ml_l_out, _, _, _ = _mlstm(z, gcol, grow, bias_c, bias_r, state_mlstm_c[:, l],
                                   state_mlstm_n[:, l].reshape(DEC_BATCH, 2, nh, 1, MLSTM_DIM),
                                   state_mlstm_m[:, l].reshape(DEC_BATCH, 2, nh, 1, 1),
                                   mlstm_norm_g[l], ctx=False)

        gm_args = (gmlp_norm_g[l], gmlp_w_s[l], gmlp_b_s[l])

        ctx_states.append((k_new, v_new,
                           hf_re.reshape(BATCH, 2, S5_GROUPS, S5_STATE), hf_im.reshape(BATCH, 2, S5_GROUPS, S5_STATE),
                           c_new, n_new.reshape(BATCH, 2, nh, MLSTM_DIM), m_new.reshape(BATCH, 2, nh)))

        w_o = w_out[l].astype(BF16)
        final = l == DEPTH - 1
        mixed = (z, (att_c, att_l), s5_args, (ml_c_out, ml_l_out), gm_args)
        if use_moe:
            x1, h2, route = _out_proj(*mixed, x, mod[l], norm2_g[l], w_o,
                                      router=(moe_router[j], moe_router_bias[j]))
            src, slot, tile_e, tile_ns = _route_plan(route)
            ys = _moe_ffn(h2, tile_e, tile_ns, src, moe_w_gate[j].astype(BF16), moe_w_up[j].astype(BF16),
                          moe_w_down[j].astype(BF16))
            x = _residual(x1, ys, mod[l], final_norm_g, route=route, slot=slot, final=final)
        else:
            x1, h2 = _out_proj(*mixed, x, mod[l], norm2_g[l], w_o)
            y = _ffn(h2, ffn_w_gate[j].astype(BF16), ffn_w_up[j].astype(BF16), ffn_w_down[j].astype(BF16))
            if final:
                x = _residual(x1, y, mod[l], final_norm_g, final=True)
            else:
                x = (x1, y, mod[l])

    y_prompt = x[0].reshape(BATCH, SEQ, D_MODEL)
    y_sample = x[1].reshape(DEC_BATCH, DEC_SEQ, D_MODEL)
    stack = lambda i: jnp.stack([s[i] for s in ctx_states], axis=1)
    return (y_prompt, y_sample, stack(0), stack(1), stack(2), stack(3), stack(4), stack(5), stack(6))
```

```python
import functools

import jax
import jax.numpy as jnp
from jax import lax
from jax.experimental import pallas as pl
from jax.experimental.pallas import tpu as pltpu

F32 = jnp.float32
BF16 = jnp.bfloat16

D_MODEL = 2048
BATCH = 16
SEQ = 256
DEPTH = 2
DEC_BATCH = 2
DEC_SEQ = 2048
PAST_LEN = 256
GRID_W = 64
MIX_W = 512
ATT_HEADS = 4
ATT_KV_HEADS = 2
HEAD_DIM = 128
ROPE_THETA = 10000.0
S5_CH = 16
S5_GROUPS = 32
S5_STATE = 64
MLSTM_HEADS = 4
MLSTM_DIM = 128
MLSTM_HG = 1
CHUNK = 128
GMLP_GROUPS = 4
D_FF = 7168
N_EXPERTS = 8
MOD_CHUNKS = 6
EPS = 1e-6
LOG2_E = 1.4426950408889634

T_CTX = BATCH * SEQ
T_LAT = DEC_BATCH * DEC_SEQ
T_ALL = T_CTX + T_LAT
N_GROUPS_MOD = 1 + DEC_BATCH

Z_AQ, Z_AK, Z_AV, Z_SX = 0, 512, 768, 1024
Z_MQ, Z_MK, Z_MV, Z_MO = 1536, 2048, 2560, 3072
Z_GU, Z_GV, Z_MG = 3584, 4096, 4608
Z_COLS = 4736
W_GATES0, W_GATES1 = 3584, 3600
LANE = 128
SUBLANE = 8

VMEM_LIMIT = 56 * 1024 * 1024

S5_SG = 4
S5_SGW = 8 * S5_STATE
S5_LAGS = 8
S5_CTX_NSEQ = 1

FFN_TM = 1024
FFN_SUB = 256
FFN_DENSE_SUB = 512
FFN_TF = 512
FFN_DENSE_TF = 1024
MOE_TILES = 2 * T_ALL // FFN_TM + N_EXPERTS
ISSUE_UNROLL = 8


def _cparams(sem=None):
    return pltpu.CompilerParams(dimension_semantics=sem, vmem_limit_bytes=VMEM_LIMIT)


def _mod_group(i, tm):
    return jnp.maximum(i * tm // DEC_SEQ - (T_CTX // DEC_SEQ - 1), 0)


def _rms(x, g):
    return x * lax.rsqrt(jnp.mean(x * x, axis=-1, keepdims=True) + EPS) * g


MOD_TK = 256


def _mod_kernel(cb_ref, w_ref, b_ref, o_ref, silu_scr):
    k = pl.program_id(1)
    n = MOD_CHUNKS * D_MODEL

    @pl.when((pl.program_id(0) == 0) & (k == 0))
    def _():
        c = cb_ref[...]
        silu_scr[...] = c * jax.nn.sigmoid(c)

    @pl.when(k == 0)
    def _():
        o_ref[0] = jnp.broadcast_to(b_ref[0], (SUBLANE, n))

    rows = pl.ds(pl.multiple_of(k * MOD_TK, MOD_TK), MOD_TK)
    for r in range(N_GROUPS_MOD):
        s = silu_scr[r, rows, :]
        for j in range(n // LANE):
            cols = slice(j * LANE, (j + 1) * LANE)
            o_ref[0, r:r + 1, cols] += jnp.sum(w_ref[0, :, cols] * s, axis=0, keepdims=True)


def _modulation(cond, w_mod, b_mod):
    cb = jnp.broadcast_to(cond[:, :, None], (N_GROUPS_MOD, D_MODEL, LANE))
    n = MOD_CHUNKS * D_MODEL
    out = pl.pallas_call(
        _mod_kernel,
        grid=(DEPTH, D_MODEL // MOD_TK),
        in_specs=[pl.BlockSpec((N_GROUPS_MOD, D_MODEL, LANE), lambda l, k: (0, 0, 0)),
                  pl.BlockSpec((1, MOD_TK, n), lambda l, k: (l, k, 0)),
                  pl.BlockSpec((1, 1, n), lambda l, k: (l, 0, 0))],
        out_specs=pl.BlockSpec((1, SUBLANE, n), lambda l, k: (l, 0, 0)),
        out_shape=jax.ShapeDtypeStruct((DEPTH, SUBLANE, n), F32),
        scratch_shapes=[pltpu.VMEM((N_GROUPS_MOD, D_MODEL, LANE), F32)],
        compiler_params=_cparams(("arbitrary", "arbitrary")),
        name="adaln_mod",
    )(cb, w_mod, b_mod.reshape(DEPTH, 1, n))
    return out[:, :N_GROUPS_MOD].reshape(DEPTH, N_GROUPS_MOD, MOD_CHUNKS, D_MODEL)


IN_TM = 256
IN_CHUNK = 512


def _pair_specs(tm, block, rows_dim=0):
    nc = T_CTX // tm

    def index(row):
        return tuple(row if d == rows_dim else 0 for d in range(len(block)))

    return [pl.BlockSpec(block, lambda i, *_: index(jnp.minimum(i, nc - 1))),
            pl.BlockSpec(block, lambda i, *_: index(jnp.maximum(i - nc, 0)))]


def _pick(tm, ctx_ref, lat_ref):
    return jnp.where(pl.program_id(0) < T_CTX // tm, ctx_ref[...], lat_ref[...])


def _store_pair(tm, val, oc_ref, ol_ref):
    i = pl.program_id(0)

    @pl.when(i < T_CTX // tm)
    def _():
        oc_ref[...] = val

    @pl.when(i >= T_CTX // tm)
    def _():
        ol_ref[...] = val


def _in_kernel(*refs, pending):
    if pending:
        x1_ref, y_ref, modp_ref, mod_ref, g_ref, wa_ref, wb_ref, wc_ref, z_ref, oc_ref, ol_ref = refs
        x = x1_ref[...] + modp_ref[0, 5:6, :] * y_ref[...]
        _store_pair(IN_TM, x, oc_ref, ol_ref)
    else:
        xc_ref, xl_ref, mod_ref, g_ref, wa_ref, wb_ref, wc_ref, z_ref = refs
        x = _pick(IN_TM, xc_ref, xl_ref)
    shift = mod_ref[0, 0:1, :]
    scale = mod_ref[0, 1:2, :]
    h = (_rms(x, g_ref[...]) * (1.0 + scale) + shift).astype(BF16)
    for w_ref, z0 in ((wa_ref, 0), (wb_ref, Z_GU), (wc_ref, Z_MG)):
        width = w_ref.shape[1]
        for c0 in range(0, width, IN_CHUNK):
            cw = min(IN_CHUNK, width - c0)
            z_ref[:, z0 + c0:z0 + c0 + cw] = jnp.dot(h, w_ref[:, c0:c0 + cw], preferred_element_type=F32)


def _in_proj(x, mod_l, g, w_in_l):
    tm = IN_TM
    w_segs = [w_in_l[:, :W_GATES0].astype(BF16), w_in_l[:, W_GATES1:].astype(BF16),
              jnp.pad(w_in_l[:, W_GATES0:W_GATES1], ((0, 0), (0, Z_COLS - Z_MG - (W_GATES1 - W_GATES0)))).astype(BF16)]
    w_specs = [pl.BlockSpec(w.shape, lambda i: (0, 0), pipeline_mode=pl.Buffered(1)) for w in w_segs]
    pending = len(x) == 3
    row = lambda i: (i, 0)
    mod_spec = pl.BlockSpec((1, MOD_CHUNKS, D_MODEL), lambda i: (_mod_group(i, tm), 0, 0))
    if pending:
        x_specs = [pl.BlockSpec((tm, D_MODEL), row), pl.BlockSpec((tm, D_MODEL), row), mod_spec]
    else:
        x_specs = _pair_specs(tm, (tm, D_MODEL))
    out_specs = [pl.BlockSpec((tm, Z_COLS), row)]
    out_shape = [jax.ShapeDtypeStruct((T_ALL, Z_COLS), F32)]
    if pending:
        out_specs += _pair_specs(tm, (tm, D_MODEL))
        out_shape += [jax.ShapeDtypeStruct((T_CTX, D_MODEL), F32), jax.ShapeDtypeStruct((T_LAT, D_MODEL), F32)]
    outs = pl.pallas_call(
        functools.partial(_in_kernel, pending=pending),
        grid=(T_ALL // tm,),
        in_specs=x_specs + [
                  mod_spec,
                  pl.BlockSpec((1, D_MODEL), lambda i: (0, 0))] + w_specs,
        out_specs=out_specs,
        out_shape=out_shape,
        compiler_params=_cparams(("arbitrary",)),
        name="in_proj_res" if pending else "in_proj",
    )(*x, mod_l, g.reshape(1, D_MODEL), *w_segs)
    return (outs[0], (outs[1], outs[2])) if pending else (outs[0], x)


def _rope(t, c, s):
    lane = lax.broadcasted_iota(jnp.int32, t.shape, 1)
    first = (lane % (HEAD_DIM // 2)) < (HEAD_DIM // 4)
    swapped = jnp.where(first, pltpu.roll(t, HEAD_DIM - HEAD_DIM // 4, 1), pltpu.roll(t, HEAD_DIM // 4, 1))
    return t * c + swapped * s


def _attn_kernel(*refs, seq, past, rope):
    if rope:
        (aq_ref, ak_ref, av_ref, kp_ref, vp_ref, cos_ref, sin_ref, qg_ref, kg_ref,
         att_ref, kb_scr, vb_scr) = refs
    else:
        aq_ref, ak_ref, av_ref, qg_ref, kg_ref, att_ref, knew_ref, vnew_ref, kb_scr, vb_scr = refs
    kn = _rms(ak_ref[...], kg_ref[...])
    v = av_ref[...]
    if rope:
        kn = _rope(kn, cos_ref[...], sin_ref[...])
        kb_scr[seq:seq + past, :] = kp_ref[0, 0].astype(BF16)
        vb_scr[seq:seq + past, :] = vp_ref[0, 0].astype(BF16)
    else:
        knew_ref[0, 0] = kn
        vnew_ref[0, 0] = v
    kb_scr[0:seq, :] = kn.astype(BF16)
    vb_scr[0:seq, :] = v.astype(BF16)
    grp = ATT_HEADS // ATT_KV_HEADS

    def q_block(qb, carry):
        rows = pl.ds(pl.multiple_of(qb * CHUNK, CHUNK), CHUNK)
        qs = []
        for g in range(grp):
            q = _rms(aq_ref[rows, g * HEAD_DIM:(g + 1) * HEAD_DIM], qg_ref[...])
            if rope:
                q = _rope(q, cos_ref[rows, :], sin_ref[rows, :])
            qs.append(q * (HEAD_DIM ** -0.5 * LOG2_E))
        q2 = jnp.concatenate(qs, axis=0).astype(BF16)
        s = lax.dot_general(q2, kb_scr[...], (((1,), (1,)), ((), ())), preferred_element_type=F32)
        m = jnp.max(s, axis=-1, keepdims=True)
        p = jnp.exp2(s - m)
        den = jnp.sum(p, axis=-1, keepdims=True)
        o = jnp.dot(p.astype(BF16), vb_scr[...], preferred_element_type=F32) / den
        for g in range(grp):
            att_ref[rows, g * HEAD_DIM:(g + 1) * HEAD_DIM] = o[g * CHUNK:(g + 1) * CHUNK].astype(BF16)
        return carry

    lax.fori_loop(0, seq // CHUNK, q_block, 0)


def _attention(z, qg, kg, *, ctx, kpast=None, vpast=None, cos=None, sin=None):
    seq = SEQ if ctx else DEC_SEQ
    nb = BATCH if ctx else DEC_BATCH
    row0 = 0 if ctx else T_CTX // DEC_SEQ
    past = 0 if ctx else PAST_LEN
    qw = HEAD_DIM * (ATT_HEADS // ATT_KV_HEADS)
    in_specs = [pl.BlockSpec((seq, qw), lambda b, h: (row0 + b, Z_AQ // qw + h)),
                pl.BlockSpec((seq, HEAD_DIM), lambda b, h: (row0 + b, Z_AK // HEAD_DIM + h)),
                pl.BlockSpec((seq, HEAD_DIM), lambda b, h: (row0 + b, Z_AV // HEAD_DIM + h))]
    args = [z, z, z]
    if not ctx:
        in_specs += [pl.BlockSpec((1, 1, past, HEAD_DIM), lambda b, h: (b, h, 0, 0)),
                     pl.BlockSpec((1, 1, past, HEAD_DIM), lambda b, h: (b, h, 0, 0)),
                     pl.BlockSpec((seq, HEAD_DIM), lambda b, h: (0, 0)),
                     pl.BlockSpec((seq, HEAD_DIM), lambda b, h: (0, 0))]
        args += [kpast, vpast, cos, sin]
    in_specs += [pl.BlockSpec((1, HEAD_DIM), lambda b, h: (0, 0)),
                 pl.BlockSpec((1, HEAD_DIM), lambda b, h: (0, 0))]
    args += [qg.reshape(1, HEAD_DIM), kg.reshape(1, HEAD_DIM)]
    out_specs = [pl.BlockSpec((seq, qw), lambda b, h: (b, h))]
    out_shape = [jax.ShapeDtypeStruct((nb * seq, MIX_W), BF16)]
    if ctx:
        out_specs += [pl.BlockSpec((1, 1, seq, HEAD_DIM), lambda b, h: (b, h, 0, 0))] * 2
        out_shape += [jax.ShapeDtypeStruct((nb, ATT_KV_HEADS, seq, HEAD_DIM), F32)] * 2
    return pl.pallas_call(
        functools.partial(_attn_kernel, seq=seq, past=past, rope=not ctx),
        grid=(nb, ATT_KV_HEADS),
        in_specs=in_specs, out_specs=out_specs, out_shape=out_shape,
        scratch_shapes=[pltpu.VMEM((seq + past, HEAD_DIM), BF16),
                        pltpu.VMEM((seq + past, HEAD_DIM), BF16)],
        compiler_params=_cparams(("arbitrary", "arbitrary")),
        name="attn_ctx" if ctx else "attn_lat",
    )(*args)


def _s5_prep_kernel(are_ref, aim_ref, ldt_ref, bre_ref, bim_ref, pre_ref, pim_ref, wre_ref, wim_ref):
    a_re = are_ref[...]
    a_im = aim_ref[...]
    dt = jnp.exp(ldt_ref[...])
    pows = []
    for tau in range(S5_LAGS + 1):
        mag = jnp.exp((tau * dt) * a_re)
        ang = (tau * dt) * a_im
        pr, pi = mag * jnp.cos(ang), mag * jnp.sin(ang)
        pre_ref[tau] = pr
        pim_ref[tau] = pi
        pows.append((pr, pi))
    nr, ni = pows[1][0] - 1.0, pows[1][1]
    den = a_re * a_re + a_im * a_im
    cr = (nr * a_re + ni * a_im) / den
    ci = (ni * a_re - nr * a_im) / den
    for d in range(2):
        b_r, b_i = bre_ref[d], bim_ref[d]
        bb_r = cr[d:d + 1] * b_r - ci[d:d + 1] * b_i
        bb_i = cr[d:d + 1] * b_i + ci[d:d + 1] * b_r
        for tau in range(S5_LAGS):
            pr, pi = pows[tau][0][d:d + 1], pows[tau][1][d:d + 1]
            wre_ref[d, tau] = pr * bb_r - pi * bb_i
            wim_ref[d, tau] = pr * bb_i + pi * bb_r


def _s5_prep(a_re, a_im, log_dt, b_re, b_im, c_re, c_im):
    gp = S5_GROUPS * S5_STATE
    ldt = jnp.broadcast_to(log_dt[:, :, None], (2, S5_GROUPS, S5_STATE)).reshape(2, gp)
    bt = lambda b: b.transpose(0, 3, 1, 2).reshape(2, S5_CH, gp)
    pre, pim, wre, wim = pl.pallas_call(
        _s5_prep_kernel,
        out_shape=[jax.ShapeDtypeStruct((S5_LAGS + 1, 2, gp), F32)] * 2
        + [jax.ShapeDtypeStruct((2, S5_LAGS, S5_CH, gp), F32)] * 2,
        name="s5_prep",
    )(a_re.reshape(2, gp), a_im.reshape(2, gp), ldt, bt(b_re), bt(b_im))
    eye = jnp.eye(8, dtype=F32)

    hw = S5_SGW // 2
    half_mask = (jnp.arange(hw)[None, :] // S5_STATE == jnp.arange(4)[:, None]).astype(F32)

    def w_layout(w):
        w = w.reshape(2, S5_LAGS // 2, 2, S5_CH, S5_SG, 2, hw).transpose(0, 4, 5, 1, 2, 3, 6)
        w = w[:, :, :, :, :, None, :, :] * half_mask[:, None, :]
        return w.reshape(2, S5_SG, 2, S5_LAGS * LANE // 2, hw)

    w_in = jnp.concatenate([w_layout(wre), w_layout(wim)], axis=-1).astype(BF16)

    def c_layout(c):
        c = c.reshape(2, S5_SG, 8, S5_CH, S5_STATE).transpose(0, 1, 2, 4, 3)
        c = c[:, :, :, :, None, :] * eye[None, None, :, None, :, None]
        return c.reshape(2, S5_SG, S5_SGW, LANE)

    w_out = jnp.concatenate([c_layout(c_re), -c_layout(c_im)], axis=2).astype(BF16)

    def p_layout(p):
        p = p[1:].reshape(S5_LAGS, 2, S5_SG, S5_SGW).transpose(1, 2, 0, 3)
        return jnp.stack([p[0], p[1, :, ::-1]], axis=0)

    def a8_layout(p):
        p = p[S5_LAGS].reshape(2, S5_SG, 1, S5_SGW)
        return jnp.broadcast_to(p, (2, S5_SG, SUBLANE, S5_SGW))

    return w_in, w_out, a8_layout(pre), a8_layout(pim), p_layout(pre), p_layout(pim)


def _s5_kernel(u_ref, w_ref, c_ref, a8r_ref, a8i_ref, pwr_ref, pwi_ref, h0r_ref, h0i_ref,
               y_ref, hfr_ref, hfi_ref, upad, wbr, wbi, *, seq, tc, nseq):
    d = pl.program_id(0)
    zeros = jnp.zeros((SUBLANE, LANE), F32)
    for q in range(nseq):
        upad[q, 0:SUBLANE, :] = zeros
        upad[q, SUBLANE:seq + SUBLANE, :] = u_ref[q * seq:(q + 1) * seq, :]
        upad[q, seq + SUBLANE:seq + 2 * SUBLANE, :] = zeros
    nch = seq // tc
    nt = tc // SUBLANE
    a8r = a8r_ref[0, 0]
    a8i = a8i_ref[0, 0]

    def run(fwd):
        pwr = pwr_ref[0, 0]
        pwi = pwi_ref[0, 0]
        state = tuple((jnp.zeros((SUBLANE, S5_SGW), F32), jnp.zeros((SUBLANE, S5_SGW), F32))
                      for _ in range(nseq))
        for ci in range(nch):
            c = ci if fwd else nch - 1 - ci
            per_seq = []
            for q in range(nseq):
                if fwd:
                    win = upad[q, c * tc:c * tc + tc + SUBLANE, :]
                    per_seq.append([pltpu.roll(win, tau, 0)[SUBLANE:SUBLANE + tc] if tau
                                    else win[SUBLANE:SUBLANE + tc] for tau in range(S5_LAGS)])
                else:
                    win = upad[q, c * tc + SUBLANE:c * tc + tc + 2 * SUBLANE, :]
                    per_seq.append([pltpu.roll(win, tc + SUBLANE - tau, 0)[0:tc] if tau else win[0:tc]
                                    for tau in range(S5_LAGS)])
            lags = [jnp.concatenate([per_seq[q][tau] for q in range(nseq)], axis=0) if nseq > 1
                    else per_seq[0][tau] for tau in range(S5_LAGS)]
            low = lax.broadcasted_iota(jnp.int32, (nseq * tc, LANE), 1) < LANE // 2
            slabs = ([], [])
            for k in range(S5_LAGS // 2):
                a, b = lags[2 * k], lags[2 * k + 1]
                slabs[0].append(jnp.where(low, a, pltpu.roll(b, LANE // 2, 1)).astype(BF16))
                slabs[1].append(jnp.where(low, pltpu.roll(a, LANE // 2, 1), b).astype(BF16))
            hw = S5_SGW // 2
            for half in range(2):
                w = jnp.dot(jnp.concatenate(slabs[half], axis=1), w_ref[0, 0, half],
                            preferred_element_type=F32)
                wbr[:, half * hw:(half + 1) * hw] = w[:, :hw]
                wbi[:, half * hw:(half + 1) * hw] = w[:, hw:]
            if ci == 0:
                for q in range(nseq):
                    h0r = h0r_ref[q, 0, 0]
                    h0i = h0i_ref[q, 0, 0]
                    r0 = q * tc + (0 if fwd else tc - SUBLANE)
                    wbr[r0:r0 + SUBLANE, :] = wbr[r0:r0 + SUBLANE, :] + (pwr * h0r - pwi * h0i)
                    wbi[r0:r0 + SUBLANE, :] = wbi[r0:r0 + SUBLANE, :] + (pwr * h0i + pwi * h0r)

            def step(i, carry):
                t = i if fwd else nt - 1 - i
                new = []
                for q in range(nseq):
                    cr, ci_ = carry[q]
                    rows = pl.ds(pl.multiple_of(q * tc + t * SUBLANE, SUBLANE), SUBLANE)
                    nr = a8r * cr - a8i * ci_ + wbr[rows, :]
                    ni = a8r * ci_ + a8i * cr + wbi[rows, :]
                    wbr[rows, :] = nr
                    wbi[rows, :] = ni
                    new.append((nr, ni))
                return tuple(new)

            state = lax.fori_loop(0, nt, step, state)
            hcat = jnp.concatenate([wbr[...].astype(BF16), wbi[...].astype(BF16)], axis=1)
            y = jnp.dot(hcat, c_ref[0, 0], preferred_element_type=F32)
            for q in range(nseq):
                y_ref[0, q * seq + c * tc:q * seq + (c + 1) * tc, :] = y[q * tc:(q + 1) * tc]
        last = SUBLANE - 1 if fwd else 0
        for q in range(nseq):
            hfr_ref[q, 0, 0] = state[q][0][last:last + 1]
            hfi_ref[q, 0, 0] = state[q][1][last:last + 1]

    @pl.when(d == 0)
    def _():
        run(True)

    @pl.when(d == 1)
    def _():
        run(False)


def _s5_scan(z, prep, h0r, h0i, *, ctx):
    w_in, w_out, a8r, a8i, pwr, pwi = prep
    seq = SEQ if ctx else DEC_SEQ
    nb = BATCH if ctx else DEC_BATCH
    row0 = 0 if ctx else T_CTX // DEC_SEQ
    tc = min(seq, 512)
    nseq = S5_CTX_NSEQ if ctx else 1
    par = lambda d, s, b: (d, s, 0, 0)
    st = lambda d, s, b: (b, d, s, 0, 0)
    return pl.pallas_call(
        functools.partial(_s5_kernel, seq=seq, tc=tc, nseq=nseq),
        grid=(2, S5_SG, nb // nseq),
        in_specs=[pl.BlockSpec((nseq * seq, LANE), lambda d, s, b: (row0 + b, Z_SX // LANE + s)),
                  pl.BlockSpec((1, 1, 2, S5_LAGS * LANE // 2, S5_SGW), lambda d, s, b: (d, s, 0, 0, 0)),
                  pl.BlockSpec((1, 1, 2 * S5_SGW, LANE), par),
                  pl.BlockSpec((1, 1, SUBLANE, S5_SGW), par),
                  pl.BlockSpec((1, 1, SUBLANE, S5_SGW), par),
                  pl.BlockSpec((1, 1, SUBLANE, S5_SGW), par),
                  pl.BlockSpec((1, 1, SUBLANE, S5_SGW), par),
                  pl.BlockSpec((nseq, 1, 1, 1, S5_SGW), st),
                  pl.BlockSpec((nseq, 1, 1, 1, S5_SGW), st)],
        out_specs=[pl.BlockSpec((1, nseq * seq, LANE), lambda d, s, b: (d, b, s)),
                   pl.BlockSpec((nseq, 1, 1, 1, S5_SGW), st),
                   pl.BlockSpec((nseq, 1, 1, 1, S5_SGW), st)],
        out_shape=[jax.ShapeDtypeStruct((2, nb * seq, MIX_W), F32),
                   jax.ShapeDtypeStruct((nb, 2, S5_SG, 1, S5_SGW), F32),
                   jax.ShapeDtypeStruct((nb, 2, S5_SG, 1, S5_SGW), F32)],
        scratch_shapes=[pltpu.VMEM((nseq, seq + 2 * SUBLANE, LANE), F32),
                        pltpu.VMEM((nseq * tc, S5_SGW), F32),
                        pltpu.VMEM((nseq * tc, S5_SGW), F32)],
        compiler_params=_cparams(("arbitrary", "arbitrary", "arbitrary")),
        name="s5_ctx" if ctx else "s5_lat",
    )(z, w_in, w_out, a8r, a8i, pwr, pwi, h0r, h0i)


def _mlstm_kernel(q_ref, k_ref, v_ref, mo_ref, gc_ref, gr_ref, bc_ref, br_ref, c0_ref, n0_ref, m0_ref,
                  ng_ref, h_ref, c_ref, n_ref, m_ref, hs_scr, *, seq, chunk0):
    nc = seq // CHUNK
    ii = lax.broadcasted_iota(jnp.int32, (CHUNK, CHUNK), 0)
    jj = lax.broadcasted_iota(jnp.int32, (CHUNK, CHUNK), 1)
    neg_inf = jnp.float32(-jnp.inf)

    def chunk_step(hh, d, cidx, carry):
        cols = slice(hh * MLSTM_DIM, (hh + 1) * MLSTM_DIM)
        fwd = d == 0
        mask = (jj <= ii) if fwd else (jj >= ii)
        mask_t = (ii <= jj) if fwd else (ii >= jj)
        last = CHUNK - 1 if fwd else 0
        c_st, n_st, m_st = carry
        rows = pl.ds(pl.multiple_of(cidx * CHUNK, CHUNK), CHUNK)
        q = q_ref[rows, cols]
        k = k_ref[rows, cols] * (MLSTM_DIM ** -0.5)
        v = v_ref[rows, cols]
        gcol = gc_ref[hh, rows, :] + bc_ref[hh]
        grow = gr_ref[hh, chunk0 + cidx] + br_ref[hh]
        li_c = gcol[:, 2 * d:2 * d + 1]
        lf_c = jax.nn.log_sigmoid(gcol[:, 2 * d + 1:2 * d + 2])
        li_r = grow[2 * d:2 * d + 1, :]
        lf_r = jax.nn.log_sigmoid(grow[2 * d + 1:2 * d + 2, :])
        b_c = jnp.sum(jnp.where(mask, lf_r, 0.0), axis=1, keepdims=True)
        b_r = jnp.sum(jnp.where(mask_t, lf_c, 0.0), axis=0, keepdims=True)
        dmat = jnp.where(mask, b_c - b_r + li_r, neg_inf)
        inter = b_c + m_st
        m_j = jnp.maximum(inter, jnp.max(dmat, axis=1, keepdims=True))
        qb = q.astype(BF16)
        vb = v.astype(BF16)
        s = lax.dot_general(qb, k.astype(BF16), (((1,), (1,)), ((), ())),
                            preferred_element_type=F32) * jnp.exp(dmat - m_j)
        w_inter = jnp.exp(inter - m_j)
        num = (jnp.dot(s.astype(BF16), vb, preferred_element_type=F32)
               + w_inter * jnp.dot(qb, c_st.astype(BF16), preferred_element_type=F32))
        den = jnp.sum(s, axis=1, keepdims=True) + w_inter * jnp.sum(q * n_st, axis=1, keepdims=True)
        hs_scr[d, rows, cols] = num / jnp.maximum(jnp.abs(den), jnp.exp(-m_j))
        m_end = m_j[last:last + 1, :]
        b_end = b_c[last:last + 1, :]
        w_c = jnp.exp(b_end - b_c + li_c - m_end)
        decay = jnp.exp(b_end + m_st - m_end)
        kw = k * w_c
        c_new = decay * c_st + lax.dot_general(kw.astype(BF16), vb, (((0,), (0,)), ((), ())),
                                               preferred_element_type=F32)
        n_new = decay * n_st + jnp.sum(kw, axis=0, keepdims=True)
        return c_new, n_new, m_end

    chains = [(hh, d) for hh in range(MLSTM_HG) for d in range(2)]

    def all_chains(ci, carry):
        return tuple(chunk_step(hh, d, ci if d == 0 else nc - 1 - ci, carry[n])
                     for n, (hh, d) in enumerate(chains))

    init = tuple((c0_ref[0, d, hh], n0_ref[0, d, hh], m0_ref[0, d, hh]) for hh, d in chains)
    final = lax.fori_loop(0, nc, all_chains, init)
    for n, (hh, d) in enumerate(chains):
        c_ref[0, d, hh], n_ref[0, d, hh], m_ref[0, d, hh] = final[n]
    for hh in range(MLSTM_HG):
        cols = slice(hh * MLSTM_DIM, (hh + 1) * MLSTM_DIM)
        h = _rms(hs_scr[0, :, cols] + hs_scr[1, :, cols], ng_ref[hh])
        h_ref[:, cols] = (h * jax.nn.sigmoid(mo_ref[:, cols])).astype(BF16)


def _mlstm(z, gcol, grow, bias_c, bias_r, c0, n0, m0, norm_g, *, ctx):
    seq = SEQ if ctx else DEC_SEQ
    nb = BATCH if ctx else DEC_BATCH
    row0 = 0 if ctx else T_CTX // DEC_SEQ
    hd = MLSTM_DIM
    nh = MLSTM_HEADS
    hg = MLSTM_HG
    hw = hg * hd
    zcol = lambda off: (lambda b, h: (row0 + b, off // hw + h))
    st5 = lambda b, h: (b, 0, h, 0, 0)
    return pl.pallas_call(
        functools.partial(_mlstm_kernel, seq=seq, chunk0=0),
        grid=(nb, nh // hg),
        in_specs=[pl.BlockSpec((seq, hw), zcol(Z_MQ)),
                  pl.BlockSpec((seq, hw), zcol(Z_MK)),
                  pl.BlockSpec((seq, hw), zcol(Z_MV)),
                  pl.BlockSpec((seq, hw), zcol(Z_MO)),
                  pl.BlockSpec((hg, seq, 4), lambda b, h: (h, row0 + b, 0)),
                  pl.BlockSpec((hg, seq // CHUNK, 4, CHUNK), lambda b, h: (h, row0 + b, 0, 0)),
                  pl.BlockSpec((hg, 1, 4), lambda b, h: (h, 0, 0)),
                  pl.BlockSpec((hg, 4, 1), lambda b, h: (h, 0, 0)),
                  pl.BlockSpec((1, 2, hg, hd, hd), st5),
                  pl.BlockSpec((1, 2, hg, 1, hd), st5),
                  pl.BlockSpec((1, 2, hg, 1, 1), st5),
                  pl.BlockSpec((hg, 1, hd), lambda b, h: (h, 0, 0))],
        out_specs=[pl.BlockSpec((seq, hw), lambda b, h: (b, h)),
                   pl.BlockSpec((1, 2, hg, hd, hd), st5),
                   pl.BlockSpec((1, 2, hg, 1, hd), st5),
                   pl.BlockSpec((1, 2, hg, 1, 1), st5)],
        out_shape=[jax.ShapeDtypeStruct((nb * seq, MIX_W), BF16),
                   jax.ShapeDtypeStruct((nb, 2, nh, hd, hd), F32),
                   jax.ShapeDtypeStruct((nb, 2, nh, 1, hd), F32),
                   jax.ShapeDtypeStruct((nb, 2, nh, 1, 1), F32)],
        scratch_shapes=[pltpu.VMEM((2, seq, hw), F32)],
        compiler_params=_cparams(("arbitrary", "arbitrary")),
        name="mlstm_ctx" if ctx else "mlstm_lat",
    )(z, z, z, z, gcol, grow, bias_c, bias_r, c0, n0, m0, norm_g.reshape(nh, 1, hd))


OUT_TM = 256


def _out_kernel(*refs, router):
    (ac_ref, al_ref, yc_ref, yl_ref, sx_ref, sd_ref, sw_ref, sb_ref, cc_ref, cl_ref,
     gu_ref, gv_ref, gn_ref, gw_ref, gb_ref, xc_ref, xl_ref, mod_ref, g_ref, w_ref), refs = refs[:20], refs[20:]
    if router:
        rw_ref, rb_ref, x1_ref, h2_ref, rt_ref = refs
    else:
        x1_ref, h2_ref = refs
    y2 = _pick(OUT_TM, yc_ref, yl_ref)
    y = jax.nn.gelu(y2[0] + y2[1] + sd_ref[...] * sx_ref[...])
    gate = jnp.dot(y.astype(BF16), sw_ref[...], preferred_element_type=F32) + sb_ref[...]
    s5_out = (y * jax.nn.sigmoid(gate)).astype(BF16)
    vn = _rms(gv_ref[...], gn_ref[...]).astype(BF16)
    gw = MIX_W // GMLP_GROUPS
    chunks = []
    for c in range(OUT_TM // CHUNK):
        r = slice(c * CHUNK, (c + 1) * CHUNK)
        groups = []
        for g in range(GMLP_GROUPS):
            cs = slice(g * gw, (g + 1) * gw)
            mixed = jnp.dot(gw_ref[g], vn[r, cs], preferred_element_type=F32) + gb_ref[:, g:g + 1]
            groups.append((gu_ref[r, cs] * mixed).astype(BF16))
        chunks.append(jnp.concatenate(groups, axis=1))
    gm_out = jnp.concatenate(chunks, axis=0)
    mix = jnp.dot(_pick(OUT_TM, ac_ref, al_ref), w_ref[0:MIX_W, :], preferred_element_type=F32)
    mix += jnp.dot(s5_out, w_ref[MIX_W:2 * MIX_W, :], preferred_element_type=F32)
    mix += jnp.dot(_pick(OUT_TM, cc_ref, cl_ref), w_ref[2 * MIX_W:3 * MIX_W, :], preferred_element_type=F32)
    mix += jnp.dot(gm_out, w_ref[3 * MIX_W:4 * MIX_W, :], preferred_element_type=F32)
    x1 = _pick(OUT_TM, xc_ref, xl_ref) + mod_ref[0, 2:3, :] * mix
    x1_ref[...] = x1
    h2 = _rms(x1, g_ref[...]) * (1.0 + mod_ref[0, 4:5, :]) + mod_ref[0, 3:4, :]
    hi = h2.astype(BF16)
    h2_ref[...] = h2 if router else hi
    if router:
        lo = (h2 - hi.astype(F32)).astype(BF16)
        both = jnp.dot(hi, rw_ref[...], preferred_element_type=F32)
        logits = (both[:, :LANE] + jnp.dot(lo, rw_ref[:, :LANE], preferred_element_type=F32)
                  + both[:, LANE:]) + rb_ref[...]
        lane = lax.broadcasted_iota(jnp.int32, logits.shape, 1)
        neg_inf = jnp.float32(-jnp.inf)
        lg = jnp.where(lane < N_EXPERTS, logits, neg_inf)
        m1 = jnp.max(lg, axis=-1, keepdims=True)
        i1 = jnp.min(jnp.where(lg == m1, lane, LANE), axis=-1, keepdims=True)
        lg2 = jnp.where(lane == i1, neg_inf, lg)
        m2 = jnp.max(lg2, axis=-1, keepdims=True)
        i2 = jnp.min(jnp.where(lg2 == m2, lane, LANE), axis=-1, keepdims=True)
        e = jnp.exp(m2 - m1)
        w1 = 1.0 / (1.0 + e)
        w2 = e / (1.0 + e)
        rt = jnp.where(lane == 0, i1.astype(F32), 0.0)
        rt = jnp.where(lane == 1, i2.astype(F32), rt)
        rt = jnp.where(lane == 2, w1, rt)
        rt = jnp.where(lane == 3, w2, rt)
        rt_ref[...] = rt


def _out_proj(z, att_pair, s5_args, ml_pair, gm_args, x_pair, mod_l, g2, w_out, router=None):
    tm = OUT_TM
    row = lambda i: (i, 0)
    fixed = lambda i: (0, 0)
    pair = _pair_specs(tm, (tm, MIX_W))
    zcols = lambda off: pl.BlockSpec((tm, MIX_W), lambda i: (i, off // MIX_W))
    vec = pl.BlockSpec((1, MIX_W), fixed)
    y_pair, d_skip, w_glu, b_glu = s5_args
    gm_norm_g, gm_w_s, gm_b_s = gm_args
    in_specs = (pair + _pair_specs(tm, (2, tm, MIX_W), rows_dim=1)
                + [zcols(Z_SX), vec, pl.BlockSpec((MIX_W, MIX_W), fixed), vec]
                + pair
                + [zcols(Z_GU), zcols(Z_GV), vec, pl.BlockSpec((GMLP_GROUPS, CHUNK, CHUNK), lambda i: (0, 0, 0)),
                   pl.BlockSpec((CHUNK, GMLP_GROUPS), fixed)]
                + _pair_specs(tm, (tm, D_MODEL))
                + [pl.BlockSpec((1, MOD_CHUNKS, D_MODEL), lambda i: (_mod_group(i, tm), 0, 0)),
                   pl.BlockSpec((1, D_MODEL), fixed),
                   pl.BlockSpec((D_MODEL, D_MODEL), fixed, pipeline_mode=pl.Buffered(1))])
    args = [*att_pair, *y_pair, z, d_skip.reshape(1, MIX_W), w_glu.astype(BF16), b_glu.reshape(1, MIX_W),
            *ml_pair, z, z, gm_norm_g.reshape(1, MIX_W), gm_w_s.astype(BF16), gm_b_s.T,
            *x_pair, mod_l, g2.reshape(1, D_MODEL), w_out]
    out_specs = [pl.BlockSpec((tm, D_MODEL), row), pl.BlockSpec((tm, D_MODEL), row)]
    out_shape = [jax.ShapeDtypeStruct((T_ALL, D_MODEL), F32),
                 jax.ShapeDtypeStruct((T_ALL, D_MODEL), BF16 if router is None else F32)]
    if router is not None:
        rw, rb = router
        rw = jnp.pad(rw, ((0, 0), (0, LANE - N_EXPERTS)))
        rh = rw.astype(BF16)
        rl = (rw - rh.astype(F32)).astype(BF16)
        in_specs += [pl.BlockSpec((D_MODEL, 2 * LANE), fixed), pl.BlockSpec((1, LANE), fixed)]
        args += [jnp.concatenate([rh, rl], axis=1), jnp.pad(rb, (0, LANE - N_EXPERTS)).reshape(1, LANE)]
        out_specs.append(pl.BlockSpec((tm, LANE), row))
        out_shape.append(jax.ShapeDtypeStruct((T_ALL, LANE), F32))
    return pl.pallas_call(
        functools.partial(_out_kernel, router=router is not None),
        grid=(T_ALL // tm,),
        in_specs=in_specs, out_specs=out_specs, out_shape=out_shape,
        compiler_params=_cparams(("arbitrary",)),
        name="out_proj_router" if router is not None else "out_proj",
    )(*args)


def _ffn_kernel(x_ref, wg_ref, wu_ref, wd_ref, o_ref):
    @pl.when(pl.program_id(1) == 0)
    def _():
        o_ref[...] = jnp.zeros_like(o_ref)

    def sub(s, carry):
        rows = pl.ds(pl.multiple_of(s * FFN_DENSE_SUB, FFN_DENSE_SUB), FFN_DENSE_SUB)
        xs = x_ref[rows, :]
        g = jnp.dot(xs, wg_ref[...], preferred_element_type=F32)
        u = jnp.dot(xs, wu_ref[...], preferred_element_type=F32)
        a = (g * jax.nn.sigmoid(g) * u).astype(BF16)
        o_ref[rows, :] += jnp.dot(a, wd_ref[...], preferred_element_type=F32)
        return carry

    lax.fori_loop(0, FFN_TM // FFN_DENSE_SUB, sub, 0)


def _ffn(x, w_gate, w_up, w_down):
    return pl.pallas_call(
        _ffn_kernel,
        grid=(T_ALL // FFN_TM, D_FF // FFN_DENSE_TF),
        in_specs=[pl.BlockSpec((FFN_TM, D_MODEL), lambda i, f: (i, 0)),
                  pl.BlockSpec((D_MODEL, FFN_DENSE_TF), lambda i, f: (0, f)),
                  pl.BlockSpec((D_MODEL, FFN_DENSE_TF), lambda i, f: (0, f)),
                  pl.BlockSpec((FFN_DENSE_TF, D_MODEL), lambda i, f: (f, 0))],
        out_specs=pl.BlockSpec((FFN_TM, D_MODEL), lambda i, f: (i, 0)),
        out_shape=jax.ShapeDtypeStruct((T_ALL, D_MODEL), F32),
        compiler_params=_cparams(("arbitrary", "arbitrary")),
        name="ffn",
    )(x, w_gate, w_up, w_down)


def _moe_ffn_kernel(te_ref, ns_ref, src_ref, x_hbm, wg_ref, wu_ref, wd_ref, o_ref, xf, xb, sems):
    i = pl.program_id(0)
    f = pl.program_id(1)
    ntiles = pl.num_programs(0)

    def row_copy(src_row, slot, r):
        return pltpu.make_async_copy(x_hbm.at[pl.ds(src_row, 1)], xf.at[slot, pl.ds(r, 1)], sems.at[slot])

    def issue(tile, slot):
        def body(r8, carry):
            for u in range(ISSUE_UNROLL):
                r = r8 * ISSUE_UNROLL + u
                row_copy(src_ref[tile * FFN_TM + r], slot, r).start(priority=u % 2)
            return carry

        lax.fori_loop(0, ns_ref[tile] * (FFN_SUB // ISSUE_UNROLL), body, 0)

    def wait_rows(tile, slot):
        def body(s, carry):
            rows = pl.ds(pl.multiple_of(s * FFN_SUB, FFN_SUB), FFN_SUB)
            pltpu.make_async_copy(x_hbm.at[pl.ds(0, FFN_SUB)], xf.at[slot, rows], sems.at[slot]).wait()
            return carry

        lax.fori_loop(0, ns_ref[tile], body, 0)

    @pl.when(f == 0)
    def _():
        slot = i % 2

        @pl.when(i == 0)
        def _():
            issue(0, 0)

        wait_rows(i, slot)

        @pl.when(i + 1 < ntiles)
        def _():
            issue(i + 1, 1 - slot)

        def cast(s, carry):
            rows = pl.ds(pl.multiple_of(s * FFN_SUB, FFN_SUB), FFN_SUB)
            xb[rows, :] = xf[slot, rows, :].astype(BF16)
            return carry

        lax.fori_loop(0, ns_ref[i], cast, 0)
        o_ref[...] = jnp.zeros_like(o_ref)

    ns = ns_ref[i]

    def block(row0, size):
        rows = pl.ds(pl.multiple_of(row0, size), size)
        xs = xb[rows, :]
        g = jnp.dot(xs, wg_ref[0], preferred_element_type=F32)
        u = jnp.dot(xs, wu_ref[0], preferred_element_type=F32)
        a = (g * jax.nn.sigmoid(g) * u).astype(BF16)
        o_ref[rows, :] += jnp.dot(a, wd_ref[0], preferred_element_type=F32)

    pairs = ns // 2

    def pair(s, carry):
        block(s * (2 * FFN_SUB), 2 * FFN_SUB)
        return carry

    lax.fori_loop(0, pairs, pair, 0)

    @pl.when(ns % 2 == 1)
    def _():
        block(pairs * (2 * FFN_SUB), FFN_SUB)


def _moe_ffn(h2, tile_e, tile_ns, src, w_gate, w_up, w_down):
    tf = FFN_TF
    nf = D_FF // tf

    def fcol(i, f, ns):
        return jnp.where(ns[i] > 0, f, nf - 1)

    return pl.pallas_call(
        _moe_ffn_kernel,
        grid_spec=pltpu.PrefetchScalarGridSpec(
            num_scalar_prefetch=3,
            grid=(MOE_TILES, nf),
            in_specs=[pl.BlockSpec(memory_space=pl.ANY),
                      pl.BlockSpec((1, D_MODEL, tf), lambda i, f, te, ns, sr: (te[i], 0, fcol(i, f, ns))),
                      pl.BlockSpec((1, D_MODEL, tf), lambda i, f, te, ns, sr: (te[i], 0, fcol(i, f, ns))),
                      pl.BlockSpec((1, tf, D_MODEL), lambda i, f, te, ns, sr: (te[i], fcol(i, f, ns), 0))],
            out_specs=pl.BlockSpec((FFN_TM, D_MODEL), lambda i, f, te, ns, sr: (i, 0)),
            scratch_shapes=[pltpu.VMEM((2, FFN_TM, D_MODEL), F32),
                            pltpu.VMEM((FFN_TM, D_MODEL), BF16),
                            pltpu.SemaphoreType.DMA((2,))]),
        out_shape=jax.ShapeDtypeStruct((MOE_TILES * FFN_TM, D_MODEL), F32),
        compiler_params=_cparams(("arbitrary", "arbitrary")),
        name="moe_ffn",
    )(tile_e, tile_ns, src, h2, w_gate, w_up, w_down)


RES_TM = 256


def _res_kernel(x_ref, y_ref, mod_ref, g_ref, oc_ref, ol_ref, *, final):
    x2 = x_ref[...] + mod_ref[0, 5:6, :] * y_ref[...]
    _store_pair(RES_TM, _rms(x2, g_ref[...]) if final else x2, oc_ref, ol_ref)


def _moe_res_kernel(slot_ref, x_ref, rt_ref, mod_ref, g_ref, ys_hbm, oc_ref, ol_ref, ybuf, sems, *, final):
    i = pl.program_id(0)
    nt = pl.num_programs(0)

    def row_copy(src_row, buf, k, r):
        return pltpu.make_async_copy(ys_hbm.at[pl.ds(src_row, 1)], ybuf.at[buf, k, pl.ds(r, 1)], sems.at[buf])

    def issue(tile, buf):
        def body(r8, carry):
            for u in range(ISSUE_UNROLL):
                r = r8 * ISSUE_UNROLL + u
                for k in range(2):
                    row_copy(slot_ref[k * T_ALL + tile * RES_TM + r], buf, k, r).start(priority=k)
            return carry

        lax.fori_loop(0, RES_TM // ISSUE_UNROLL, body, 0)

    buf = i % 2

    @pl.when(i == 0)
    def _():
        issue(0, 0)

    for k in range(2):
        pltpu.make_async_copy(ys_hbm.at[pl.ds(0, RES_TM)], ybuf.at[buf, k], sems.at[buf]).wait()

    @pl.when(i + 1 < nt)
    def _():
        issue(i + 1, 1 - buf)

    rt = rt_ref[...]
    ffn = rt[:, 2:3] * ybuf[buf, 0] + rt[:, 3:4] * ybuf[buf, 1]
    x2 = x_ref[...] + mod_ref[0, 5:6, :] * ffn
    _store_pair(RES_TM, _rms(x2, g_ref[...]) if final else x2, oc_ref, ol_ref)


def _residual(x1, y, mod_l, final_g, *, route=None, slot=None, final):
    tm = RES_TM
    nt = T_ALL // tm
    out_shape = [jax.ShapeDtypeStruct((T_CTX, D_MODEL), F32), jax.ShapeDtypeStruct((T_LAT, D_MODEL), F32)]
    out_specs = _pair_specs(tm, (tm, D_MODEL))
    if route is None:
        row = lambda i: (i, 0)
        return pl.pallas_call(
            functools.partial(_res_kernel, final=final),
            grid=(nt,),
            in_specs=[pl.BlockSpec((tm, D_MODEL), row), pl.BlockSpec((tm, D_MODEL), row),
                      pl.BlockSpec((1, MOD_CHUNKS, D_MODEL), lambda i: (_mod_group(i, tm), 0, 0)),
                      pl.BlockSpec((1, D_MODEL), lambda i: (0, 0))],
            out_specs=out_specs,
            out_shape=out_shape,
            compiler_params=_cparams(("arbitrary",)),
            name="ffn_residual",
        )(x1, y, mod_l, final_g.reshape(1, D_MODEL))
    row = lambda i, sl: (i, 0)
    return pl.pallas_call(
        functools.partial(_moe_res_kernel, final=final),
        grid_spec=pltpu.PrefetchScalarGridSpec(
            num_scalar_prefetch=1,
            grid=(nt,),
            in_specs=[pl.BlockSpec((tm, D_MODEL), row), pl.BlockSpec((tm, LANE), row),
                      pl.BlockSpec((1, MOD_CHUNKS, D_MODEL), lambda i, sl: (_mod_group(i, tm), 0, 0)),
                      pl.BlockSpec((1, D_MODEL), lambda i, sl: (0, 0)),
                      pl.BlockSpec(memory_space=pl.ANY)],
            out_specs=out_specs,
            scratch_shapes=[pltpu.VMEM((2, 2, tm, D_MODEL), F32), pltpu.SemaphoreType.DMA((2,))]),
        out_shape=out_shape,
        compiler_params=_cparams(("arbitrary",)),
        name="moe_residual",
    )(slot, x1, route, mod_l, final_g.reshape(1, D_MODEL), y)


def _route_plan(route):
    e_flat = jnp.concatenate([route[:, 0], route[:, 1]]).astype(jnp.int32)
    onehot = (e_flat[:, None] == jnp.arange(N_EXPERTS, dtype=jnp.int32)[None, :]).astype(jnp.int32)
    ranks = jnp.cumsum(onehot, axis=0) - onehot
    rank = jnp.sum(ranks * onehot, axis=1)
    counts = jnp.sum(onehot, axis=0)
    tiles = (counts + FFN_TM - 1) // FFN_TM
    tile_start = jnp.cumsum(tiles) - tiles
    slot = tile_start[e_flat] * FFN_TM + rank
    n_slots = MOE_TILES * FFN_TM
    tok = jnp.concatenate([jnp.arange(T_ALL, dtype=jnp.int32)] * 2)
    src = jnp.zeros((n_slots,), jnp.int32).at[slot].set(tok, unique_indices=True)
    tile_ids = jnp.arange(MOE_TILES, dtype=jnp.int32)
    ends = jnp.cumsum(tiles)
    tile_e = jnp.sum((tile_ids[:, None] >= ends[None, :]).astype(jnp.int32), axis=1)
    used = tile_e < N_EXPERTS
    last_e = jnp.max(jnp.where(counts > 0, jnp.arange(N_EXPERTS, dtype=jnp.int32), 0))
    tile_e = jnp.where(used, tile_e, last_e)
    rows_in_tile = jnp.clip(counts[tile_e] - (tile_ids - tile_start[tile_e]) * FFN_TM, 0, FFN_TM)
    tile_ns = jnp.where(used, (rows_in_tile + FFN_SUB - 1) // FFN_SUB, 0).astype(jnp.int32)
    return src, slot, tile_e.astype(jnp.int32), tile_ns


def _rope_tables():
    length = DEC_SEQ
    r = jnp.repeat(jnp.arange(length // GRID_W, dtype=F32), GRID_W)
    col = (jnp.arange(length) % GRID_W).astype(F32)
    half = HEAD_DIM // 2
    inv = ROPE_THETA ** (-jnp.arange(0, half, 2, dtype=F32) / half)
    ar, ac = r[:, None] * inv, col[:, None] * inv
    cos = jnp.concatenate([jnp.cos(ar), jnp.cos(ar), jnp.cos(ac), jnp.cos(ac)], axis=-1)
    sin = jnp.concatenate([-jnp.sin(ar), jnp.sin(ar), -jnp.sin(ac), jnp.sin(ac)], axis=-1)
    return cos, sin


def kernel(x_prompt, x_sample, c, c_ctx, cache_attn_k, cache_attn_v, state_s5_re, state_s5_im, state_mlstm_c, state_mlstm_n, state_mlstm_m, norm1_g, norm2_g, w_mod, b_mod, w_in, w_out, q_norm_g, k_norm_g, s5_a_re, s5_a_im, s5_log_dt, s5_b_re, s5_b_im, s5_c_re, s5_c_im, s5_d, s5_w_glu, s5_b_glu, mlstm_i_bias, mlstm_f_bias, mlstm_norm_g, gmlp_norm_g, gmlp_w_s, gmlp_b_s, ffn_w_gate, ffn_w_up, ffn_w_down, moe_router, moe_router_bias, moe_w_gate, moe_w_up, moe_w_down, final_norm_g):
    x = (x_prompt.reshape(T_CTX, D_MODEL), x_sample.reshape(T_LAT, D_MODEL))
    cond = jnp.concatenate([c_ctx[None, :], c], axis=0)
    mod = _modulation(cond, w_mod, b_mod)
    cos, sin = _rope_tables()
    nh = MLSTM_HEADS
    zeros_s5 = jnp.zeros((BATCH, 2, S5_SG, 1, S5_SGW), F32)
    zeros_c = jnp.zeros((BATCH, 2, nh, MLSTM_DIM, MLSTM_DIM), F32)
    zeros_n = jnp.zeros((BATCH, 2, nh, 1, MLSTM_DIM), F32)
    zeros_m = jnp.zeros((BATCH, 2, nh, 1, 1), F32)

    ctx_states = []
    for l in range(DEPTH):
        use_moe = l % 2 == 1
        j = l // 2
        z, x = _in_proj(x, mod[l], norm1_g[l], w_in[l])

        att_c, k_new, v_new = _attention(z, q_norm_g[l], k_norm_g[l], ctx=True)
        (att_l,) = _attention(z, q_norm_g[l], k_norm_g[l], ctx=False,
                              kpast=cache_attn_k[:, l], vpast=cache_attn_v[:, l], cos=cos, sin=sin)

        prep = _s5_prep(s5_a_re[l], s5_a_im[l], s5_log_dt[l], s5_b_re[l], s5_b_im[l], s5_c_re[l], s5_c_im[l])
        y_c, hf_re, hf_im = _s5_scan(z, prep, zeros_s5, zeros_s5, ctx=True)
        st = lambda s: s[:, l].reshape(DEC_BATCH, 2, S5_SG, 1, S5_SGW)
        y_l, _, _ = _s5_scan(z, prep, st(state_s5_re), st(state_s5_im), ctx=False)
        s5_args = ((y_c, y_l), s5_d[l], s5_w_glu[l], s5_b_glu[l])

        mg = z[:, Z_MG:Z_MG + 16].reshape(T_ALL, 2, 2, nh)
        gcol = mg.transpose(3, 0, 1, 2).reshape(nh, T_ALL, 4)
        grow = mg.reshape(T_ALL // CHUNK, CHUNK, 4, nh).transpose(3, 0, 2, 1)
        bias = jnp.stack([mlstm_i_bias[l], mlstm_f_bias[l]], axis=1)
        bias_c = bias.transpose(2, 0, 1).reshape(nh, 1, 4)
        bias_r = bias_c.reshape(nh, 4, 1)
        ml_c_out, c_new, n_new, m_new = _mlstm(z, gcol, grow, bias_c, bias_r, zeros_c, zeros_n, zeros_m,
                                               mlstm_norm_g[l], ctx=True)
        ml_l_out, _, _, _ = _mlstm(z, gcol, grow, bias_c, bias_r, state_mlstm_c[:, l],
                                   state_mlstm_n[:, l].reshape(DEC_BATCH, 2, nh, 1, MLSTM_DIM),
                                   state_mlstm_m[:, l].reshape(DEC_BATCH, 2, nh, 1, 1),
                                   mlstm_norm_g[l], ctx=False)

        gm_args = (gmlp_norm_g[l], gmlp_w_s[l], gmlp_b_s[l])

        ctx_states.append((k_new, v_new,
                           hf_re.reshape(BATCH, 2, S5_GROUPS, S5_STATE), hf_im.reshape(BATCH, 2, S5_GROUPS, S5_STATE),
                           c_new, n_new.reshape(BATCH, 2, nh, MLSTM_DIM), m_new.reshape(BATCH, 2, nh)))

        w_o = w_out[l].astype(BF16)
        final = l == DEPTH - 1
        mixed = (z, (att_c, att_l), s5_args, (ml_c_out, ml_l_out), gm_args)
        if use_moe:
            x1, h2, route = _out_proj(*mixed, x, mod[l], norm2_g[l], w_o,
                                      router=(moe_router[j], moe_router_bias[j]))
            src, slot, tile_e, tile_ns = _route_plan(route)
            ys = _moe_ffn(h2, tile_e, tile_ns, src, moe_w_gate[j].astype(BF16), moe_w_up[j].astype(BF16),
                          moe_w_down[j].astype(BF16))
            x = _residual(x1, ys, mod[l], final_norm_g, route=route, slot=slot, final=final)
        else:
            x1, h2 = _out_proj(*mixed, x, mod[l], norm2_g[l], w_o)
            y = _ffn(h2, ffn_w_gate[j].astype(BF16), ffn_w_up[j].astype(BF16), ffn_w_down[j].astype(BF16))
            if final:
                x = _residual(x1, y, mod[l], final_norm_g, final=True)
            else:
                x = (x1, y, mod[l])

    y_prompt = x[0].reshape(BATCH, SEQ, D_MODEL)
    y_sample = x[1].reshape(DEC_BATCH, DEC_SEQ, D_MODEL)
    stack = lambda i: jnp.stack([s[i] for s in ctx_states], axis=1)
    return (y_prompt, y_sample, stack(0), stack(1), stack(2), stack(3), stack(4), stack(5), stack(6))
```
